```python
import jax, jax.numpy as jnp
from jax import lax
import numpy as np

D_MODEL = 1024
BATCH = 32
SEQ = 2048
DEPTH = 1

CHUNK = 64
EPS = 1e-6
RWKV_HEAD = 64
RWKV_HEADS = 8
RWKV_WIDTH = RWKV_HEADS * RWKV_HEAD
DECAY_LORA = 64
ICLR_LORA = 64
GATE_LORA = 128
GN_EPS = 64e-5
MLSTM_HEADS = 4
MLSTM_DK = 64
MLSTM_DV = 128
MLSTM_QK = MLSTM_HEADS * MLSTM_DK
MLSTM_WIDTH = MLSTM_HEADS * MLSTM_DV
CONV_WIDTH = 4
GATE_SOFTCAP = 15.0
MIX_WIDTH = RWKV_WIDTH + MLSTM_WIDTH
RWKV_COLS = 3 * RWKV_WIDTH + DECAY_LORA + ICLR_LORA + GATE_LORA
MLSTM_COLS = 2 * MLSTM_QK + 2 * MLSTM_WIDTH + 2 * MLSTM_HEADS
IN_COLS = RWKV_COLS + MLSTM_COLS
N_GROUPS = 4
EXPERTS_PER_GROUP = 8
N_EXPERTS = N_GROUPS * EXPERTS_PER_GROUP
TOP_K = 2
D_EXPERT = 256

kernel_name = 'hybrid_rwkv7_mlstm_hmoe_block'


def rms_norm(x, w):
    xf = x.astype(jnp.float32)
    y = xf * lax.rsqrt(jnp.mean(xf * xf, axis=-1, keepdims=True) + EPS)
    return (y * w.astype(jnp.float32)).astype(x.dtype)


def modulate(xn, shift, scale):
    return xn * (1.0 + scale[:, None, :]) + shift[:, None, :]


def token_shift(u, mu):
    prev = jnp.pad(u, ((0, 0), (1, 0), (0, 0)))[:, :-1]
    return u + (prev - u) * mu


def causal_dwconv(u, w, b):
    ch = u.shape[-1]
    out = lax.conv_general_dilated(
        u, w[:, None, :].astype(u.dtype), window_strides=(1,),
        padding=[(w.shape[0] - 1, 0)], dimension_numbers=('NWC', 'WIO', 'NWC'),
        feature_group_count=ch)
    return out + b


def softcap(z):
    return GATE_SOFTCAP * jnp.tanh(z / GATE_SOFTCAP)


def rwkv7_group(u, mu, w0, w_up, a0, a_up, g_up, k_k, k_a, r_k, gn_w, gn_b):
    bsz, seq, _ = u.shape
    f32 = jnp.float32
    u = token_shift(u, mu)
    o1 = RWKV_WIDTH
    o2 = 2 * RWKV_WIDTH
    o3 = 3 * RWKV_WIDTH
    o4 = o3 + DECAY_LORA
    o5 = o4 + ICLR_LORA
    r, k, v, wd, ad, gd = jnp.split(u, [o1, o2, o3, o4, o5], axis=-1)
    w_log = -jax.nn.softplus(-(w0 + jnp.tanh(wd) @ w_up)) - 0.5
    decay = jnp.exp(-jnp.exp(w_log.astype(f32)))
    a = jax.nn.sigmoid(a0 + ad @ a_up)
    g = jax.nn.sigmoid(gd) @ g_up
    heads = lambda t: t.reshape(bsz, seq, RWKV_HEADS, RWKV_HEAD).astype(f32)
    kk = heads(k * k_k)
    kk = kk / jnp.maximum(jnp.sqrt(jnp.sum(kk * kk, axis=-1, keepdims=True)), 1e-12)
    k = k * (1.0 + (a - 1.0) * k_a)
    r_h, k_h, v_h, w_h, a_h = heads(r), heads(k), heads(v), heads(decay), heads(a)
    b_h = kk * a_h

    def step(state, inp):
        r_t, w_t, k_t, v_t, kk_t, b_t = inp
        sa = jnp.einsum('bhij,bhj->bhi', state, -kk_t)
        state = (state * w_t[:, :, None, :] + sa[..., None] * b_t[:, :, None, :]
                 + v_t[..., None] * k_t[:, :, None, :])
        return state, jnp.einsum('bhij,bhj->bhi', state, r_t)

    xs = tuple(jnp.swapaxes(t, 0, 1) for t in (r_h, w_h, k_h, v_h, kk, b_h))
    state0 = jnp.zeros((bsz, RWKV_HEADS, RWKV_HEAD, RWKV_HEAD), f32)
    _, y = lax.scan(step, state0, xs)
    y = jnp.swapaxes(y, 0, 1)
    mean = jnp.mean(y, axis=-1, keepdims=True)
    var = jnp.mean(jnp.square(y - mean), axis=-1, keepdims=True)
    y = (y - mean) * lax.rsqrt(var + GN_EPS)
    y = y.reshape(bsz, seq, RWKV_WIDTH) * gn_w + gn_b
    bonus = jnp.sum(r_h * k_h * r_k.astype(f32), axis=-1, keepdims=True) * v_h
    y = (y + bonus.reshape(bsz, seq, RWKV_WIDTH)) * g
    return y.astype(u.dtype)


def to_chunks(t):
    bsz, seq, nh = t.shape[:3]
    t = t.reshape((bsz, seq // CHUNK, CHUNK, nh) + t.shape[3:])
    return t.transpose((1, 0, 3, 2) + tuple(range(4, t.ndim)))


def mlstm_chunkwise(q, k, v, log_i, log_f):
    bsz, seq = q.shape[:2]
    f32 = jnp.float32
    causal = jnp.tril(jnp.ones((CHUNK, CHUNK), dtype=bool))

    def step(carry, inp):
        c_mat, n_vec, m = carry
        qc, kc, vc, li, lf = inp
        b = jnp.cumsum(lf, axis=-1)
        d_mat = jnp.where(causal, b[..., :, None] - b[..., None, :] + li[..., None, :], -jnp.inf)
        inter = b + m[..., None]
        m_t = jnp.maximum(inter, jnp.max(d_mat, axis=-1))
        scores = jnp.einsum('bhtd,bhsd->bhts', qc, kc) * jnp.exp(d_mat - m_t[..., None])
        w_inter = jnp.exp(inter - m_t)
        num = (jnp.einsum('bhts,bhsv->bhtv', scores, vc)
               + w_inter[..., None] * jnp.einsum('bhtd,bhdv->bhtv', qc, c_mat))
        den = jnp.sum(scores, axis=-1) + w_inter * jnp.einsum('bhtd,bhd->bht', qc, n_vec)
        h = num / jnp.maximum(jnp.abs(den), jnp.exp(-m_t))[..., None]
        end_log = b[..., -1:] - b + li
        m_new = jnp.maximum(b[..., -1] + m, jnp.max(end_log, axis=-1))
        carry_w = jnp.exp(b[..., -1] + m - m_new)
        kw = kc * jnp.exp(end_log - m_new[..., None])[..., None]
        c_mat = carry_w[..., None, None] * c_mat + jnp.einsum('bhsd,bhsv->bhdv', kw, vc)
        n_vec = carry_w[..., None] * n_vec + jnp.sum(kw, axis=-2)
        return (c_mat, n_vec, m_new), h

    carry0 = (jnp.zeros((bsz, MLSTM_HEADS, MLSTM_DK, MLSTM_DV), f32),
              jnp.zeros((bsz, MLSTM_HEADS, MLSTM_DK), f32),
              jnp.zeros((bsz, MLSTM_HEADS), f32))
    xs = (to_chunks(q * (MLSTM_DK ** -0.5)), to_chunks(k), to_chunks(v),
          to_chunks(log_i), to_chunks(log_f))
    _, h = lax.scan(step, carry0, xs)
    return h.transpose(1, 0, 3, 2, 4).reshape(bsz, seq, MLSTM_HEADS, MLSTM_DV)


def mlstm_group(u, conv_w, conv_b, i_b, f_b, hn_w):
    bsz, seq, _ = u.shape
    f32 = jnp.float32
    o1 = 2 * MLSTM_QK
    o2 = o1 + MLSTM_WIDTH
    o3 = o2 + MLSTM_WIDTH
    o4 = o3 + MLSTM_HEADS
    qk, v, o, ig, fg = jnp.split(u, [o1, o2, o3, o4], axis=-1)
    qk = jax.nn.silu(causal_dwconv(qk, conv_w, conv_b))
    q, k = jnp.split(qk, 2, axis=-1)
    log_i = softcap((ig + i_b).astype(f32))
    log_f = jax.nn.log_sigmoid(softcap((fg + f_b).astype(f32)))
    heads_k = lambda t: t.reshape(bsz, seq, MLSTM_HEADS, MLSTM_DK).astype(f32)
    h = mlstm_chunkwise(heads_k(q), heads_k(k),
                        v.reshape(bsz, seq, MLSTM_HEADS, MLSTM_DV).astype(f32), log_i, log_f)
    h = h * lax.rsqrt(jnp.mean(h * h, axis=-1, keepdims=True) + EPS)
    h = h.reshape(bsz, seq, MLSTM_WIDTH) * hn_w
    return (h * jax.nn.sigmoid(o.astype(f32))).astype(u.dtype)


def hier_moe(xn, w_group, b_group, w_router, b_router, w_gate, w_up, w_down):
    bsz, seq, _ = xn.shape
    f32 = jnp.float32
    g_logits = (xn @ w_group + b_group).astype(f32)
    g_prob = jax.nn.softmax(g_logits, axis=-1)
    g_idx = jnp.argmax(g_logits, axis=-1)
    g_onehot = jax.nn.one_hot(g_idx, N_GROUPS, dtype=f32)
    g_w = jnp.max(g_prob, axis=-1)
    e_logits = (xn @ w_router + b_router).astype(f32).reshape(bsz, seq, N_GROUPS, EXPERTS_PER_GROUP)
    sel = jnp.sum(e_logits * g_onehot[..., None], axis=2)
    top_v, top_i = lax.top_k(sel, TOP_K)
    top_w = jax.nn.softmax(top_v, axis=-1)
    within = jnp.sum(jax.nn.one_hot(top_i, EXPERTS_PER_GROUP, dtype=f32) * top_w[..., None], axis=-2)
    combine = (g_onehot[..., None] * within[..., None, :] * g_w[..., None, None])
    combine = combine.reshape(bsz, seq, N_EXPERTS).astype(xn.dtype)

    def per_sequence(args):
        xs, cw = args
        hg = jnp.einsum('sd,edf->sef', xs, w_gate)
        hu = jnp.einsum('sd,edf->sef', xs, w_up)
        act = jax.nn.silu(hg) * hu * cw[..., None]
        return jnp.einsum('sef,efd->sd', act, w_down)

    return lax.map(per_sequence, (xn, combine))


def setup_inputs(seed: int = 0) -> dict:
    key = jax.random.key(seed)
    ks = jax.random.split(key, 32)
    L, D = DEPTH, D_MODEL

    def nrm(i, shape, scale):
        return scale * jax.random.normal(ks[i], shape, jnp.float32)

    def uni(i, shape, lo, hi):
        return jax.random.uniform(ks[i], shape, jnp.float32, lo, hi)

    return {
        'x': nrm(0, (BATCH, SEQ, D), 1.0),
        'c': nrm(1, (BATCH, D), 1.0),
        'ada_w': nrm(2, (L, D, 6 * D), 0.5 * D ** -0.5),
        'ada_b': nrm(3, (L, 6 * D), 0.01),
        'mix_norm_w': 1.0 + nrm(4, (L, D), 0.02),
        'w_in': nrm(5, (L, D, IN_COLS), D ** -0.5),
        'rwkv_mu': uni(6, (L, RWKV_COLS), 0.0, 1.0),
        'rwkv_w0': uni(7, (L, RWKV_WIDTH), -4.0, 0.0),
        'rwkv_w_up': nrm(8, (L, DECAY_LORA, RWKV_WIDTH), DECAY_LORA ** -0.5),
        'rwkv_a0': nrm(9, (L, RWKV_WIDTH), 0.5),
        'rwkv_a_up': nrm(10, (L, ICLR_LORA, RWKV_WIDTH), ICLR_LORA ** -0.5),
        'rwkv_g_up': nrm(11, (L, GATE_LORA, RWKV_WIDTH), GATE_LORA ** -0.5),
        'rwkv_k_k': 0.85 + nrm(12, (L, RWKV_WIDTH), 0.05),
        'rwkv_k_a': 1.0 + nrm(13, (L, RWKV_WIDTH), 0.05),
        'rwkv_r_k': nrm(14, (L, RWKV_HEADS, RWKV_HEAD), 0.1),
        'rwkv_gn_w': 1.0 + nrm(15, (L, RWKV_WIDTH), 0.02),
        'rwkv_gn_b': nrm(16, (L, RWKV_WIDTH), 0.01),
        'mlstm_conv_w': nrm(17, (L, CONV_WIDTH, 2 * MLSTM_QK), CONV_WIDTH ** -0.5),
        'mlstm_conv_b': nrm(18, (L, 2 * MLSTM_QK), 0.01),
        'mlstm_i_b': -3.0 + nrm(19, (L, MLSTM_HEADS), 0.1),
        'mlstm_f_b': jnp.linspace(3.0, 6.0, MLSTM_HEADS, dtype=jnp.float32)[None, :] + nrm(20, (L, MLSTM_HEADS), 0.1),
        'mlstm_hn_w': 1.0 + nrm(21, (L, MLSTM_WIDTH), 0.02),
        'w_out': nrm(22, (L, MIX_WIDTH, D), MIX_WIDTH ** -0.5),
        'ffn_norm_w': 1.0 + nrm(23, (L, D), 0.02),
        'moe_w_group': nrm(24, (L, D, N_GROUPS), D ** -0.5),
        'moe_b_group': nrm(25, (L, N_GROUPS), 0.01),
        'moe_w_router': nrm(26, (L, D, N_EXPERTS), D ** -0.5),
        'moe_b_router': nrm(27, (L, N_EXPERTS), 0.01),
        'moe_w_gate': nrm(28, (L, N_EXPERTS, D, D_EXPERT), D ** -0.5),
        'moe_w_up': nrm(29, (L, N_EXPERTS, D, D_EXPERT), D ** -0.5),
        'moe_w_down': nrm(30, (L, N_EXPERTS, D_EXPERT, D), D_EXPERT ** -0.5),
        'final_norm_w': 1.0 + nrm(31, (D,), 0.02),
    }


def reference(x, c, ada_w, ada_b, mix_norm_w, w_in, rwkv_mu, rwkv_w0, rwkv_w_up, rwkv_a0,
              rwkv_a_up, rwkv_g_up, rwkv_k_k, rwkv_k_a, rwkv_r_k, rwkv_gn_w, rwkv_gn_b,
              mlstm_conv_w, mlstm_conv_b, mlstm_i_b, mlstm_f_b, mlstm_hn_w, w_out, ffn_norm_w,
              moe_w_group, moe_b_group, moe_w_router, moe_b_router, moe_w_gate, moe_w_up,
              moe_w_down, final_norm_w):
    for l in range(DEPTH):
        mod = jax.nn.silu(c) @ ada_w[l] + ada_b[l]
        sh_m, sc_m, g_m, sh_f, sc_f, g_f = jnp.split(mod, 6, axis=-1)
        h = modulate(rms_norm(x, mix_norm_w[l]), sh_m, sc_m)
        proj = h @ w_in[l]
        y_r = rwkv7_group(proj[..., :RWKV_COLS], rwkv_mu[l], rwkv_w0[l], rwkv_w_up[l], rwkv_a0[l],
                          rwkv_a_up[l], rwkv_g_up[l], rwkv_k_k[l], rwkv_k_a[l], rwkv_r_k[l],
                          rwkv_gn_w[l], rwkv_gn_b[l])
        y_m = mlstm_group(proj[..., RWKV_COLS:], mlstm_conv_w[l], mlstm_conv_b[l], mlstm_i_b[l],
                          mlstm_f_b[l], mlstm_hn_w[l])
        y = jnp.concatenate([y_r, y_m], axis=-1) @ w_out[l]
        x = x + g_m[:, None, :] * y
        h = modulate(rms_norm(x, ffn_norm_w[l]), sh_f, sc_f)
        y = hier_moe(h, moe_w_group[l], moe_b_group[l], moe_w_router[l], moe_b_router[l],
                     moe_w_gate[l], moe_w_up[l], moe_w_down[l])
        x = x + g_f[:, None, :] * y
    return rms_norm(x, final_norm_w)
```

```python
import functools

import jax
import jax.numpy as jnp
from jax import lax
from jax.experimental import pallas as pl
from jax.experimental.pallas import tpu as pltpu

F32 = jnp.float32
BF16 = jnp.bfloat16
HIGHEST = lax.Precision.HIGHEST

CHUNK = 64
EPS = 1e-6
RWKV_HEAD = 64
RWKV_HEADS = 8
RWKV_WIDTH = RWKV_HEADS * RWKV_HEAD
DECAY_LORA = 64
ICLR_LORA = 64
GATE_LORA = 128
GN_EPS = 64e-5
MLSTM_HEADS = 4
MLSTM_DK = 64
MLSTM_DV = 128
MLSTM_QK = MLSTM_HEADS * MLSTM_DK
MLSTM_WIDTH = MLSTM_HEADS * MLSTM_DV
CONV_WIDTH = 4
GATE_SOFTCAP = 15.0
RWKV_COLS = 3 * RWKV_WIDTH + DECAY_LORA + ICLR_LORA + GATE_LORA
N_GROUPS = 4
EXPERTS_PER_GROUP = 8
N_EXPERTS = N_GROUPS * EXPERTS_PER_GROUP
LANES = 128
MLSTM_MAIN = 2 * MLSTM_QK + 2 * MLSTM_WIDTH
MLSTM_PAD_COLS = MLSTM_MAIN + LANES
VMEM_LIMIT = 56 * 1024 * 1024


def _mm(a, b):
    return jnp.dot(a.astype(BF16), b.astype(BF16), preferred_element_type=F32)


def _nt(a, b):
    return lax.dot_general(a.astype(BF16), b.astype(BF16), (((1,), (1,)), ((), ())),
                           preferred_element_type=F32)


def _tn(a, b):
    return lax.dot_general(a.astype(BF16), b.astype(BF16), (((0,), (0,)), ((), ())),
                           preferred_element_type=F32)


def _mm_f32(a, b):
    return jnp.dot(a, b, preferred_element_type=F32, precision=HIGHEST)


def _sigmoid(z):
    return 1.0 / (1.0 + jnp.exp(-z))


def _softplus(z):
    return jnp.maximum(z, 0.0) + jnp.log1p(jnp.exp(-jnp.abs(z)))


def _params(*sem):
    return pltpu.CompilerParams(dimension_semantics=sem, vmem_limit_bytes=VMEM_LIMIT)


def _ada_kernel(c_ref, w_ref, b_ref, o_ref):
    c = c_ref[...]
    o_ref[...] = _mm_f32(c * _sigmoid(c), w_ref[...]) + b_ref[...]


def _ada(c, w, b):
    bsz, d = c.shape
    n = w.shape[1]
    return pl.pallas_call(
        _ada_kernel,
        grid=(n // d,),
        in_specs=[pl.BlockSpec((bsz, d), lambda j: (0, 0)),
                  pl.BlockSpec((d, d), lambda j: (0, j)),
                  pl.BlockSpec((1, d), lambda j: (0, j))],
        out_specs=pl.BlockSpec((bsz, d), lambda j: (0, j)),
        out_shape=jax.ShapeDtypeStruct((bsz, n), F32),
        compiler_params=_params("arbitrary"),
        name="ada",
    )(c, w, b.reshape(1, n))


def _norm_mod(x, nw, shift, scale):
    y = x * lax.rsqrt(jnp.mean(x * x, axis=-1, keepdims=True) + EPS) * nw
    return y * (1.0 + scale) + shift


def _in_proj_kernel(x_ref, mod_ref, nw_ref, wr_ref, wm_ref, pr_ref, pm_ref):
    h = _norm_mod(x_ref[...], nw_ref[...], mod_ref[0:1, :], mod_ref[1:2, :]).astype(BF16)
    pr_ref[...] = jnp.dot(h, wr_ref[...], preferred_element_type=F32)
    pm_ref[...] = jnp.dot(h, wm_ref[...], preferred_element_type=F32)


def _in_proj(x, mod, nw, w_r, w_m, tm):
    bsz, seq, d = x.shape
    nr, nm = w_r.shape[1], w_m.shape[1]
    return pl.pallas_call(
        _in_proj_kernel,
        grid=(bsz, seq // tm),
        in_specs=[pl.BlockSpec((None, tm, d), lambda b, i: (b, i, 0)),
                  pl.BlockSpec((None, 6, d), lambda b, i: (b, 0, 0)),
                  pl.BlockSpec((1, d), lambda b, i: (0, 0)),
                  pl.BlockSpec((d, nr), lambda b, i: (0, 0)),
                  pl.BlockSpec((d, nm), lambda b, i: (0, 0))],
        out_specs=[pl.BlockSpec((None, tm, nr), lambda b, i: (b, i, 0)),
                   pl.BlockSpec((None, tm, nm), lambda b, i: (b, i, 0))],
        out_shape=[jax.ShapeDtypeStruct((bsz, seq, nr), F32),
                   jax.ShapeDtypeStruct((bsz, seq, nm), F32)],
        compiler_params=_params("arbitrary", "arbitrary"),
        name="in_proj",
    )(x, mod, nw, w_r, w_m)


def _rwkv_kernel(pr_ref, mu_ref, w0_ref, wup_ref, a0_ref, aup_ref, gup_ref, kk_ref, ka_ref,
                 rk_ref, gnw_ref, gnb_ref, o_ref, s_scr, prev_scr):
    C = CHUNK
    N = RWKV_HEAD

    @pl.when(pl.program_id(1) == 0)
    def _():
        s_scr[...] = jnp.zeros_like(s_scr)
        prev_scr[...] = jnp.zeros_like(prev_scr)

    x = pr_ref[...]
    row1 = lax.broadcasted_iota(jnp.int32, x.shape, 0)
    prev = jnp.where(row1 == 0, prev_scr[...], pltpu.roll(x, 1, axis=0))
    prev_scr[...] = x[C - 1:C, :]
    u = x + (prev - x) * mu_ref[...]

    o1, o2, o3 = RWKV_WIDTH, 2 * RWKV_WIDTH, 3 * RWKV_WIDTH
    o4 = o3 + DECAY_LORA
    o5 = o4 + ICLR_LORA
    r, k, v = u[:, :o1], u[:, o1:o2], u[:, o2:o3]
    wd, ad, gd = u[:, o3:o4], u[:, o4:o5], u[:, o5:]
    w_log = -_softplus(-(w0_ref[...] + _mm(jnp.tanh(wd), wup_ref[...]))) - 0.5
    lw = -jnp.exp(w_log)
    a = _sigmoid(a0_ref[...] + _mm(ad, aup_ref[...]))
    g = _mm(_sigmoid(gd), gup_ref[...])

    row = lax.broadcasted_iota(jnp.int32, (C, C), 0)
    col = lax.broadcasted_iota(jnp.int32, (C, C), 1)
    strict = col < row
    incl = col <= row
    same16 = (row >> 4) == (col >> 4)
    same32 = (row >> 5) == (col >> 5)
    eye = (row == col).astype(F32)
    cum = _mm_f32(incl.astype(F32), lw)

    for h in range(RWKV_HEADS):
        sl = slice(h * N, (h + 1) * N)
        r_h, k_h, v_h, a_h, lw_h, cum_h = r[:, sl], k[:, sl], v[:, sl], a[:, sl], lw[:, sl], cum[:, sl]
        kk = k_h * kk_ref[:, sl]
        kk = kk / jnp.maximum(jnp.sqrt(jnp.sum(kk * kk, axis=-1, keepdims=True)), 1e-12)
        k2 = k_h * (1.0 + (a_h - 1.0) * ka_ref[:, sl])
        b_h = kk * a_h
        cl = cum_h[C - 1:C, :]
        p_inv = jnp.exp(-cum_h)
        p_rest = jnp.exp(cl - cum_h)
        ah = -kk * jnp.exp(cum_h - lw_h)
        rh = r_h * jnp.exp(cum_h)
        bt, kt = b_h * p_inv, k2 * p_inv
        l_ab = jnp.where(strict, _nt(ah, bt), 0.0)
        l_ak = jnp.where(strict, _nt(ah, kt), 0.0)
        m_rb = jnp.where(incl, _nt(rh, bt), 0.0)
        m_rk = jnp.where(incl, _nt(rh, kt), 0.0)
        n1 = jnp.where(same16, l_ab, 0.0)
        xi = eye + n1
        n2 = _mm(n1, n1)
        xi = xi + _mm(xi, n2)
        n4 = _mm(n2, n2)
        xi = xi + _mm(xi, n4)
        n8 = _mm(n4, n4)
        xi = xi + _mm(xi, n8)
        e1 = jnp.where(same32 & jnp.logical_not(same16), l_ab, 0.0)
        xi = xi + _mm(_mm(xi, e1), xi)
        e2 = jnp.where(same32, 0.0, l_ab)
        xi = xi + _mm(_mm(xi, e2), xi)
        a2 = _mm(xi, ah)
        u0 = _mm(xi, _mm(l_ak, v_h))
        s0 = s_scr[h]
        y = _nt(rh + _mm(m_rb, a2), s0) + _mm(m_rb, u0) + _mm(m_rk, v_h)
        s_scr[h] = (s0 * jnp.exp(cl) + _mm(s0, _tn(a2, b_h * p_rest))
                    + _tn(u0, b_h * p_rest) + _tn(v_h, k2 * p_rest))
        mean = jnp.mean(y, axis=-1, keepdims=True)
        yc = y - mean
        var = jnp.mean(yc * yc, axis=-1, keepdims=True)
        yn = yc * lax.rsqrt(var + GN_EPS) * gnw_ref[:, sl] + gnb_ref[:, sl]
        bonus = jnp.sum(r_h * k2 * rk_ref[:, sl], axis=-1, keepdims=True) * v_h
        o_ref[:, sl] = (yn + bonus) * g[:, sl]


def _rwkv(pr, mu, w0, w_up, a0, a_up, g_up, k_k, k_a, r_k, gn_w, gn_b):
    bsz, seq, cols = pr.shape
    row = lambda p: p.reshape(1, -1)
    vec = lambda n: pl.BlockSpec((1, n), lambda b, c: (0, 0))
    mat = lambda m: pl.BlockSpec(m.shape, lambda b, c: (0, 0))
    return pl.pallas_call(
        _rwkv_kernel,
        grid=(bsz, seq // CHUNK),
        in_specs=[pl.BlockSpec((None, CHUNK, cols), lambda b, c: (b, c, 0)),
                  vec(cols), vec(RWKV_WIDTH), mat(w_up), vec(RWKV_WIDTH), mat(a_up), mat(g_up),
                  vec(RWKV_WIDTH), vec(RWKV_WIDTH), vec(RWKV_WIDTH), vec(RWKV_WIDTH), vec(RWKV_WIDTH)],
        out_specs=pl.BlockSpec((None, CHUNK, RWKV_WIDTH), lambda b, c: (b, c, 0)),
        out_shape=jax.ShapeDtypeStruct((bsz, seq, RWKV_WIDTH), F32),
        scratch_shapes=[pltpu.VMEM((RWKV_HEADS, RWKV_HEAD, RWKV_HEAD), F32),
                        pltpu.VMEM((1, cols), F32)],
        compiler_params=_params("arbitrary", "arbitrary"),
        name="rwkv",
    )(pr, row(mu), row(w0), w_up, row(a0), a_up, g_up, row(k_k), row(k_a), row(r_k), row(gn_w), row(gn_b))


def _mlstm_kernel(pm_ref, cw_ref, cb_ref, gb_ref, hn_ref, o_ref, ext_scr, c_scr, n_scr, m_scr):
    C = CHUNK
    DK, DV = MLSTM_DK, MLSTM_DV
    QK2 = 2 * MLSTM_QK

    @pl.when(pl.program_id(1) == 0)
    def _():
        ext_scr[0:8, :] = jnp.zeros((8, QK2), F32)
        c_scr[...] = jnp.zeros_like(c_scr)
        n_scr[...] = jnp.zeros_like(n_scr)
        m_scr[...] = jnp.zeros_like(m_scr)

    ext_scr[8:8 + C, :] = pm_ref[:, 0:QK2]
    conv = cb_ref[...] + cw_ref[0:1, :] * ext_scr[5:5 + C, :]
    for i in range(1, CONV_WIDTH):
        conv = conv + cw_ref[i:i + 1, :] * ext_scr[5 + i:5 + i + C, :]
    ext_scr[0:8, :] = ext_scr[C:C + 8, :]
    qk = conv * _sigmoid(conv)
    q = qk[:, :MLSTM_QK] * (DK ** -0.5)
    k = qk[:, MLSTM_QK:]
    v = pm_ref[:, QK2:QK2 + MLSTM_WIDTH]
    og = pm_ref[:, QK2 + MLSTM_WIDTH:MLSTM_MAIN]

    z = GATE_SOFTCAP * jnp.tanh((pm_ref[:, MLSTM_MAIN:] + gb_ref[...]) / GATE_SOFTCAP)
    log_f = jnp.minimum(z, 0.0) - jnp.log1p(jnp.exp(-jnp.abs(z)))
    row = lax.broadcasted_iota(jnp.int32, (C, C), 0)
    col = lax.broadcasted_iota(jnp.int32, (C, C), 1)
    incl = col <= row
    lane = lax.broadcasted_iota(jnp.int32, (C, LANES), 1)
    gates = jnp.where(lane < MLSTM_HEADS, z, _mm_f32(incl.astype(F32), log_f))
    ri = lax.broadcasted_iota(jnp.int32, (LANES, LANES), 0)
    ci = lax.broadcasted_iota(jnp.int32, (LANES, LANES), 1)
    gates_t = lax.dot_general((ri == ci).astype(F32), gates, (((1,), (1,)), ((), ())),
                              preferred_element_type=F32, precision=HIGHEST)

    for h in range(MLSTM_HEADS):
        li_col = gates[:, h:h + 1]
        b_col = gates[:, MLSTM_HEADS + h:MLSTM_HEADS + h + 1]
        li_row = gates_t[h:h + 1, :]
        b_row = gates_t[MLSTM_HEADS + h:MLSTM_HEADS + h + 1, :]
        m = m_scr[h][0:1, 0:1]
        q_h = q[:, h * DK:(h + 1) * DK]
        k_h = k[:, h * DK:(h + 1) * DK]
        v_h = v[:, h * DV:(h + 1) * DV]
        d_mat = jnp.where(incl, b_col - b_row + li_row, -jnp.inf)
        inter = b_col + m
        m_t = jnp.maximum(inter, jnp.max(d_mat, axis=-1, keepdims=True))
        scores = _nt(q_h, k_h) * jnp.exp(d_mat - m_t)
        w_inter = jnp.exp(inter - m_t)
        c_mat = c_scr[h]
        n_vec = n_scr[h][0:1, 0:DK]
        num = _mm(scores, v_h) + w_inter * _mm(q_h, c_mat)
        den = jnp.sum(scores, axis=-1, keepdims=True) + w_inter * jnp.sum(q_h * n_vec, axis=-1, keepdims=True)
        hh = num / jnp.maximum(jnp.abs(den), jnp.exp(-m_t))
        b_last = b_col[C - 1:C, :]
        end_log = b_last - b_col + li_col
        m_new = jnp.maximum(b_last + m, jnp.max(end_log, axis=0, keepdims=True))
        carry_w = jnp.exp(b_last + m - m_new)
        kw = k_h * jnp.exp(end_log - m_new)
        c_scr[h] = carry_w * c_mat + _tn(kw, v_h)
        n_new = carry_w * n_vec + jnp.sum(kw, axis=0, keepdims=True)
        n_scr[h] = jnp.broadcast_to(jnp.concatenate([n_new, jnp.zeros((1, LANES - DK), F32)], axis=1), (8, LANES))
        m_scr[h] = jnp.broadcast_to(m_new, (8, LANES))
        hn = hh * lax.rsqrt(jnp.mean(hh * hh, axis=-1, keepdims=True) + EPS) * hn_ref[:, h * DV:(h + 1) * DV]
        o_ref[:, h * DV:(h + 1) * DV] = hn * _sigmoid(og[:, h * DV:(h + 1) * DV])


def _mlstm(pm, conv_w, conv_b, i_b, f_b, hn_w):
    bsz, seq, cols = pm.shape
    gate_b = jnp.zeros((1, LANES), F32).at[0, :MLSTM_HEADS].set(i_b).at[0, MLSTM_HEADS:2 * MLSTM_HEADS].set(f_b)
    full = lambda a: pl.BlockSpec(a.shape, lambda b, c: (0, 0))
    conv_b = conv_b.reshape(1, -1)
    hn_w = hn_w.reshape(1, -1)
    return pl.pallas_call(
        _mlstm_kernel,
        grid=(bsz, seq // CHUNK),
        in_specs=[pl.BlockSpec((None, CHUNK, cols), lambda b, c: (b, c, 0)),
                  full(conv_w), full(conv_b), full(gate_b), full(hn_w)],
        out_specs=pl.BlockSpec((None, CHUNK, MLSTM_WIDTH), lambda b, c: (b, c, 0)),
        out_shape=jax.ShapeDtypeStruct((bsz, seq, MLSTM_WIDTH), F32),
        scratch_shapes=[pltpu.VMEM((CHUNK + 8, 2 * MLSTM_QK), F32),
                        pltpu.VMEM((MLSTM_HEADS, MLSTM_DK, MLSTM_DV), F32),
                        pltpu.VMEM((MLSTM_HEADS, 8, LANES), F32),
                        pltpu.VMEM((MLSTM_HEADS, 8, LANES), F32)],
        compiler_params=_params("arbitrary", "arbitrary"),
        name="mlstm",
    )(pm, conv_w, conv_b, gate_b, hn_w)


def _mix_out_kernel(x_ref, yr_ref, ym_ref, mod_ref, nw_ref, wor_ref, wom_ref, wg_ref, bg_ref, we_ref, be_ref,
                    x1_ref, h_ref, cw_ref):
    y = (jnp.dot(yr_ref[...].astype(BF16), wor_ref[...], preferred_element_type=F32)
         + jnp.dot(ym_ref[...].astype(BF16), wom_ref[...], preferred_element_type=F32))
    x1 = x_ref[...] + mod_ref[2:3, :] * y
    x1_ref[...] = x1
    h = _norm_mod(x1, nw_ref[...], mod_ref[3:4, :], mod_ref[4:5, :])
    h_ref[...] = h.astype(BF16)

    gl = _mm_f32(h, wg_ref[...]) + bg_ref[...]
    el = _mm_f32(h, we_ref[...]) + be_ref[...]
    lane = lax.broadcasted_iota(jnp.int32, gl.shape, 1)
    neg = -jnp.inf
    gl = jnp.where(lane < N_GROUPS, gl, neg)
    g_max = jnp.max(gl, axis=-1, keepdims=True)
    g_idx = jnp.min(jnp.where(gl == g_max, lane, LANES), axis=-1, keepdims=True)
    g_w = 1.0 / jnp.sum(jnp.exp(gl - g_max), axis=-1, keepdims=True)
    in_group = ((lane >> 3) == g_idx) & (lane < N_EXPERTS)
    sel = jnp.where(in_group, el, neg)
    v1 = jnp.max(sel, axis=-1, keepdims=True)
    i1 = jnp.min(jnp.where(sel == v1, lane, LANES), axis=-1, keepdims=True)
    rest = jnp.where(lane == i1, neg, sel)
    v2 = jnp.max(rest, axis=-1, keepdims=True)
    i2 = jnp.min(jnp.where(rest == v2, lane, LANES), axis=-1, keepdims=True)
    e2 = jnp.exp(v2 - v1)
    w1 = 1.0 / (1.0 + e2)
    w2 = e2 / (1.0 + e2)
    cw_ref[...] = jnp.where(lane == i1, w1 * g_w, jnp.where(lane == i2, w2 * g_w, 0.0))


def _mix_out(x, y_r, y_m, mod, nw, wo_r, wo_m, wg, bg, we, be, tm):
    bsz, seq, d = x.shape
    tok = lambda n: pl.BlockSpec((None, tm, n), lambda b, i: (b, i, 0))
    full = lambda a: pl.BlockSpec(a.shape, lambda b, i: (0, 0))
    return pl.pallas_call(
        _mix_out_kernel,
        grid=(bsz, seq // tm),
        in_specs=[tok(d), tok(RWKV_WIDTH), tok(MLSTM_WIDTH),
                  pl.BlockSpec((None, 6, d), lambda b, i: (b, 0, 0)),
                  full(nw), full(wo_r), full(wo_m), full(wg), full(bg), full(we), full(be)],
        out_specs=[tok(d), tok(d), tok(LANES)],
        out_shape=[jax.ShapeDtypeStruct((bsz, seq, d), F32),
                   jax.ShapeDtypeStruct((bsz, seq, d), BF16),
                   jax.ShapeDtypeStruct((bsz, seq, LANES), F32)],
        compiler_params=_params("arbitrary", "arbitrary"),
        name="mix_out",
    )(x, y_r, y_m, mod, nw, wo_r, wo_m, wg, bg, we, be)


def _moe_kernel(h_ref, cw_ref, x1_ref, mod_ref, wg_ref, wu_ref, wd_ref, fw_ref, o_ref, acc_scr):
    e = pl.program_id(2)

    @pl.when(e == 0)
    def _():
        acc_scr[...] = jnp.zeros_like(acc_scr)

    h = h_ref[...]
    hg = jnp.dot(h, wg_ref[...], preferred_element_type=F32)
    hu = jnp.dot(h, wu_ref[...], preferred_element_type=F32)
    lane = lax.broadcasted_iota(jnp.int32, cw_ref.shape, 1)
    cw = jnp.sum(jnp.where(lane == e, cw_ref[...], 0.0), axis=-1, keepdims=True)
    act = hg * _sigmoid(hg) * hu * cw
    acc_scr[...] += jnp.dot(act.astype(BF16), wd_ref[...], preferred_element_type=F32)

    @pl.when(e == pl.num_programs(2) - 1)
    def _():
        x2 = x1_ref[...] + mod_ref[5:6, :] * acc_scr[...]
        o_ref[...] = x2 * lax.rsqrt(jnp.mean(x2 * x2, axis=-1, keepdims=True) + EPS) * fw_ref[...]


def _moe(h, cw, x1, mod, w_gate, w_up, w_down, fw, tm):
    bsz, seq, d = x1.shape
    ne, _, f = w_gate.shape
    tok = lambda n: pl.BlockSpec((None, tm, n), lambda b, i, e: (b, i, 0))
    return pl.pallas_call(
        _moe_kernel,
        grid=(bsz, seq // tm, ne),
        in_specs=[tok(d), tok(LANES), tok(d),
                  pl.BlockSpec((None, 6, d), lambda b, i, e: (b, 0, 0)),
                  pl.BlockSpec((None, d, f), lambda b, i, e: (e, 0, 0)),
                  pl.BlockSpec((None, d, f), lambda b, i, e: (e, 0, 0)),
                  pl.BlockSpec((None, f, d), lambda b, i, e: (e, 0, 0)),
                  pl.BlockSpec((1, d), lambda b, i, e: (0, 0))],
        out_specs=tok(d),
        out_shape=jax.ShapeDtypeStruct((bsz, seq, d), F32),
        scratch_shapes=[pltpu.VMEM((tm, d), F32)],
        compiler_params=_params("arbitrary", "arbitrary", "arbitrary"),
        name="moe",
    )(h, cw, x1, mod, w_gate, w_up, w_down, fw)


def _pad_cols(w, n):
    return jnp.pad(w, ((0, 0), (0, n - w.shape[1])))


def kernel(x, c, ada_w, ada_b, mix_norm_w, w_in, rwkv_mu, rwkv_w0, rwkv_w_up, rwkv_a0, rwkv_a_up, rwkv_g_up, rwkv_k_k, rwkv_k_a, rwkv_r_k, rwkv_gn_w, rwkv_gn_b, mlstm_conv_w, mlstm_conv_b, mlstm_i_b, mlstm_f_b, mlstm_hn_w, w_out, ffn_norm_w, moe_w_group, moe_b_group, moe_w_router, moe_b_router, moe_w_gate, moe_w_up, moe_w_down, final_norm_w):
    bsz, seq, d = x.shape
    depth = ada_w.shape[0]
    assert seq % CHUNK == 0
    tm = min(512, seq)
    tm_moe = min(1024, seq)
    for l in range(depth):
        mod = _ada(c, ada_w[l], ada_b[l]).reshape(bsz, 6, d)
        w_r = w_in[l][:, :RWKV_COLS].astype(BF16)
        w_m = _pad_cols(w_in[l][:, RWKV_COLS:], MLSTM_PAD_COLS).astype(BF16)
        pr, pm = _in_proj(x, mod, mix_norm_w[l].reshape(1, d), w_r, w_m, tm)
        y_r = _rwkv(pr, rwkv_mu[l], rwkv_w0[l], rwkv_w_up[l], rwkv_a0[l], rwkv_a_up[l], rwkv_g_up[l],
                    rwkv_k_k[l], rwkv_k_a[l], rwkv_r_k[l], rwkv_gn_w[l], rwkv_gn_b[l])
        y_m = _mlstm(pm, mlstm_conv_w[l], mlstm_conv_b[l], mlstm_i_b[l], mlstm_f_b[l], mlstm_hn_w[l])
        wo = w_out[l].astype(BF16)
        x1, h, cw = _mix_out(
            x, y_r, y_m, mod, ffn_norm_w[l].reshape(1, d), wo[:RWKV_WIDTH], wo[RWKV_WIDTH:],
            _pad_cols(moe_w_group[l], LANES), _pad_cols(moe_b_group[l].reshape(1, -1), LANES),
            _pad_cols(moe_w_router[l], LANES), _pad_cols(moe_b_router[l].reshape(1, -1), LANES), tm)
        last = l == depth - 1
        fw = final_norm_w.reshape(1, d) if last else None
        assert last, "the fused final norm assumes a single layer"
        x = _moe(h, cw, x1, mod, moe_w_gate[l].astype(BF16), moe_w_up[l].astype(BF16),
                 moe_w_down[l].astype(BF16), fw, tm_moe)
    return x
```

```python
import functools

import jax
import jax.numpy as jnp
from jax import lax
from jax.experimental import pallas as pl
from jax.experimental.pallas import tpu as pltpu

F32 = jnp.float32
BF16 = jnp.bfloat16
HIGHEST = lax.Precision.HIGHEST

CHUNK = 64
EPS = 1e-6
RWKV_HEAD = 64
RWKV_HEADS = 8
RWKV_WIDTH = RWKV_HEADS * RWKV_HEAD
DECAY_LORA = 64
ICLR_LORA = 64
GATE_LORA = 128
GN_EPS = 64e-5
MLSTM_HEADS = 4
MLSTM_DK = 64
MLSTM_DV = 128
MLSTM_QK = MLSTM_HEADS * MLSTM_DK
MLSTM_WIDTH = MLSTM_HEADS * MLSTM_DV
CONV_WIDTH = 4
GATE_SOFTCAP = 15.0
RWKV_COLS = 3 * RWKV_WIDTH + DECAY_LORA + ICLR_LORA + GATE_LORA
N_GROUPS = 4
EXPERTS_PER_GROUP = 8
N_EXPERTS = N_GROUPS * EXPERTS_PER_GROUP
LANES = 128
MLSTM_MAIN = 2 * MLSTM_QK + 2 * MLSTM_WIDTH
MLSTM_PAD_COLS = MLSTM_MAIN + LANES
VMEM_LIMIT = 56 * 1024 * 1024


def _mm(a, b):
    return jnp.dot(a.astype(BF16), b.astype(BF16), preferred_element_type=F32)


def _nt(a, b):
    return lax.dot_general(a.astype(BF16), b.astype(BF16), (((1,), (1,)), ((), ())),
                           preferred_element_type=F32)


def _tn(a, b):
    return lax.dot_general(a.astype(BF16), b.astype(BF16), (((0,), (0,)), ((), ())),
                           preferred_element_type=F32)


def _mm_f32(a, b):
    return jnp.dot(a, b, preferred_element_type=F32, precision=HIGHEST)


def _sigmoid(z):
    return 1.0 / (1.0 + jnp.exp(-z))


def _softplus(z):
    return jnp.maximum(z, 0.0) + jnp.log1p(jnp.exp(-jnp.abs(z)))


def _params(*sem):
    return pltpu.CompilerParams(dimension_semantics=sem, vmem_limit_bytes=VMEM_LIMIT)


def _ada_kernel(c_ref, w_ref, b_ref, o_ref):
    c = c_ref[...]
    o_ref[...] = _mm_f32(c * _sigmoid(c), w_ref[...]) + b_ref[...]


def _ada(c, w, b):
    bsz, d = c.shape
    n = w.shape[1]
    return pl.pallas_call(
        _ada_kernel,
        grid=(n // d,),
        in_specs=[pl.BlockSpec((bsz, d), lambda j: (0, 0)),
                  pl.BlockSpec((d, d), lambda j: (0, j)),
                  pl.BlockSpec((1, d), lambda j: (0, j))],
        out_specs=pl.BlockSpec((bsz, d), lambda j: (0, j)),
        out_shape=jax.ShapeDtypeStruct((bsz, n), F32),
        compiler_params=_params("arbitrary"),
        name="ada",
    )(c, w, b.reshape(1, n))


def _norm_mod(x, nw, shift, scale):
    y = x * lax.rsqrt(jnp.mean(x * x, axis=-1, keepdims=True) + EPS) * nw
    return y * (1.0 + scale) + shift


def _in_proj_kernel(x_ref, mod_ref, nw_ref, wr_ref, wm_ref, pr_ref, pm_ref):
    h = _norm_mod(x_ref[...], nw_ref[...], mod_ref[0:1, :], mod_ref[1:2, :]).astype(BF16)
    pr_ref[...] = jnp.dot(h, wr_ref[...], preferred_element_type=F32)
    pm_ref[...] = jnp.dot(h, wm_ref[...], preferred_element_type=F32)


def _in_proj(x, mod, nw, w_r, w_m, tm):
    bsz, seq, d = x.shape
    nr, nm = w_r.shape[1], w_m.shape[1]
    return pl.pallas_call(
        _in_proj_kernel,
        grid=(bsz, seq // tm),
        in_specs=[pl.BlockSpec((None, tm, d), lambda b, i: (b, i, 0)),
                  pl.BlockSpec((None, 6, d), lambda b, i: (b, 0, 0)),
                  pl.BlockSpec((1, d), lambda b, i: (0, 0)),
                  pl.BlockSpec((d, nr), lambda b, i: (0, 0)),
                  pl.BlockSpec((d, nm), lambda b, i: (0, 0))],
        out_specs=[pl.BlockSpec((None, tm, nr), lambda b, i: (b, i, 0)),
                   pl.BlockSpec((None, tm, nm), lambda b, i: (b, i, 0))],
        out_shape=[jax.ShapeDtypeStruct((bsz, seq, nr), F32),
                   jax.ShapeDtypeStruct((bsz, seq, nm), F32)],
        compiler_params=_params("arbitrary", "arbitrary"),
        name="in_proj",
    )(x, mod, nw, w_r, w_m)


def _rwkv_kernel(pr_ref, mu_ref, w0_ref, wup_ref, a0_ref, aup_ref, gup_ref, kk_ref, ka_ref,
                 rk_ref, gnw_ref, gnb_ref, o_ref, s_scr, prev_scr):
    C = CHUNK
    N = RWKV_HEAD

    @pl.when(pl.program_id(1) == 0)
    def _():
        s_scr[...] = jnp.zeros_like(s_scr)
        prev_scr[...] = jnp.zeros_like(prev_scr)

    x = pr_ref[...]
    row1 = lax.broadcasted_iota(jnp.int32, x.shape, 0)
    prev = jnp.where(row1 == 0, prev_scr[...], pltpu.roll(x, 1, axis=0))
    prev_scr[...] = x[C - 1:C, :]
    u = x + (prev - x) * mu_ref[...]

    o1, o2, o3 = RWKV_WIDTH, 2 * RWKV_WIDTH, 3 * RWKV_WIDTH
    o4 = o3 + DECAY_LORA
    o5 = o4 + ICLR_LORA
    r, k, v = u[:, :o1], u[:, o1:o2], u[:, o2:o3]
    wd, ad, gd = u[:, o3:o4], u[:, o4:o5], u[:, o5:]
    w_log = -_softplus(-(w0_ref[...] + _mm(jnp.tanh(wd), wup_ref[...]))) - 0.5
    lw = -jnp.exp(w_log)
    a = _sigmoid(a0_ref[...] + _mm(ad, aup_ref[...]))
    g = _mm(_sigmoid(gd), gup_ref[...])

    row = lax.broadcasted_iota(jnp.int32, (C, C), 0)
    col = lax.broadcasted_iota(jnp.int32, (C, C), 1)
    strict = col < row
    incl = col <= row
    same16 = (row >> 4) == (col >> 4)
    same32 = (row >> 5) == (col >> 5)
    eye = (row == col).astype(F32)
    cum = _mm_f32(incl.astype(F32), lw)

    H = range(RWKV_HEADS)
    each = lambda f, *ls: [f(*t) for t in zip(*ls)]
    sls = [slice(h * N, (h + 1) * N) for h in H]
    r_h, k_h, v_h, a_h, lw_h, cum_h = ([t[:, sl] for sl in sls] for t in (r, k, v, a, lw, cum))
    kk = [k_h[h] * kk_ref[:, sls[h]] for h in H]
    kk = each(lambda t: t / jnp.maximum(jnp.sqrt(jnp.sum(t * t, axis=-1, keepdims=True)), 1e-12), kk)
    k2 = [k_h[h] * (1.0 + (a_h[h] - 1.0) * ka_ref[:, sls[h]]) for h in H]
    b_h = each(lambda t, a_: t * a_, kk, a_h)
    cl = each(lambda t: t[C - 1:C, :], cum_h)
    p_inv = each(lambda t: jnp.exp(-t), cum_h)
    p_rest = each(lambda c_, t: jnp.exp(c_ - t), cl, cum_h)
    ah = each(lambda kk_, t, w_: -kk_ * jnp.exp(t - w_), kk, cum_h, lw_h)
    rh = each(lambda r_, t: r_ * jnp.exp(t), r_h, cum_h)
    bt = each(lambda b_, p_: b_ * p_, b_h, p_inv)
    kt = each(lambda k_, p_: k_ * p_, k2, p_inv)
    l_ab = each(lambda x_, y_: jnp.where(strict, _nt(x_, y_), 0.0), ah, bt)
    l_ak = each(lambda x_, y_: jnp.where(strict, _nt(x_, y_), 0.0), ah, kt)
    m_rb = each(lambda x_, y_: jnp.where(incl, _nt(x_, y_), 0.0), rh, bt)
    m_rk = each(lambda x_, y_: jnp.where(incl, _nt(x_, y_), 0.0), rh, kt)
    n1 = each(lambda t: jnp.where(same16, t, 0.0), l_ab)
    xi = each(lambda t: eye + t, n1)
    nk = n1
    for _ in range(3):
        nk = each(lambda t: _mm(t, t), nk)
        xi = each(lambda x_, t: x_ + _mm(x_, t), xi, nk)
    e1 = each(lambda t: jnp.where(same32 & jnp.logical_not(same16), t, 0.0), l_ab)
    xe = each(_mm, xi, e1)
    xi = each(lambda x_, t: x_ + _mm(t, x_), xi, xe)
    e2 = each(lambda t: jnp.where(same32, 0.0, t), l_ab)
    xe = each(_mm, xi, e2)
    xi = each(lambda x_, t: x_ + _mm(t, x_), xi, xe)
    z = each(_mm, l_ak, v_h)
    a2 = each(_mm, xi, ah)
    u0 = each(_mm, xi, z)
    s0 = [s_scr[h] for h in H]
    r2 = each(lambda r_, m_, a_: r_ + _mm(m_, a_), rh, m_rb, a2)
    y0 = each(lambda mb, u_, mk, v_: _mm(mb, u_) + _mm(mk, v_), m_rb, u0, m_rk, v_h)
    y = each(lambda r_, s_, y_: _nt(r_, s_) + y_, r2, s0, y0)
    b2 = each(lambda b_, p_: b_ * p_, b_h, p_rest)
    gt = each(_tn, a2, b2)
    ht = each(lambda u_, b_, v_, k_, p_: _tn(u_, b_) + _tn(v_, k_ * p_), u0, b2, v_h, k2, p_rest)
    for h in H:
        s_scr[h] = s0[h] * jnp.exp(cl[h]) + _mm(s0[h], gt[h]) + ht[h]
    for h in H:
        sl = sls[h]
        mean = jnp.mean(y[h], axis=-1, keepdims=True)
        yc = y[h] - mean
        var = jnp.mean(yc * yc, axis=-1, keepdims=True)
        yn = yc * lax.rsqrt(var + GN_EPS) * gnw_ref[:, sl] + gnb_ref[:, sl]
        bonus = jnp.sum(r_h[h] * k2[h] * rk_ref[:, sl], axis=-1, keepdims=True) * v_h[h]
        o_ref[:, sl] = (yn + bonus) * g[:, sl]


def _rwkv(pr, mu, w0, w_up, a0, a_up, g_up, k_k, k_a, r_k, gn_w, gn_b):
    bsz, seq, cols = pr.shape
    row = lambda p: p.reshape(1, -1)
    vec = lambda n: pl.BlockSpec((1, n), lambda b, c: (0, 0))
    mat = lambda m: pl.BlockSpec(m.shape, lambda b, c: (0, 0))
    return pl.pallas_call(
        _rwkv_kernel,
        grid=(bsz, seq // CHUNK),
        in_specs=[pl.BlockSpec((None, CHUNK, cols), lambda b, c: (b, c, 0)),
                  vec(cols), vec(RWKV_WIDTH), mat(w_up), vec(RWKV_WIDTH), mat(a_up), mat(g_up),
                  vec(RWKV_WIDTH), vec(RWKV_WIDTH), vec(RWKV_WIDTH), vec(RWKV_WIDTH), vec(RWKV_WIDTH)],
        out_specs=pl.BlockSpec((None, CHUNK, RWKV_WIDTH), lambda b, c: (b, c, 0)),
        out_shape=jax.ShapeDtypeStruct((bsz, seq, RWKV_WIDTH), F32),
        scratch_shapes=[pltpu.VMEM((RWKV_HEADS, RWKV_HEAD, RWKV_HEAD), F32),
                        pltpu.VMEM((1, cols), F32)],
        compiler_params=_params("arbitrary", "arbitrary"),
        name="rwkv",
    )(pr, row(mu), row(w0), w_up, row(a0), a_up, g_up, row(k_k), row(k_a), row(r_k), row(gn_w), row(gn_b))


def _mlstm_kernel(pm_ref, cw_ref, cb_ref, gb_ref, hn_ref, o_ref, ext_scr, c_scr, n_scr, m_scr):
    C = CHUNK
    DK, DV = MLSTM_DK, MLSTM_DV
    QK2 = 2 * MLSTM_QK

    @pl.when(pl.program_id(1) == 0)
    def _():
        ext_scr[0:8, :] = jnp.zeros((8, QK2), F32)
        c_scr[...] = jnp.zeros_like(c_scr)
        n_scr[...] = jnp.zeros_like(n_scr)
        m_scr[...] = jnp.zeros_like(m_scr)

    ext_scr[8:8 + C, :] = pm_ref[:, 0:QK2]
    conv = cb_ref[...] + cw_ref[0:1, :] * ext_scr[5:5 + C, :]
    for i in range(1, CONV_WIDTH):
        conv = conv + cw_ref[i:i + 1, :] * ext_scr[5 + i:5 + i + C, :]
    ext_scr[0:8, :] = ext_scr[C:C + 8, :]
    qk = conv * _sigmoid(conv)
    q = qk[:, :MLSTM_QK] * (DK ** -0.5)
    k = qk[:, MLSTM_QK:]
    v = pm_ref[:, QK2:QK2 + MLSTM_WIDTH]
    og = pm_ref[:, QK2 + MLSTM_WIDTH:MLSTM_MAIN]

    z = GATE_SOFTCAP * jnp.tanh((pm_ref[:, MLSTM_MAIN:] + gb_ref[...]) / GATE_SOFTCAP)
    log_f = jnp.minimum(z, 0.0) - jnp.log1p(jnp.exp(-jnp.abs(z)))
    row = lax.broadcasted_iota(jnp.int32, (C, C), 0)
    col = lax.broadcasted_iota(jnp.int32, (C, C), 1)
    incl = col <= row
    lane = lax.broadcasted_iota(jnp.int32, (C, LANES), 1)
    gates = jnp.where(lane < MLSTM_HEADS, z, _mm_f32(incl.astype(F32), log_f))
    ri = lax.broadcasted_iota(jnp.int32, (LANES, LANES), 0)
    ci = lax.broadcasted_iota(jnp.int32, (LANES, LANES), 1)
    gates_t = lax.dot_general((ri == ci).astype(F32), gates, (((1,), (1,)), ((), ())),
                              preferred_element_type=F32, precision=HIGHEST)

    H = range(MLSTM_HEADS)
    each = lambda f, *ls: [f(*t) for t in zip(*ls)]
    li_col = [gates[:, h:h + 1] for h in H]
    b_col = [gates[:, MLSTM_HEADS + h:MLSTM_HEADS + h + 1] for h in H]
    li_row = [gates_t[h:h + 1, :] for h in H]
    b_row = [gates_t[MLSTM_HEADS + h:MLSTM_HEADS + h + 1, :] for h in H]
    m = [m_scr[h][0:1, 0:1] for h in H]
    q_h = [q[:, h * DK:(h + 1) * DK] for h in H]
    k_h = [k[:, h * DK:(h + 1) * DK] for h in H]
    v_h = [v[:, h * DV:(h + 1) * DV] for h in H]
    c_mat = [c_scr[h] for h in H]
    n_vec = [n_scr[h][0:1, 0:DK] for h in H]
    qk_h = each(_nt, q_h, k_h)
    qc_h = each(_mm, q_h, c_mat)
    d_mat = each(lambda bc, br, lr: jnp.where(incl, bc - br + lr, -jnp.inf), b_col, b_row, li_row)
    inter = each(lambda bc, m_: bc + m_, b_col, m)
    m_t = each(lambda i_, d_: jnp.maximum(i_, jnp.max(d_, axis=-1, keepdims=True)), inter, d_mat)
    scores = each(lambda s_, d_, mt: s_ * jnp.exp(d_ - mt), qk_h, d_mat, m_t)
    w_inter = each(lambda i_, mt: jnp.exp(i_ - mt), inter, m_t)
    sv_h = each(_mm, scores, v_h)
    b_last = each(lambda bc: bc[C - 1:C, :], b_col)
    end_log = each(lambda bl, bc, lc: bl - bc + lc, b_last, b_col, li_col)
    m_new = each(lambda bl, m_, el: jnp.maximum(bl + m_, jnp.max(el, axis=0, keepdims=True)), b_last, m, end_log)
    carry_w = each(lambda bl, m_, mn: jnp.exp(bl + m_ - mn), b_last, m, m_new)
    kw = each(lambda k_, el, mn: k_ * jnp.exp(el - mn), k_h, end_log, m_new)
    kv_h = each(_tn, kw, v_h)
    for h in H:
        c_scr[h] = carry_w[h] * c_mat[h] + kv_h[h]
        n_new = carry_w[h] * n_vec[h] + jnp.sum(kw[h], axis=0, keepdims=True)
        n_scr[h] = jnp.broadcast_to(jnp.concatenate([n_new, jnp.zeros((1, LANES - DK), F32)], axis=1), (8, LANES))
        m_scr[h] = jnp.broadcast_to(m_new[h], (8, LANES))
    for h in H:
        num = sv_h[h] + w_inter[h] * qc_h[h]
        den = (jnp.sum(scores[h], axis=-1, keepdims=True)
               + w_inter[h] * jnp.sum(q_h[h] * n_vec[h], axis=-1, keepdims=True))
        hh = num / jnp.maximum(jnp.abs(den), jnp.exp(-m_t[h]))
        hn = hh * lax.rsqrt(jnp.mean(hh * hh, axis=-1, keepdims=True) + EPS) * hn_ref[:, h * DV:(h + 1) * DV]
        o_ref[:, h * DV:(h + 1) * DV] = hn * _sigmoid(og[:, h * DV:(h + 1) * DV])


def _mlstm(pm, conv_w, conv_b, i_b, f_b, hn_w):
    bsz, seq, cols = pm.shape
    gate_b = jnp.zeros((1, LANES), F32).at[0, :MLSTM_HEADS].set(i_b).at[0, MLSTM_HEADS:2 * MLSTM_HEADS].set(f_b)
    full = lambda a: pl.BlockSpec(a.shape, lambda b, c: (0, 0))
    conv_b = conv_b.reshape(1, -1)
    hn_w = hn_w.reshape(1, -1)
    return pl.pallas_call(
        _mlstm_kernel,
        grid=(bsz, seq // CHUNK),
        in_specs=[pl.BlockSpec((None, CHUNK, cols), lambda b, c: (b, c, 0)),
                  full(conv_w), full(conv_b), full(gate_b), full(hn_w)],
        out_specs=pl.BlockSpec((None, CHUNK, MLSTM_WIDTH), lambda b, c: (b, c, 0)),
        out_shape=jax.ShapeDtypeStruct((bsz, seq, MLSTM_WIDTH), F32),
        scratch_shapes=[pltpu.VMEM((CHUNK + 8, 2 * MLSTM_QK), F32),
                        pltpu.VMEM((MLSTM_HEADS, MLSTM_DK, MLSTM_DV), F32),
                        pltpu.VMEM((MLSTM_HEADS, 8, LANES), F32),
                        pltpu.VMEM((MLSTM_HEADS, 8, LANES), F32)],
        compiler_params=_params("arbitrary", "arbitrary"),
        name="mlstm",
    )(pm, conv_w, conv_b, gate_b, hn_w)


def _mix_out_kernel(x_ref, yr_ref, ym_ref, mod_ref, nw_ref, wor_ref, wom_ref, wg_ref, bg_ref, we_ref, be_ref,
                    x1_ref, h_ref, cw_ref):
    y = (jnp.dot(yr_ref[...].astype(BF16), wor_ref[...], preferred_element_type=F32)
         + jnp.dot(ym_ref[...].astype(BF16), wom_ref[...], preferred_element_type=F32))
    x1 = x_ref[...] + mod_ref[2:3, :] * y
    x1_ref[...] = x1
    h = _norm_mod(x1, nw_ref[...], mod_ref[3:4, :], mod_ref[4:5, :])
    h_ref[...] = h.astype(BF16)

    gl = _mm_f32(h, wg_ref[...]) + bg_ref[...]
    el = _mm_f32(h, we_ref[...]) + be_ref[...]
    lane = lax.broadcasted_iota(jnp.int32, gl.shape, 1)
    neg = -jnp.inf
    gl = jnp.where(lane < N_GROUPS, gl, neg)
    g_max = jnp.max(gl, axis=-1, keepdims=True)
    g_idx = jnp.min(jnp.where(gl == g_max, lane, LANES), axis=-1, keepdims=True)
    g_w = 1.0 / jnp.sum(jnp.exp(gl - g_max), axis=-1, keepdims=True)
    in_group = ((lane >> 3) == g_idx) & (lane < N_EXPERTS)
    sel = jnp.where(in_group, el, neg)
    v1 = jnp.max(sel, axis=-1, keepdims=True)
    i1 = jnp.min(jnp.where(sel == v1, lane, LANES), axis=-1, keepdims=True)
    rest = jnp.where(lane == i1, neg, sel)
    v2 = jnp.max(rest, axis=-1, keepdims=True)
    i2 = jnp.min(jnp.where(rest == v2, lane, LANES), axis=-1, keepdims=True)
    e2 = jnp.exp(v2 - v1)
    w1 = 1.0 / (1.0 + e2)
    w2 = e2 / (1.0 + e2)
    cw_ref[...] = jnp.where(lane == i1, w1 * g_w, jnp.where(lane == i2, w2 * g_w, 0.0))


def _mix_out(x, y_r, y_m, mod, nw, wo_r, wo_m, wg, bg, we, be, tm):
    bsz, seq, d = x.shape
    tok = lambda n: pl.BlockSpec((None, tm, n), lambda b, i: (b, i, 0))
    full = lambda a: pl.BlockSpec(a.shape, lambda b, i: (0, 0))
    return pl.pallas_call(
        _mix_out_kernel,
        grid=(bsz, seq // tm),
        in_specs=[tok(d), tok(RWKV_WIDTH), tok(MLSTM_WIDTH),
                  pl.BlockSpec((None, 6, d), lambda b, i: (b, 0, 0)),
                  full(nw), full(wo_r), full(wo_m), full(wg), full(bg), full(we), full(be)],
        out_specs=[tok(d), tok(d), tok(LANES)],
        out_shape=[jax.ShapeDtypeStruct((bsz, seq, d), F32),
                   jax.ShapeDtypeStruct((bsz, seq, d), BF16),
                   jax.ShapeDtypeStruct((bsz, seq, LANES), F32)],
        compiler_params=_params("arbitrary", "arbitrary"),
        name="mix_out",
    )(x, y_r, y_m, mod, nw, wo_r, wo_m, wg, bg, we, be)


def _moe_kernel(h_ref, cw_ref, x1_ref, mod_ref, wg_ref, wu_ref, wd_ref, fw_ref, o_ref, acc_scr):
    e = pl.program_id(2)

    @pl.when(e == 0)
    def _():
        acc_scr[...] = jnp.zeros_like(acc_scr)

    h = h_ref[...]
    hg = jnp.dot(h, wg_ref[...], preferred_element_type=F32)
    hu = jnp.dot(h, wu_ref[...], preferred_element_type=F32)
    lane = lax.broadcasted_iota(jnp.int32, cw_ref.shape, 1)
    cw = jnp.sum(jnp.where(lane == e, cw_ref[...], 0.0), axis=-1, keepdims=True)
    act = hg * _sigmoid(hg) * hu * cw
    acc_scr[...] += jnp.dot(act.astype(BF16), wd_ref[...], preferred_element_type=F32)

    @pl.when(e == pl.num_programs(2) - 1)
    def _():
        x2 = x1_ref[...] + mod_ref[5:6, :] * acc_scr[...]
        o_ref[...] = x2 * lax.rsqrt(jnp.mean(x2 * x2, axis=-1, keepdims=True) + EPS) * fw_ref[...]


def _moe(h, cw, x1, mod, w_gate, w_up, w_down, fw, tm):
    bsz, seq, d = x1.shape
    ne, _, f = w_gate.shape
    tok = lambda n: pl.BlockSpec((None, tm, n), lambda b, i, e: (b, i, 0))
    return pl.pallas_call(
        _moe_kernel,
        grid=(bsz, seq // tm, ne),
        in_specs=[tok(d), tok(LANES), tok(d),
                  pl.BlockSpec((None, 6, d), lambda b, i, e: (b, 0, 0)),
                  pl.BlockSpec((None, d, f), lambda b, i, e: (e, 0, 0)),
                  pl.BlockSpec((None, d, f), lambda b, i, e: (e, 0, 0)),
                  pl.BlockSpec((None, f, d), lambda b, i, e: (e, 0, 0)),
                  pl.BlockSpec((1, d), lambda b, i, e: (0, 0))],
        out_specs=tok(d),
        out_shape=jax.ShapeDtypeStruct((bsz, seq, d), F32),
        scratch_shapes=[pltpu.VMEM((tm, d), F32)],
        compiler_params=_params("arbitrary", "arbitrary", "arbitrary"),
        name="moe",
    )(h, cw, x1, mod, w_gate, w_up, w_down, fw)


def _pad_cols(w, n):
    return jnp.pad(w, ((0, 0), (0, n - w.shape[1])))


def kernel(x, c, ada_w, ada_b, mix_norm_w, w_in, rwkv_mu, rwkv_w0, rwkv_w_up, rwkv_a0, rwkv_a_up, rwkv_g_up, rwkv_k_k, rwkv_k_a, rwkv_r_k, rwkv_gn_w, rwkv_gn_b, mlstm_conv_w, mlstm_conv_b, mlstm_i_b, mlstm_f_b, mlstm_hn_w, w_out, ffn_norm_w, moe_w_group, moe_b_group, moe_w_router, moe_b_router, moe_w_gate, moe_w_up, moe_w_down, final_norm_w):
    bsz, seq, d = x.shape
    depth = ada_w.shape[0]
    assert seq % CHUNK == 0
    tm = min(512, seq)
    tm_moe = min(1024, seq)
    for l in range(depth):
        mod = _ada(c, ada_w[l], ada_b[l]).reshape(bsz, 6, d)
        w_r = w_in[l][:, :RWKV_COLS].astype(BF16)
        w_m = _pad_cols(w_in[l][:, RWKV_COLS:], MLSTM_PAD_COLS).astype(BF16)
        pr, pm = _in_proj(x, mod, mix_norm_w[l].reshape(1, d), w_r, w_m, tm)
        y_r = _rwkv(pr, rwkv_mu[l], rwkv_w0[l], rwkv_w_up[l], rwkv_a0[l], rwkv_a_up[l], rwkv_g_up[l],
                    rwkv_k_k[l], rwkv_k_a[l], rwkv_r_k[l], rwkv_gn_w[l], rwkv_gn_b[l])
        y_m = _mlstm(pm, mlstm_conv_w[l], mlstm_conv_b[l], mlstm_i_b[l], mlstm_f_b[l], mlstm_hn_w[l])
        wo = w_out[l].astype(BF16)
        x1, h, cw = _mix_out(
            x, y_r, y_m, mod, ffn_norm_w[l].reshape(1, d), wo[:RWKV_WIDTH], wo[RWKV_WIDTH:],
            _pad_cols(moe_w_group[l], LANES), _pad_cols(moe_b_group[l].reshape(1, -1), LANES),
            _pad_cols(moe_w_router[l], LANES), _pad_cols(moe_b_router[l].reshape(1, -1), LANES), tm)
        last = l == depth - 1
        fw = final_norm_w.reshape(1, d) if last else None
        assert last, "the fused final norm assumes a single layer"
        x = _moe(h, cw, x1, mod, moe_w_gate[l].astype(BF16), moe_w_up[l].astype(BF16),
                 moe_w_down[l].astype(BF16), fw, tm_moe)
    return x
```

```python
import jax
import jax.numpy as jnp
from jax import lax
from jax.experimental import pallas as pl
from jax.experimental.pallas import tpu as pltpu

F32 = jnp.float32
BF16 = jnp.bfloat16
HIGHEST = lax.Precision.HIGHEST

CHUNK = 64
EPS = 1e-6
RWKV_HEAD = 64
RWKV_HEADS = 8
RWKV_WIDTH = RWKV_HEADS * RWKV_HEAD
DECAY_LORA = 64
ICLR_LORA = 64
GATE_LORA = 128
GN_EPS = 64e-5
MLSTM_HEADS = 4
MLSTM_DK = 64
MLSTM_DV = 128
MLSTM_QK = MLSTM_HEADS * MLSTM_DK
MLSTM_WIDTH = MLSTM_HEADS * MLSTM_DV
CONV_WIDTH = 4
GATE_SOFTCAP = 15.0
RWKV_COLS = 3 * RWKV_WIDTH + DECAY_LORA + ICLR_LORA + GATE_LORA
N_GROUPS = 4
EXPERTS_PER_GROUP = 8
N_EXPERTS = N_GROUPS * EXPERTS_PER_GROUP
LANES = 128
MLSTM_MAIN = 2 * MLSTM_QK + 2 * MLSTM_WIDTH
MLSTM_PAD_COLS = MLSTM_MAIN + LANES
VMEM_LIMIT = 56 * 1024 * 1024


def _mm(a, b):
    return jnp.dot(a.astype(BF16), b.astype(BF16), preferred_element_type=F32)


def _nt(a, b):
    return lax.dot_general(a.astype(BF16), b.astype(BF16), (((1,), (1,)), ((), ())),
                           preferred_element_type=F32)


def _tn(a, b):
    return lax.dot_general(a.astype(BF16), b.astype(BF16), (((0,), (0,)), ((), ())),
                           preferred_element_type=F32)


def _mm_f32(a, b):
    return jnp.dot(a, b, preferred_element_type=F32, precision=HIGHEST)


def _sigmoid(z):
    return 1.0 / (1.0 + jnp.exp(-z))


def _softplus(z):
    return jnp.maximum(z, 0.0) + jnp.log1p(jnp.exp(-jnp.abs(z)))


def _params(*sem):
    return pltpu.CompilerParams(dimension_semantics=sem, vmem_limit_bytes=VMEM_LIMIT)


def _ada_kernel(c_ref, w_ref, b_ref, o_ref):
    c = c_ref[...]
    o_ref[...] = _mm_f32(c * _sigmoid(c), w_ref[...]) + b_ref[...]


def _ada(c, w, b):
    bsz, d = c.shape
    n = w.shape[1]
    return pl.pallas_call(
        _ada_kernel,
        grid=(n // d,),
        in_specs=[pl.BlockSpec((bsz, d), lambda j: (0, 0)),
                  pl.BlockSpec((d, d), lambda j: (0, j)),
                  pl.BlockSpec((1, d), lambda j: (0, j))],
        out_specs=pl.BlockSpec((bsz, d), lambda j: (0, j)),
        out_shape=jax.ShapeDtypeStruct((bsz, n), F32),
        compiler_params=_params("arbitrary"),
        name="ada",
    )(c, w, b.reshape(1, n))


def _norm_mod(x, nw, shift, scale):
    y = x * lax.rsqrt(jnp.mean(x * x, axis=-1, keepdims=True) + EPS) * nw
    return y * (1.0 + scale) + shift


def _in_proj_kernel(x_ref, mod_ref, nw_ref, wr_ref, wm_ref, pr_ref, pm_ref):
    h = _norm_mod(x_ref[...], nw_ref[...], mod_ref[0:1, :], mod_ref[1:2, :]).astype(BF16)
    pr_ref[...] = jnp.dot(h, wr_ref[...], preferred_element_type=F32)
    pm_ref[...] = jnp.dot(h, wm_ref[...], preferred_element_type=F32)


def _in_proj(x, mod, nw, w_r, w_m, tm):
    bsz, seq, d = x.shape
    nr, nm = w_r.shape[1], w_m.shape[1]
    return pl.pallas_call(
        _in_proj_kernel,
        grid=(bsz, seq // tm),
        in_specs=[pl.BlockSpec((None, tm, d), lambda b, i: (b, i, 0)),
                  pl.BlockSpec((None, 6, d), lambda b, i: (b, 0, 0)),
                  pl.BlockSpec((1, d), lambda b, i: (0, 0)),
                  pl.BlockSpec((d, nr), lambda b, i: (0, 0)),
                  pl.BlockSpec((d, nm), lambda b, i: (0, 0))],
        out_specs=[pl.BlockSpec((None, tm, nr), lambda b, i: (b, i, 0)),
                   pl.BlockSpec((None, tm, nm), lambda b, i: (b, i, 0))],
        out_shape=[jax.ShapeDtypeStruct((bsz, seq, nr), F32),
                   jax.ShapeDtypeStruct((bsz, seq, nm), F32)],
        compiler_params=_params("arbitrary", "arbitrary"),
        name="in_proj",
    )(x, mod, nw, w_r, w_m)


def _rwkv_kernel(pr_ref, mu_ref, w0_ref, wup_ref, a0_ref, aup_ref, gup_ref, kk_ref, ka_ref,
                 rk_ref, gnw_ref, gnb_ref, o_ref, s_scr, prev_scr):
    C = CHUNK
    N = RWKV_HEAD

    @pl.when(pl.program_id(1) == 0)
    def _():
        s_scr[...] = jnp.zeros_like(s_scr)
        prev_scr[...] = jnp.zeros_like(prev_scr)

    x = pr_ref[...]
    row1 = lax.broadcasted_iota(jnp.int32, x.shape, 0)
    prev = jnp.where(row1 == 0, prev_scr[...], pltpu.roll(x, 1, axis=0))
    prev_scr[...] = x[C - 1:C, :]
    u = x + (prev - x) * mu_ref[...]

    o1, o2, o3 = RWKV_WIDTH, 2 * RWKV_WIDTH, 3 * RWKV_WIDTH
    o4 = o3 + DECAY_LORA
    o5 = o4 + ICLR_LORA
    r, k, v = u[:, :o1], u[:, o1:o2], u[:, o2:o3]
    wd, ad, gd = u[:, o3:o4], u[:, o4:o5], u[:, o5:]
    w_log = -_softplus(-(w0_ref[...] + _mm(jnp.tanh(wd), wup_ref[...]))) - 0.5
    lw = -jnp.exp(w_log)
    a = _sigmoid(a0_ref[...] + _mm(ad, aup_ref[...]))
    g = _mm(_sigmoid(gd), gup_ref[...])

    row = lax.broadcasted_iota(jnp.int32, (C, C), 0)
    col = lax.broadcasted_iota(jnp.int32, (C, C), 1)
    strict = col < row
    incl = col <= row
    same16 = (row >> 4) == (col >> 4)
    same32 = (row >> 5) == (col >> 5)
    eye = (row == col).astype(F32)
    cum = _mm_f32(incl.astype(F32), lw)

    H = range(RWKV_HEADS)
    each = lambda f, *ls: [f(*t) for t in zip(*ls)]
    sls = [slice(h * N, (h + 1) * N) for h in H]
    r_h, k_h, v_h, a_h, lw_h, cum_h = ([t[:, sl] for sl in sls] for t in (r, k, v, a, lw, cum))
    kk = [k_h[h] * kk_ref[:, sls[h]] for h in H]
    kk = each(lambda t: t / jnp.maximum(jnp.sqrt(jnp.sum(t * t, axis=-1, keepdims=True)), 1e-12), kk)
    k2 = [k_h[h] * (1.0 + (a_h[h] - 1.0) * ka_ref[:, sls[h]]) for h in H]
    b_h = each(lambda t, a_: t * a_, kk, a_h)
    cl = each(lambda t: t[C - 1:C, :], cum_h)
    p_inv = each(lambda t: jnp.exp(-t), cum_h)
    p_rest = each(lambda c_, t: jnp.exp(c_ - t), cl, cum_h)
    ah = each(lambda kk_, t, w_: -kk_ * jnp.exp(t - w_), kk, cum_h, lw_h)
    rh = each(lambda r_, t: r_ * jnp.exp(t), r_h, cum_h)
    bt = each(lambda b_, p_: b_ * p_, b_h, p_inv)
    kt = each(lambda k_, p_: k_ * p_, k2, p_inv)
    l_ab = each(lambda x_, y_: jnp.where(strict, _nt(x_, y_), 0.0), ah, bt)
    l_ak = each(lambda x_, y_: jnp.where(strict, _nt(x_, y_), 0.0), ah, kt)
    m_rb = each(lambda x_, y_: jnp.where(incl, _nt(x_, y_), 0.0), rh, bt)
    m_rk = each(lambda x_, y_: jnp.where(incl, _nt(x_, y_), 0.0), rh, kt)
    n1 = each(lambda t: jnp.where(same16, t, 0.0), l_ab)
    xi = each(lambda t: eye + t, n1)
    nk = n1
    for _ in range(3):
        nk = each(lambda t: _mm(t, t), nk)
        xi = each(lambda x_, t: x_ + _mm(x_, t), xi, nk)
    e1 = each(lambda t: jnp.where(same32 & jnp.logical_not(same16), t, 0.0), l_ab)
    xe = each(_mm, xi, e1)
    xi = each(lambda x_, t: x_ + _mm(t, x_), xi, xe)
    e2 = each(lambda t: jnp.where(same32, 0.0, t), l_ab)
    xe = each(_mm, xi, e2)
    xi = each(lambda x_, t: x_ + _mm(t, x_), xi, xe)
    z = each(_mm, l_ak, v_h)
    a2 = each(_mm, xi, ah)
    u0 = each(_mm, xi, z)
    s0 = [s_scr[h] for h in H]
    r2 = each(lambda r_, m_, a_: r_ + _mm(m_, a_), rh, m_rb, a2)
    y0 = each(lambda mb, u_, mk, v_: _mm(mb, u_) + _mm(mk, v_), m_rb, u0, m_rk, v_h)
    y = each(lambda r_, s_, y_: _nt(r_, s_) + y_, r2, s0, y0)
    b2 = each(lambda b_, p_: b_ * p_, b_h, p_rest)
    gt = each(_tn, a2, b2)
    ht = each(lambda u_, b_, v_, k_, p_: _tn(u_, b_) + _tn(v_, k_ * p_), u0, b2, v_h, k2, p_rest)
    for h in H:
        s_scr[h] = s0[h] * jnp.exp(cl[h]) + _mm(s0[h], gt[h]) + ht[h]
    for h in H:
        sl = sls[h]
        mean = jnp.mean(y[h], axis=-1, keepdims=True)
        yc = y[h] - mean
        var = jnp.mean(yc * yc, axis=-1, keepdims=True)
        yn = yc * lax.rsqrt(var + GN_EPS) * gnw_ref[:, sl] + gnb_ref[:, sl]
        bonus = jnp.sum(r_h[h] * k2[h] * rk_ref[:, sl], axis=-1, keepdims=True) * v_h[h]
        o_ref[:, sl] = (yn + bonus) * g[:, sl]


def _rwkv(pr, mu, w0, w_up, a0, a_up, g_up, k_k, k_a, r_k, gn_w, gn_b):
    bsz, seq, cols = pr.shape
    row = lambda p: p.reshape(1, -1)
    vec = lambda n: pl.BlockSpec((1, n), lambda b, c: (0, 0))
    mat = lambda m: pl.BlockSpec(m.shape, lambda b, c: (0, 0))
    return pl.pallas_call(
        _rwkv_kernel,
        grid=(bsz, seq // CHUNK),
        in_specs=[pl.BlockSpec((None, CHUNK, cols), lambda b, c: (b, c, 0)),
                  vec(cols), vec(RWKV_WIDTH), mat(w_up), vec(RWKV_WIDTH), mat(a_up), mat(g_up),
                  vec(RWKV_WIDTH), vec(RWKV_WIDTH), vec(RWKV_WIDTH), vec(RWKV_WIDTH), vec(RWKV_WIDTH)],
        out_specs=pl.BlockSpec((None, CHUNK, RWKV_WIDTH), lambda b, c: (b, c, 0)),
        out_shape=jax.ShapeDtypeStruct((bsz, seq, RWKV_WIDTH), F32),
        scratch_shapes=[pltpu.VMEM((RWKV_HEADS, RWKV_HEAD, RWKV_HEAD), F32),
                        pltpu.VMEM((1, cols), F32)],
        compiler_params=_params("arbitrary", "arbitrary"),
        name="rwkv",
    )(pr, row(mu), row(w0), w_up, row(a0), a_up, g_up, row(k_k), row(k_a), row(r_k), row(gn_w), row(gn_b))


def _mlstm_kernel(pm_ref, cw_ref, cb_ref, gb_ref, hn_ref, o_ref, ext_scr, c_scr, n_scr, m_scr):
    C = CHUNK
    DK, DV = MLSTM_DK, MLSTM_DV
    QK2 = 2 * MLSTM_QK

    @pl.when(pl.program_id(1) == 0)
    def _():
        ext_scr[0:8, :] = jnp.zeros((8, QK2), F32)
        c_scr[...] = jnp.zeros_like(c_scr)
        n_scr[...] = jnp.zeros_like(n_scr)
        m_scr[...] = jnp.zeros_like(m_scr)

    ext_scr[8:8 + C, :] = pm_ref[:, 0:QK2]
    conv = cb_ref[...] + cw_ref[0:1, :] * ext_scr[5:5 + C, :]
    for i in range(1, CONV_WIDTH):
        conv = conv + cw_ref[i:i + 1, :] * ext_scr[5 + i:5 + i + C, :]
    ext_scr[0:8, :] = ext_scr[C:C + 8, :]
    qk = conv * _sigmoid(conv)
    q = qk[:, :MLSTM_QK] * (DK ** -0.5)
    k = qk[:, MLSTM_QK:]
    v = pm_ref[:, QK2:QK2 + MLSTM_WIDTH]
    og = pm_ref[:, QK2 + MLSTM_WIDTH:MLSTM_MAIN]

    z = GATE_SOFTCAP * jnp.tanh((pm_ref[:, MLSTM_MAIN:] + gb_ref[...]) / GATE_SOFTCAP)
    log_f = jnp.minimum(z, 0.0) - jnp.log1p(jnp.exp(-jnp.abs(z)))
    row = lax.broadcasted_iota(jnp.int32, (C, C), 0)
    col = lax.broadcasted_iota(jnp.int32, (C, C), 1)
    incl = col <= row
    lane = lax.broadcasted_iota(jnp.int32, (C, LANES), 1)
    gates = jnp.where(lane < MLSTM_HEADS, z, _mm_f32(incl.astype(F32), log_f))
    ri = lax.broadcasted_iota(jnp.int32, (LANES, LANES), 0)
    ci = lax.broadcasted_iota(jnp.int32, (LANES, LANES), 1)
    gates_t = lax.dot_general((ri == ci).astype(F32), gates, (((1,), (1,)), ((), ())),
                              preferred_element_type=F32, precision=HIGHEST)

    H = range(MLSTM_HEADS)
    each = lambda f, *ls: [f(*t) for t in zip(*ls)]
    li_col = [gates[:, h:h + 1] for h in H]
    b_col = [gates[:, MLSTM_HEADS + h:MLSTM_HEADS + h + 1] for h in H]
    li_row = [gates_t[h:h + 1, :] for h in H]
    b_row = [gates_t[MLSTM_HEADS + h:MLSTM_HEADS + h + 1, :] for h in H]
    m = [m_scr[h][0:1, 0:1] for h in H]
    q_h = [q[:, h * DK:(h + 1) * DK] for h in H]
    k_h = [k[:, h * DK:(h + 1) * DK] for h in H]
    v_h = [v[:, h * DV:(h + 1) * DV] for h in H]
    c_mat = [c_scr[h] for h in H]
    n_vec = [n_scr[h][0:1, 0:DK] for h in H]
    qk_h = each(_nt, q_h, k_h)
    qc_h = each(_mm, q_h, c_mat)
    d_mat = each(lambda bc, br, lr: jnp.where(incl, bc - br + lr, -jnp.inf), b_col, b_row, li_row)
    inter = each(lambda bc, m_: bc + m_, b_col, m)
    m_t = each(lambda i_, d_: jnp.maximum(i_, jnp.max(d_, axis=-1, keepdims=True)), inter, d_mat)
    scores = each(lambda s_, d_, mt: s_ * jnp.exp(d_ - mt), qk_h, d_mat, m_t)
    w_inter = each(lambda i_, mt: jnp.exp(i_ - mt), inter, m_t)
    sv_h = each(_mm, scores, v_h)
    b_last = each(lambda bc: bc[C - 1:C, :], b_col)
    end_log = each(lambda bl, bc, lc: bl - bc + lc, b_last, b_col, li_col)
    m_new = each(lambda bl, m_, el: jnp.maximum(bl + m_, jnp.max(el, axis=0, keepdims=True)), b_last, m, end_log)
    carry_w = each(lambda bl, m_, mn: jnp.exp(bl + m_ - mn), b_last, m, m_new)
    kw = each(lambda k_, el, mn: k_ * jnp.exp(el - mn), k_h, end_log, m_new)
    kv_h = each(_tn, kw, v_h)
    for h in H:
        c_scr[h] = carry_w[h] * c_mat[h] + kv_h[h]
        n_new = carry_w[h] * n_vec[h] + jnp.sum(kw[h], axis=0, keepdims=True)
        n_scr[h] = jnp.broadcast_to(jnp.concatenate([n_new, jnp.zeros((1, LANES - DK), F32)], axis=1), (8, LANES))
        m_scr[h] = jnp.broadcast_to(m_new[h], (8, LANES))
    for h in H:
        num = sv_h[h] + w_inter[h] * qc_h[h]
        den = (jnp.sum(scores[h], axis=-1, keepdims=True)
               + w_inter[h] * jnp.sum(q_h[h] * n_vec[h], axis=-1, keepdims=True))
        hh = num / jnp.maximum(jnp.abs(den), jnp.exp(-m_t[h]))
        hn = hh * lax.rsqrt(jnp.mean(hh * hh, axis=-1, keepdims=True) + EPS) * hn_ref[:, h * DV:(h + 1) * DV]
        o_ref[:, h * DV:(h + 1) * DV] = hn * _sigmoid(og[:, h * DV:(h + 1) * DV])


def _mlstm(pm, conv_w, conv_b, i_b, f_b, hn_w):
    bsz, seq, cols = pm.shape
    gate_b = jnp.zeros((1, LANES), F32).at[0, :MLSTM_HEADS].set(i_b).at[0, MLSTM_HEADS:2 * MLSTM_HEADS].set(f_b)
    full = lambda a: pl.BlockSpec(a.shape, lambda b, c: (0, 0))
    conv_b = conv_b.reshape(1, -1)
    hn_w = hn_w.reshape(1, -1)
    return pl.pallas_call(
        _mlstm_kernel,
        grid=(bsz, seq // CHUNK),
        in_specs=[pl.BlockSpec((None, CHUNK, cols), lambda b, c: (b, c, 0)),
                  full(conv_w), full(conv_b), full(gate_b), full(hn_w)],
        out_specs=pl.BlockSpec((None, CHUNK, MLSTM_WIDTH), lambda b, c: (b, c, 0)),
        out_shape=jax.ShapeDtypeStruct((bsz, seq, MLSTM_WIDTH), F32),
        scratch_shapes=[pltpu.VMEM((CHUNK + 8, 2 * MLSTM_QK), F32),
                        pltpu.VMEM((MLSTM_HEADS, MLSTM_DK, MLSTM_DV), F32),
                        pltpu.VMEM((MLSTM_HEADS, 8, LANES), F32),
                        pltpu.VMEM((MLSTM_HEADS, 8, LANES), F32)],
        compiler_params=_params("arbitrary", "arbitrary"),
        name="mlstm",
    )(pm, conv_w, conv_b, gate_b, hn_w)


def _mix_out_kernel(x_ref, yr_ref, ym_ref, mod_ref, nw_ref, wor_ref, wom_ref, wg_ref, bg_ref, we_ref, be_ref,
                    x1_ref, hx_ref):
    d = x_ref.shape[-1]
    y = (jnp.dot(yr_ref[...].astype(BF16), wor_ref[...], preferred_element_type=F32)
         + jnp.dot(ym_ref[...].astype(BF16), wom_ref[...], preferred_element_type=F32))
    x1 = x_ref[...] + mod_ref[2:3, :] * y
    x1_ref[...] = x1
    h = _norm_mod(x1, nw_ref[...], mod_ref[3:4, :], mod_ref[4:5, :])
    hx_ref[:, :d] = h.astype(BF16)

    gl = _mm_f32(h, wg_ref[...]) + bg_ref[...]
    el = _mm_f32(h, we_ref[...]) + be_ref[...]
    lane = lax.broadcasted_iota(jnp.int32, gl.shape, 1)
    neg = -jnp.inf
    gl = jnp.where(lane < N_GROUPS, gl, neg)
    g_max = jnp.max(gl, axis=-1, keepdims=True)
    g_idx = jnp.min(jnp.where(gl == g_max, lane, LANES), axis=-1, keepdims=True)
    g_w = 1.0 / jnp.sum(jnp.exp(gl - g_max), axis=-1, keepdims=True)
    in_group = ((lane >> 3) == g_idx) & (lane < N_EXPERTS)
    sel = jnp.where(in_group, el, neg)
    v1 = jnp.max(sel, axis=-1, keepdims=True)
    i1 = jnp.min(jnp.where(sel == v1, lane, LANES), axis=-1, keepdims=True)
    rest = jnp.where(lane == i1, neg, sel)
    v2 = jnp.max(rest, axis=-1, keepdims=True)
    i2 = jnp.min(jnp.where(rest == v2, lane, LANES), axis=-1, keepdims=True)
    e2 = jnp.exp(v2 - v1)
    w1 = 1.0 / (1.0 + e2)
    w2 = e2 / (1.0 + e2)
    cw = jnp.where(lane == i1, w1 * g_w, jnp.where(lane == i2, w2 * g_w, 0.0))
    hi = cw.astype(BF16).astype(F32)
    mid = (cw - hi).astype(BF16).astype(F32)
    lo = (cw - hi - mid).astype(BF16).astype(F32)
    flag = jnp.where(hi != 0.0, 1.0, 0.0)
    pack = (hi + pltpu.roll(mid, N_EXPERTS, axis=1) + pltpu.roll(lo, 2 * N_EXPERTS, axis=1)
            + pltpu.roll(flag, 3 * N_EXPERTS, axis=1))
    hx_ref[:, d:] = pack.astype(BF16)


def _mix_out(x, y_r, y_m, mod, nw, wo_r, wo_m, wg, bg, we, be, tm):
    bsz, seq, d = x.shape
    tok = lambda n: pl.BlockSpec((None, tm, n), lambda b, i: (b, i, 0))
    full = lambda a: pl.BlockSpec(a.shape, lambda b, i: (0, 0))
    return pl.pallas_call(
        _mix_out_kernel,
        grid=(bsz, seq // tm),
        in_specs=[tok(d), tok(RWKV_WIDTH), tok(MLSTM_WIDTH),
                  pl.BlockSpec((None, 6, d), lambda b, i: (b, 0, 0)),
                  full(nw), full(wo_r), full(wo_m), full(wg), full(bg), full(we), full(be)],
        out_specs=[tok(d), tok(d + LANES)],
        out_shape=[jax.ShapeDtypeStruct((bsz, seq, d), F32),
                   jax.ShapeDtypeStruct((bsz, seq, d + LANES), BF16)],
        compiler_params=_params("arbitrary", "arbitrary"),
        name="mix_out",
    )(x, y_r, y_m, mod, nw, wo_r, wo_m, wg, bg, we, be)


ROW_CHUNK = 16
SORT_TILE = 512
EXPERT_TILE = 512
NO_SLOT = 1e9


def _moe_plan(flags, tms, ts):
    t, ne = flags.shape
    nt = t // tms
    i32 = jnp.int32
    cnt = flags.reshape(nt, tms, ne).astype(i32).sum(axis=1)
    pc = (cnt + ROW_CHUNK - 1) // ROW_CHUNK * ROW_CHUNK
    loc_start = jnp.cumsum(pc, axis=1) - pc
    len_e = pc.sum(axis=0)
    seg_e = (len_e + ts - 1) // ts * ts
    e_end = jnp.cumsum(seg_e)
    e_start = e_end - seg_e
    dst_start = e_start[None, :] + jnp.cumsum(pc, axis=0) - pc
    n_tiles_max = _moe_max_rows(t, ne, tms, ts) // ts
    tile_expert = jnp.minimum(
        jnp.searchsorted(e_end, jnp.arange(n_tiles_max, dtype=i32) * ts, side="right"), ne - 1).astype(i32)
    return dict(
        loc_start=loc_start.reshape(-1).astype(i32), n_chunks=(pc // ROW_CHUNK).reshape(-1).astype(i32),
        dst_start=dst_start.reshape(-1).astype(i32), tail_start=(e_start + len_e).astype(i32),
        tail_chunks=((seg_e - len_e) // ROW_CHUNK).astype(i32),
        n_used=(e_end[-1:] // ts).astype(i32), tile_expert=tile_expert,
        loc_start_vec=jnp.pad(loc_start.astype(F32), ((0, 0), (0, LANES - ne))).reshape(nt, 1, LANES))


def _moe_max_rows(t, ne, tms, ts):
    rows = 2 * t + (t // tms) * ne * (ROW_CHUNK - 1) + ne * (ts - 1)
    return (rows + ts - 1) // ts * ts


def _local_rows(tms):
    return 2 * tms + N_EXPERTS * ROW_CHUNK


def _tile_slots(blk, loc_start_vec):
    tms = blk.shape[0]
    lane = lax.broadcasted_iota(jnp.int32, blk.shape, 1)
    m = jnp.where(lane < N_EXPERTS, pltpu.roll(blk, N_EXPERTS, axis=1), 0.0)
    row = lax.broadcasted_iota(jnp.int32, (tms, tms), 0)
    col = lax.broadcasted_iota(jnp.int32, (tms, tms), 1)
    before = jnp.dot((col < row).astype(BF16), m.astype(BF16), preferred_element_type=F32)
    slot = loc_start_vec + before
    slot_a = jnp.min(jnp.where(m > 0.0, slot, NO_SLOT), axis=-1, keepdims=True)
    slot_b = jnp.max(jnp.where(m > 0.0, slot, -NO_SLOT), axis=-1, keepdims=True)
    return slot_a, slot_b


def _moe_sort_kernel(ls_ref, nch_ref, ds_ref, tls_ref, tlc_ref, nu_ref, hx_ref, lsv_ref, xo_ref, xs_scr, z_scr,
                     sem, zsem):
    i = pl.program_id(0)
    tms, width = hx_ref.shape
    loc = xs_scr.shape[0]
    d = width - LANES
    slot_a, slot_b = _tile_slots(hx_ref[:, d:].astype(F32), lsv_ref[...])
    lane = lax.broadcasted_iota(jnp.int32, (tms, LANES), 1)
    packed = jnp.where(lane == 0, slot_a, jnp.where(lane == 1, slot_b, 0.0))
    ri = lax.broadcasted_iota(jnp.int32, (LANES, LANES), 0)
    ci = lax.broadcasted_iota(jnp.int32, (LANES, LANES), 1)
    slots_t = lax.dot_general((ri == ci).astype(F32), packed, (((1,), (1,)), ((), ())),
                              preferred_element_type=F32, precision=HIGHEST)
    srow = lax.broadcasted_iota(jnp.int32, (loc, tms), 0).astype(F32)
    perm = jnp.where((srow == slots_t[0:1, :]) | (srow == slots_t[1:2, :]), 1.0, 0.0).astype(BF16)
    xs_scr[...] = jnp.dot(perm, hx_ref[...], preferred_element_type=F32).astype(BF16)

    def copy(src, src_row, dst_row):
        return pltpu.make_async_copy(src.at[pl.ds(pl.multiple_of(src_row, ROW_CHUNK), ROW_CHUNK)],
                                     xo_ref.at[pl.ds(pl.multiple_of(dst_row, ROW_CHUNK), ROW_CHUNK)], sem)

    n_started = jnp.int32(0)
    for e in range(N_EXPERTS):
        n = nch_ref[i * N_EXPERTS + e]
        l0 = ls_ref[i * N_EXPERTS + e]
        d0 = ds_ref[i * N_EXPERTS + e]

        def run_body(j, carry, l0=l0, d0=d0):
            copy(xs_scr, l0 + j * ROW_CHUNK, d0 + j * ROW_CHUNK).start()
            return carry

        lax.fori_loop(0, n, run_body, 0)
        n_started = n_started + n

    z_scr[...] = jnp.zeros_like(z_scr)
    ts = z_scr.shape[0]
    n_tail = jnp.int32(0)
    for e in range(N_EXPERTS):
        n = jnp.where(i == 0, tlc_ref[e], 0)
        t0 = tls_ref[e]

        def tail_body(j, carry, t0=t0):
            copy(z_scr, 0, t0 + j * ROW_CHUNK).start()
            return carry

        lax.fori_loop(0, n, tail_body, 0)
        n_tail = n_tail + n

    def tile_copy(tile):
        return pltpu.make_async_copy(z_scr, xo_ref.at[pl.ds(pl.multiple_of(tile * ts, ts), ts)], zsem)

    n_spare = jnp.where(i == 0, xo_ref.shape[0] // ts - nu_ref[0], 0)

    def spare_body(j, carry):
        tile_copy(nu_ref[0] + j).start()
        return carry

    lax.fori_loop(0, n_spare, spare_body, 0)

    def wait_body(j, carry):
        copy(xs_scr, 0, 0).wait()
        return carry

    lax.fori_loop(0, n_started + n_tail, wait_body, 0)

    def spare_wait_body(j, carry):
        tile_copy(0).wait()
        return carry

    lax.fori_loop(0, n_spare, spare_wait_body, 0)


def _moe_sort(hx, plan, tms, ts):
    t, width = hx.shape
    nt = t // tms
    rows = _moe_max_rows(t, N_EXPERTS, tms, ts)
    return pl.pallas_call(
        _moe_sort_kernel,
        grid_spec=pltpu.PrefetchScalarGridSpec(
            num_scalar_prefetch=6,
            grid=(nt,),
            in_specs=[pl.BlockSpec((tms, width), lambda i, *_: (i, 0)),
                      pl.BlockSpec((None, 1, LANES), lambda i, *_: (i, 0, 0))],
            out_specs=pl.BlockSpec(memory_space=pl.ANY),
            scratch_shapes=[pltpu.VMEM((_local_rows(tms), width), BF16),
                            pltpu.VMEM((ts, width), BF16),
                            pltpu.SemaphoreType.DMA(()),
                            pltpu.SemaphoreType.DMA(())]),
        out_shape=jax.ShapeDtypeStruct((rows, width), BF16),
        compiler_params=_params("arbitrary"),
        name="moe_sort",
    )(plan["loc_start"], plan["n_chunks"], plan["dst_start"], plan["tail_start"], plan["tail_chunks"],
      plan["n_used"], hx, plan["loc_start_vec"])


def _moe_experts_kernel(te_ref, nu_ref, xs_ref, wg_ref, wu_ref, wd_ref, o_ref):
    j = pl.program_id(0)

    @pl.when(j < nu_ref[0])
    def _():
        d = xs_ref.shape[1] - LANES
        e = te_ref[j]
        x = xs_ref[:, :d]
        wblk = xs_ref[:, d:].astype(F32)
        lane = lax.broadcasted_iota(jnp.int32, wblk.shape, 1)
        mine = (lane == e) | (lane == e + N_EXPERTS) | (lane == e + 2 * N_EXPERTS)
        cw = jnp.sum(jnp.where(mine, wblk, 0.0), axis=-1, keepdims=True)
        hg = jnp.dot(x, wg_ref[...], preferred_element_type=F32)
        hu = jnp.dot(x, wu_ref[...], preferred_element_type=F32)
        act = hg * _sigmoid(hg) * hu * cw
        o_ref[...] = jnp.dot(act.astype(BF16), wd_ref[...], preferred_element_type=F32).astype(BF16)

    @pl.when(j >= nu_ref[0])
    def _():
        o_ref[...] = jnp.zeros_like(o_ref)


def _moe_experts(xs, plan, w_gate, w_up, w_down, ts):
    rows, width = xs.shape
    ne, d, f = w_gate.shape
    tile = lambda j, te, nu: (jnp.minimum(j, nu[0] - 1), 0)
    return pl.pallas_call(
        _moe_experts_kernel,
        grid_spec=pltpu.PrefetchScalarGridSpec(
            num_scalar_prefetch=2,
            grid=(rows // ts,),
            in_specs=[pl.BlockSpec((ts, width), tile),
                      pl.BlockSpec((None, d, f), lambda j, te, nu: (te[j], 0, 0)),
                      pl.BlockSpec((None, d, f), lambda j, te, nu: (te[j], 0, 0)),
                      pl.BlockSpec((None, f, d), lambda j, te, nu: (te[j], 0, 0))],
            out_specs=pl.BlockSpec((ts, d), lambda j, te, nu: (j, 0))),
        out_shape=jax.ShapeDtypeStruct((rows, d), BF16),
        compiler_params=_params("arbitrary"),
        name="moe_experts",
    )(plan["tile_expert"], plan["n_used"], xs, w_gate, w_up, w_down)


def _moe_combine_kernel(ls_ref, nch_ref, ds_ref, wblk_ref, lsv_ref, x1_ref, mod_ref, fw_ref, ys_hbm,
                        o_ref, ys_scr, sem):
    i = pl.program_id(0)
    tms = x1_ref.shape[0]
    loc = ys_scr.shape[0]

    @pl.when(i == 0)
    def _():
        ys_scr[...] = jnp.zeros_like(ys_scr)

    def copy(src_row, dst_row):
        return pltpu.make_async_copy(ys_hbm.at[pl.ds(pl.multiple_of(src_row, ROW_CHUNK), ROW_CHUNK)],
                                     ys_scr.at[pl.ds(pl.multiple_of(dst_row, ROW_CHUNK), ROW_CHUNK)], sem)

    n_started = jnp.int32(0)
    for e in range(N_EXPERTS):
        n = nch_ref[i * N_EXPERTS + e]
        l0 = ls_ref[i * N_EXPERTS + e]
        d0 = ds_ref[i * N_EXPERTS + e]

        def run_body(j, carry, l0=l0, d0=d0):
            copy(d0 + j * ROW_CHUNK, l0 + j * ROW_CHUNK).start()
            return carry

        lax.fori_loop(0, n, run_body, 0)
        n_started = n_started + n

    slot_a, slot_b = _tile_slots(wblk_ref[...].astype(F32), lsv_ref[...])
    scol = lax.broadcasted_iota(jnp.int32, (tms, loc), 1).astype(F32)
    perm = jnp.where((scol == slot_a) | (scol == slot_b), 1.0, 0.0).astype(BF16)

    def wait_body(j, carry):
        copy(0, 0).wait()
        return carry

    lax.fori_loop(0, n_started, wait_body, 0)
    y = jnp.dot(perm, ys_scr[...], preferred_element_type=F32)
    x2 = x1_ref[...] + mod_ref[5:6, :] * y
    o_ref[...] = x2 * lax.rsqrt(jnp.mean(x2 * x2, axis=-1, keepdims=True) + EPS) * fw_ref[...]


def _moe_combine(hx, ys, x1, mod, fw, plan, tms, tiles_per_seq):
    t, d = x1.shape
    nt = t // tms
    return pl.pallas_call(
        _moe_combine_kernel,
        grid_spec=pltpu.PrefetchScalarGridSpec(
            num_scalar_prefetch=3,
            grid=(nt,),
            in_specs=[pl.BlockSpec((tms, LANES), lambda i, *_: (i, d // LANES)),
                      pl.BlockSpec((None, 1, LANES), lambda i, *_: (i, 0, 0)),
                      pl.BlockSpec((tms, d), lambda i, *_: (i, 0)),
                      pl.BlockSpec((None, 6, d), lambda i, *_: (i // tiles_per_seq, 0, 0)),
                      pl.BlockSpec((1, d), lambda i, *_: (0, 0)),
                      pl.BlockSpec(memory_space=pl.ANY)],
            out_specs=pl.BlockSpec((tms, d), lambda i, *_: (i, 0)),
            scratch_shapes=[pltpu.VMEM((_local_rows(tms), d), BF16),
                            pltpu.SemaphoreType.DMA(())]),
        out_shape=jax.ShapeDtypeStruct((t, d), F32),
        compiler_params=_params("arbitrary"),
        name="moe_combine",
    )(plan["loc_start"], plan["n_chunks"], plan["dst_start"], hx, plan["loc_start_vec"], x1, mod, fw, ys)


def _pad_cols(w, n):
    return jnp.pad(w, ((0, 0), (0, n - w.shape[1])))


def kernel(x, c, ada_w, ada_b, mix_norm_w, w_in, rwkv_mu, rwkv_w0, rwkv_w_up, rwkv_a0, rwkv_a_up, rwkv_g_up, rwkv_k_k, rwkv_k_a, rwkv_r_k, rwkv_gn_w, rwkv_gn_b, mlstm_conv_w, mlstm_conv_b, mlstm_i_b, mlstm_f_b, mlstm_hn_w, w_out, ffn_norm_w, moe_w_group, moe_b_group, moe_w_router, moe_b_router, moe_w_gate, moe_w_up, moe_w_down, final_norm_w):
    bsz, seq, d = x.shape
    assert ada_w.shape[0] == 1, "the fused final norm assumes a single layer"
    assert seq % CHUNK == 0
    l = 0
    tm = min(512, seq)
    tms = min(SORT_TILE, seq)
    mod = _ada(c, ada_w[l], ada_b[l]).reshape(bsz, 6, d)
    w_r = w_in[l][:, :RWKV_COLS].astype(BF16)
    w_m = _pad_cols(w_in[l][:, RWKV_COLS:], MLSTM_PAD_COLS).astype(BF16)
    pr, pm = _in_proj(x, mod, mix_norm_w[l].reshape(1, d), w_r, w_m, tm)
    y_r = _rwkv(pr, rwkv_mu[l], rwkv_w0[l], rwkv_w_up[l], rwkv_a0[l], rwkv_a_up[l], rwkv_g_up[l],
                rwkv_k_k[l], rwkv_k_a[l], rwkv_r_k[l], rwkv_gn_w[l], rwkv_gn_b[l])
    y_m = _mlstm(pm, mlstm_conv_w[l], mlstm_conv_b[l], mlstm_i_b[l], mlstm_f_b[l], mlstm_hn_w[l])
    wo = w_out[l].astype(BF16)
    x1, hx = _mix_out(
        x, y_r, y_m, mod, ffn_norm_w[l].reshape(1, d), wo[:RWKV_WIDTH], wo[RWKV_WIDTH:],
        _pad_cols(moe_w_group[l], LANES), _pad_cols(moe_b_group[l].reshape(1, -1), LANES),
        _pad_cols(moe_w_router[l], LANES), _pad_cols(moe_b_router[l].reshape(1, -1), LANES), tm)
    hx = hx.reshape(bsz * seq, d + LANES)
    flags = hx[:, d + 3 * N_EXPERTS:] != 0
    plan = _moe_plan(flags, tms, EXPERT_TILE)
    xs = _moe_sort(hx, plan, tms, EXPERT_TILE)
    ys = _moe_experts(xs, plan, moe_w_gate[l].astype(BF16), moe_w_up[l].astype(BF16),
                      moe_w_down[l].astype(BF16), EXPERT_TILE)
    out = _moe_combine(hx, ys, x1.reshape(bsz * seq, d), mod, final_norm_w.reshape(1, d), plan, tms, seq // tms)
    return out.reshape(bsz, seq, d)
```

```python
import jax
import jax.numpy as jnp
from jax import lax
from jax.experimental import pallas as pl
from jax.experimental.pallas import tpu as pltpu

F32 = jnp.float32
BF16 = jnp.bfloat16
HIGHEST = lax.Precision.HIGHEST

CHUNK = 64
RWKV_STEP_ROWS = 256
MLSTM_STEP_ROWS = 256
EPS = 1e-6
RWKV_HEAD = 64
RWKV_HEADS = 8
RWKV_WIDTH = RWKV_HEADS * RWKV_HEAD
DECAY_LORA = 64
ICLR_LORA = 64
GATE_LORA = 128
GN_EPS = 64e-5
MLSTM_HEADS = 4
MLSTM_DK = 64
MLSTM_DV = 128
MLSTM_QK = MLSTM_HEADS * MLSTM_DK
MLSTM_WIDTH = MLSTM_HEADS * MLSTM_DV
CONV_WIDTH = 4
GATE_SOFTCAP = 15.0
RWKV_COLS = 3 * RWKV_WIDTH + DECAY_LORA + ICLR_LORA + GATE_LORA
N_GROUPS = 4
EXPERTS_PER_GROUP = 8
N_EXPERTS = N_GROUPS * EXPERTS_PER_GROUP
LANES = 128
MLSTM_MAIN = 2 * MLSTM_QK + 2 * MLSTM_WIDTH
MLSTM_PAD_COLS = MLSTM_MAIN + LANES
VMEM_LIMIT = 56 * 1024 * 1024


def _mm(a, b):
    return jnp.dot(a.astype(BF16), b.astype(BF16), preferred_element_type=F32)


def _nt(a, b):
    return lax.dot_general(a.astype(BF16), b.astype(BF16), (((1,), (1,)), ((), ())),
                           preferred_element_type=F32)


def _tn(a, b):
    return lax.dot_general(a.astype(BF16), b.astype(BF16), (((0,), (0,)), ((), ())),
                           preferred_element_type=F32)


def _mm_f32(a, b):
    return jnp.dot(a, b, preferred_element_type=F32, precision=HIGHEST)


def _sigmoid(z):
    return 1.0 / (1.0 + jnp.exp(-z))


def _softplus(z):
    return jnp.maximum(z, 0.0) + jnp.log1p(jnp.exp(-jnp.abs(z)))


def _params(*sem):
    return pltpu.CompilerParams(dimension_semantics=sem, vmem_limit_bytes=VMEM_LIMIT)


def _ada_kernel(c_ref, w_ref, b_ref, o_ref):
    c = c_ref[...]
    o_ref[...] = _mm_f32(c * _sigmoid(c), w_ref[...]) + b_ref[...]


def _ada(c, w, b):
    bsz, d = c.shape
    n = w.shape[1]
    return pl.pallas_call(
        _ada_kernel,
        grid=(n // d,),
        in_specs=[pl.BlockSpec((bsz, d), lambda j: (0, 0)),
                  pl.BlockSpec((d, d), lambda j: (0, j)),
                  pl.BlockSpec((1, d), lambda j: (0, j))],
        out_specs=pl.BlockSpec((bsz, d), lambda j: (0, j)),
        out_shape=jax.ShapeDtypeStruct((bsz, n), F32),
        compiler_params=_params("arbitrary"),
        name="ada",
    )(c, w, b.reshape(1, n))


def _norm_mod(x, nw, shift, scale):
    y = x * lax.rsqrt(jnp.mean(x * x, axis=-1, keepdims=True) + EPS) * nw
    return y * (1.0 + scale) + shift


def _in_proj_kernel(x_ref, mod_ref, nw_ref, wr_ref, wm_ref, pr_ref, pm_ref):
    h = _norm_mod(x_ref[...], nw_ref[...], mod_ref[0:1, :], mod_ref[1:2, :]).astype(BF16)
    pr_ref[...] = jnp.dot(h, wr_ref[...], preferred_element_type=F32)
    pm_ref[...] = jnp.dot(h, wm_ref[...], preferred_element_type=F32)


def _in_proj(x, mod, nw, w_r, w_m, tm):
    bsz, seq, d = x.shape
    nr, nm = w_r.shape[1], w_m.shape[1]
    return pl.pallas_call(
        _in_proj_kernel,
        grid=(bsz, seq // tm),
        in_specs=[pl.BlockSpec((None, tm, d), lambda b, i: (b, i, 0)),
                  pl.BlockSpec((None, 6, d), lambda b, i: (b, 0, 0)),
                  pl.BlockSpec((1, d), lambda b, i: (0, 0)),
                  pl.BlockSpec((d, nr), lambda b, i: (0, 0)),
                  pl.BlockSpec((d, nm), lambda b, i: (0, 0))],
        out_specs=[pl.BlockSpec((None, tm, nr), lambda b, i: (b, i, 0)),
                   pl.BlockSpec((None, tm, nm), lambda b, i: (b, i, 0))],
        out_shape=[jax.ShapeDtypeStruct((bsz, seq, nr), F32),
                   jax.ShapeDtypeStruct((bsz, seq, nm), F32)],
        compiler_params=_params("arbitrary", "arbitrary"),
        name="in_proj",
    )(x, mod, nw, w_r, w_m)


def _rwkv_kernel(pr_ref, mu_ref, w0_ref, wup_ref, a0_ref, aup_ref, gup_ref, kk_ref, ka_ref,
                 rk_ref, gnw_ref, gnb_ref, o_ref, s_scr, prev_scr):
    C = CHUNK
    N = RWKV_HEAD
    rows = pr_ref.shape[0]
    nc = rows // C

    @pl.when(pl.program_id(1) == 0)
    def _():
        s_scr[...] = jnp.zeros_like(s_scr)
        prev_scr[...] = jnp.zeros_like(prev_scr)

    x = pr_ref[...]
    row1 = lax.broadcasted_iota(jnp.int32, x.shape, 0)
    prev = jnp.where(row1 == 0, prev_scr[...], pltpu.roll(x, 1, axis=0))
    prev_scr[...] = x[rows - 1:rows, :]
    u = x + (prev - x) * mu_ref[...]

    o1, o2, o3 = RWKV_WIDTH, 2 * RWKV_WIDTH, 3 * RWKV_WIDTH
    o4 = o3 + DECAY_LORA
    o5 = o4 + ICLR_LORA
    r, k, v = u[:, :o1], u[:, o1:o2], u[:, o2:o3]
    wd, ad, gd = u[:, o3:o4], u[:, o4:o5], u[:, o5:]
    w_log = -_softplus(-(w0_ref[...] + _mm(jnp.tanh(wd), wup_ref[...]))) - 0.5
    lw = -jnp.exp(w_log)
    a = _sigmoid(a0_ref[...] + _mm(ad, aup_ref[...]))
    g = _mm(_sigmoid(gd), gup_ref[...])
    kk_all = k * kk_ref[...]
    k2_all = k * (1.0 + (a - 1.0) * ka_ref[...])
    rk_all = r * k2_all * rk_ref[...]

    rr = lax.broadcasted_iota(jnp.int32, (rows, rows), 0)
    cc = lax.broadcasted_iota(jnp.int32, (rows, rows), 1)
    in_chunk = (cc <= rr) & ((rr >> 6) == (cc >> 6))
    cum = _mm_f32(in_chunk.astype(F32), lw)

    row = lax.broadcasted_iota(jnp.int32, (C, C), 0)
    col = lax.broadcasted_iota(jnp.int32, (C, C), 1)
    same16 = (row >> 4) == (col >> 4)
    same32 = (row >> 5) == (col >> 5)
    eye = (row == col).astype(F32)
    row2 = lax.broadcasted_iota(jnp.int32, (C, 2 * C), 0)
    col2 = lax.broadcasted_iota(jnp.int32, (C, 2 * C), 1) & (C - 1)
    strict2 = col2 < row2
    incl2 = col2 <= row2

    units = [(j, h) for j in range(nc) for h in range(RWKV_HEADS)]
    each = lambda f, *ls: [f(*t) for t in zip(*ls)]
    cut = lambda t: [t[j * C:(j + 1) * C, h * N:(h + 1) * N] for j, h in units]
    r_h, v_h, a_h, lw_h, cum_h, kk, k2 = (cut(t) for t in (r, v, a, lw, cum, kk_all, k2_all))
    kk = each(lambda t: t / jnp.maximum(jnp.sqrt(jnp.sum(t * t, axis=-1, keepdims=True)), 1e-12), kk)
    b_h = each(lambda t, a_: t * a_, kk, a_h)
    cl = each(lambda t: t[C - 1:C, :], cum_h)
    p_inv = each(lambda t: jnp.exp(-t), cum_h)
    p_rest = each(lambda c_, t: jnp.exp(c_ - t), cl, cum_h)
    ah = each(lambda kk_, t, w_: -kk_ * jnp.exp(t - w_), kk, cum_h, lw_h)
    rh = each(lambda r_, t: r_ * jnp.exp(t), r_h, cum_h)
    ar = each(lambda a_, r_: jnp.concatenate([a_, r_], axis=0), ah, rh)
    bk = each(lambda b_, k_, p_: jnp.concatenate([b_ * p_, k_ * p_], axis=0), b_h, k2, p_inv)
    x1 = each(_nt, ar, bk)
    l_both = each(lambda t: jnp.where(strict2, t[:C, :], 0.0), x1)
    m_both = each(lambda t: jnp.where(incl2, t[C:, :], 0.0), x1)
    l_ab = each(lambda t: t[:, :C], l_both)
    n1 = each(lambda t: jnp.where(same16, t, 0.0), l_ab)
    xi = each(lambda t: eye + t, n1)
    nk = n1
    for _ in range(3):
        nk = each(lambda t: _mm(t, t), nk)
        xi = each(lambda x_, t: x_ + _mm(x_, t), xi, nk)
    e1 = each(lambda t: jnp.where(same32 & jnp.logical_not(same16), t, 0.0), l_ab)
    xe = each(_mm, xi, e1)
    xi = each(lambda x_, t: x_ + _mm(t, x_), xi, xe)
    e2 = each(lambda t: jnp.where(same32, 0.0, t), l_ab)
    xe = each(_mm, xi, e2)
    xi = each(lambda x_, t: x_ + _mm(t, x_), xi, xe)
    zero = jnp.zeros((C, N), F32)
    z = each(lambda l_, v_: _mm(l_, jnp.concatenate([zero, v_], axis=0)), l_both, v_h)
    a2 = each(_mm, xi, ah)
    u0 = each(_mm, xi, z)
    uv = each(lambda u_, v_: jnp.concatenate([u_, v_], axis=0), u0, v_h)
    r2 = each(lambda r_, m_, a_: r_ + _mm(m_[:, :C], a_), rh, m_both, a2)
    y0 = each(_mm, m_both, uv)
    b2 = each(lambda b_, p_: b_ * p_, b_h, p_rest)
    gt = each(_tn, a2, b2)
    ht = each(lambda uv_, b_, k_, p_: _tn(uv_, jnp.concatenate([b_, k_ * p_], axis=0)), uv, b2, k2, p_rest)
    p_end = each(jnp.exp, cl)
    state = [s_scr[h] for h in range(RWKV_HEADS)]
    y = []
    for i, (j, h) in enumerate(units):
        y.append(_nt(r2[i], state[h]) + y0[i])
        state[h] = state[h] * p_end[i] + _mm(state[h], gt[i]) + ht[i]
    for h in range(RWKV_HEADS):
        s_scr[h] = state[h]
    rk_sum = each(lambda t: jnp.sum(t, axis=-1, keepdims=True), cut(rk_all))
    yc = each(lambda t: t - jnp.mean(t, axis=-1, keepdims=True), y)
    var = each(lambda t: jnp.mean(t * t, axis=-1, keepdims=True), yc)
    for i, (j, h) in enumerate(units):
        sl = slice(h * N, (h + 1) * N)
        yn = yc[i] * lax.rsqrt(var[i] + GN_EPS) * gnw_ref[:, sl] + gnb_ref[:, sl]
        o_ref[j * C:(j + 1) * C, sl] = (yn + rk_sum[i] * v_h[i]) * g[j * C:(j + 1) * C, sl]


def _rwkv(pr, mu, w0, w_up, a0, a_up, g_up, k_k, k_a, r_k, gn_w, gn_b, rows):
    bsz, seq, cols = pr.shape
    row = lambda p: p.reshape(1, -1)
    vec = lambda n: pl.BlockSpec((1, n), lambda b, c: (0, 0))
    mat = lambda m: pl.BlockSpec(m.shape, lambda b, c: (0, 0))
    return pl.pallas_call(
        _rwkv_kernel,
        grid=(bsz, seq // rows),
        in_specs=[pl.BlockSpec((None, rows, cols), lambda b, c: (b, c, 0)),
                  vec(cols), vec(RWKV_WIDTH), mat(w_up), vec(RWKV_WIDTH), mat(a_up), mat(g_up),
                  vec(RWKV_WIDTH), vec(RWKV_WIDTH), vec(RWKV_WIDTH), vec(RWKV_WIDTH), vec(RWKV_WIDTH)],
        out_specs=pl.BlockSpec((None, rows, RWKV_WIDTH), lambda b, c: (b, c, 0)),
        out_shape=jax.ShapeDtypeStruct((bsz, seq, RWKV_WIDTH), F32),
        scratch_shapes=[pltpu.VMEM((RWKV_HEADS, RWKV_HEAD, RWKV_HEAD), F32),
                        pltpu.VMEM((1, cols), F32)],
        compiler_params=_params("arbitrary", "arbitrary"),
        name="rwkv",
    )(pr, row(mu), row(w0), w_up, row(a0), a_up, g_up, row(k_k), row(k_a), row(r_k), row(gn_w), row(gn_b))


def _mlstm_kernel(pm_ref, cw_ref, cb_ref, gb_ref, hn_ref, o_ref, ext_scr, c_scr, n_scr, m_scr):
    C = CHUNK
    DK, DV = MLSTM_DK, MLSTM_DV
    QK2 = 2 * MLSTM_QK
    rows = pm_ref.shape[0]
    nc = rows // C

    @pl.when(pl.program_id(1) == 0)
    def _():
        ext_scr[0:8, :] = jnp.zeros((8, QK2), F32)
        c_scr[...] = jnp.zeros_like(c_scr)
        n_scr[...] = jnp.zeros_like(n_scr)
        m_scr[...] = jnp.zeros_like(m_scr)

    ext_scr[8:8 + rows, :] = pm_ref[:, 0:QK2]
    conv = cb_ref[...] + cw_ref[0:1, :] * ext_scr[5:5 + rows, :]
    for i in range(1, CONV_WIDTH):
        conv = conv + cw_ref[i:i + 1, :] * ext_scr[5 + i:5 + i + rows, :]
    ext_scr[0:8, :] = ext_scr[rows:rows + 8, :]
    qk = conv * _sigmoid(conv)
    q = qk[:, :MLSTM_QK] * (DK ** -0.5)
    k = qk[:, MLSTM_QK:]
    v = pm_ref[:, QK2:QK2 + MLSTM_WIDTH]
    og = pm_ref[:, QK2 + MLSTM_WIDTH:MLSTM_MAIN]

    z = GATE_SOFTCAP * jnp.tanh((pm_ref[:, MLSTM_MAIN:] + gb_ref[...]) / GATE_SOFTCAP)
    log_f = jnp.minimum(z, 0.0) - jnp.log1p(jnp.exp(-jnp.abs(z)))
    rr = lax.broadcasted_iota(jnp.int32, (rows, rows), 0)
    cc = lax.broadcasted_iota(jnp.int32, (rows, rows), 1)
    in_chunk = (cc <= rr) & ((rr >> 6) == (cc >> 6))
    lane = lax.broadcasted_iota(jnp.int32, (rows, LANES), 1)
    gates = jnp.where(lane < MLSTM_HEADS, z, _mm_f32(in_chunk.astype(F32), log_f))
    ri = lax.broadcasted_iota(jnp.int32, (LANES, LANES), 0)
    ci = lax.broadcasted_iota(jnp.int32, (LANES, LANES), 1)
    gates_t = lax.dot_general((ri == ci).astype(F32), gates, (((1,), (1,)), ((), ())),
                              preferred_element_type=F32, precision=HIGHEST)
    row = lax.broadcasted_iota(jnp.int32, (C, C), 0)
    col = lax.broadcasted_iota(jnp.int32, (C, C), 1)
    incl = col <= row

    units = [(j, h) for j in range(nc) for h in range(MLSTM_HEADS)]
    each = lambda f, *ls: [f(*t) for t in zip(*ls)]
    li_col = [gates[j * C:(j + 1) * C, h:h + 1] for j, h in units]
    b_col = [gates[j * C:(j + 1) * C, MLSTM_HEADS + h:MLSTM_HEADS + h + 1] for j, h in units]
    li_row = [gates_t[h:h + 1, j * C:(j + 1) * C] for j, h in units]
    b_row = [gates_t[MLSTM_HEADS + h:MLSTM_HEADS + h + 1, j * C:(j + 1) * C] for j, h in units]
    q_h = [q[j * C:(j + 1) * C, h * DK:(h + 1) * DK] for j, h in units]
    k_h = [k[j * C:(j + 1) * C, h * DK:(h + 1) * DK] for j, h in units]
    v_h = [v[j * C:(j + 1) * C, h * DV:(h + 1) * DV] for j, h in units]
    b_last = each(lambda bc: bc[C - 1:C, :], b_col)
    end_log = each(lambda bl, bc, lc: bl - bc + lc, b_last, b_col, li_col)
    end_max = each(lambda el: jnp.max(el, axis=0, keepdims=True), end_log)
    m_run = [m_scr[h][0:1, 0:1] for h in range(MLSTM_HEADS)]
    m, m_new = [], []
    for i, (j, h) in enumerate(units):
        m.append(m_run[h])
        m_run[h] = jnp.maximum(b_last[i] + m_run[h], end_max[i])
        m_new.append(m_run[h])
    qk_h = each(_nt, q_h, k_h)
    d_mat = each(lambda bc, br, lr: jnp.where(incl, bc - br + lr, -jnp.inf), b_col, b_row, li_row)
    inter = each(lambda bc, m_: bc + m_, b_col, m)
    m_t = each(lambda i_, d_: jnp.maximum(i_, jnp.max(d_, axis=-1, keepdims=True)), inter, d_mat)
    scores = each(lambda s_, d_, mt: s_ * jnp.exp(d_ - mt), qk_h, d_mat, m_t)
    w_inter = each(lambda i_, mt: jnp.exp(i_ - mt), inter, m_t)
    sv_h = each(_mm, scores, v_h)
    carry_w = each(lambda bl, m_, mn: jnp.exp(bl + m_ - mn), b_last, m, m_new)
    kw = each(lambda k_, el, mn: k_ * jnp.exp(el - mn), k_h, end_log, m_new)
    kv_h = each(_tn, kw, v_h)
    c_run = [c_scr[h] for h in range(MLSTM_HEADS)]
    n_run = [n_scr[h][0:1, 0:DK] for h in range(MLSTM_HEADS)]
    c_mat, n_vec = [], []
    for i, (j, h) in enumerate(units):
        c_mat.append(c_run[h])
        n_vec.append(n_run[h])
        c_run[h] = carry_w[i] * c_run[h] + kv_h[i]
        n_run[h] = carry_w[i] * n_run[h] + jnp.sum(kw[i], axis=0, keepdims=True)
    for h in range(MLSTM_HEADS):
        c_scr[h] = c_run[h]
        n_scr[h] = jnp.broadcast_to(jnp.concatenate([n_run[h], jnp.zeros((1, LANES - DK), F32)], axis=1), (8, LANES))
        m_scr[h] = jnp.broadcast_to(m_run[h], (8, LANES))
    qc_h = each(_mm, q_h, c_mat)
    num = each(lambda s_, w_, c_: s_ + w_ * c_, sv_h, w_inter, qc_h)
    s_sum = each(lambda s_: jnp.sum(s_, axis=-1, keepdims=True), scores)
    qn = each(lambda q_, n_: jnp.sum(q_ * n_, axis=-1, keepdims=True), q_h, n_vec)
    den = each(lambda s_, w_, q_: s_ + w_ * q_, s_sum, w_inter, qn)
    hh = each(lambda n_, d_, mt: n_ / jnp.maximum(jnp.abs(d_), jnp.exp(-mt)), num, den, m_t)
    ms = each(lambda t: jnp.mean(t * t, axis=-1, keepdims=True), hh)
    for i, (j, h) in enumerate(units):
        hn = hh[i] * lax.rsqrt(ms[i] + EPS) * hn_ref[:, h * DV:(h + 1) * DV]
        o_ref[j * C:(j + 1) * C, h * DV:(h + 1) * DV] = hn * _sigmoid(og[j * C:(j + 1) * C, h * DV:(h + 1) * DV])


def _mlstm(pm, conv_w, conv_b, i_b, f_b, hn_w, rows):
    bsz, seq, cols = pm.shape
    gate_b = jnp.zeros((1, LANES), F32).at[0, :MLSTM_HEADS].set(i_b).at[0, MLSTM_HEADS:2 * MLSTM_HEADS].set(f_b)
    full = lambda a: pl.BlockSpec(a.shape, lambda b, c: (0, 0))
    conv_b = conv_b.reshape(1, -1)
    hn_w = hn_w.reshape(1, -1)
    return pl.pallas_call(
        _mlstm_kernel,
        grid=(bsz, seq // rows),
        in_specs=[pl.BlockSpec((None, rows, cols), lambda b, c: (b, c, 0)),
                  full(conv_w), full(conv_b), full(gate_b), full(hn_w)],
        out_specs=pl.BlockSpec((None, rows, MLSTM_WIDTH), lambda b, c: (b, c, 0)),
        out_shape=jax.ShapeDtypeStruct((bsz, seq, MLSTM_WIDTH), F32),
        scratch_shapes=[pltpu.VMEM((rows + 8, 2 * MLSTM_QK), F32),
                        pltpu.VMEM((MLSTM_HEADS, MLSTM_DK, MLSTM_DV), F32),
                        pltpu.VMEM((MLSTM_HEADS, 8, LANES), F32),
                        pltpu.VMEM((MLSTM_HEADS, 8, LANES), F32)],
        compiler_params=_params("arbitrary", "arbitrary"),
        name="mlstm",
    )(pm, conv_w, conv_b, gate_b, hn_w)


def _mix_out_kernel(x_ref, yr_ref, ym_ref, mod_ref, nw_ref, wor_ref, wom_ref, wr_ref, br_ref, x1_ref, hx_ref):
    d = x_ref.shape[-1]
    y = (jnp.dot(yr_ref[...].astype(BF16), wor_ref[...], preferred_element_type=F32)
         + jnp.dot(ym_ref[...].astype(BF16), wom_ref[...], preferred_element_type=F32))
    x1 = x_ref[...] + mod_ref[2:3, :] * y
    x1_ref[...] = x1
    h = _norm_mod(x1, nw_ref[...], mod_ref[3:4, :], mod_ref[4:5, :])
    h_hi = h.astype(BF16)
    hx_ref[:, :d] = h_hi

    h_lo = (h - h_hi.astype(F32)).astype(BF16)
    w = wr_ref[...]
    w_hi = w.astype(BF16)
    w_lo = (w - w_hi.astype(F32)).astype(BF16)
    hw = jnp.dot(h_hi, jnp.concatenate([w_hi, w_lo], axis=1), preferred_element_type=F32)
    logits = hw[:, :LANES] + hw[:, LANES:] + jnp.dot(h_lo, w_hi, preferred_element_type=F32) + br_ref[...]
    lane = lax.broadcasted_iota(jnp.int32, logits.shape, 1)
    neg = -jnp.inf
    el = logits
    gl = jnp.where((lane >= N_EXPERTS) & (lane < N_EXPERTS + N_GROUPS), logits, neg)
    g_max = jnp.max(gl, axis=-1, keepdims=True)
    g_idx = jnp.min(jnp.where(gl == g_max, lane, LANES), axis=-1, keepdims=True) - N_EXPERTS
    g_w = 1.0 / jnp.sum(jnp.exp(gl - g_max), axis=-1, keepdims=True)
    in_group = ((lane >> 3) == g_idx) & (lane < N_EXPERTS)
    sel = jnp.where(in_group, el, neg)
    v1 = jnp.max(sel, axis=-1, keepdims=True)
    i1 = jnp.min(jnp.where(sel == v1, lane, LANES), axis=-1, keepdims=True)
    rest = jnp.where(lane == i1, neg, sel)
    v2 = jnp.max(rest, axis=-1, keepdims=True)
    i2 = jnp.min(jnp.where(rest == v2, lane, LANES), axis=-1, keepdims=True)
    e2 = jnp.exp(v2 - v1)
    w1 = 1.0 / (1.0 + e2)
    w2 = e2 / (1.0 + e2)
    cw = jnp.where(lane == i1, w1 * g_w, jnp.where(lane == i2, w2 * g_w, 0.0))
    hi = cw.astype(BF16).astype(F32)
    mid = (cw - hi).astype(BF16).astype(F32)
    lo = (cw - hi - mid).astype(BF16).astype(F32)
    flag = jnp.where(hi != 0.0, 1.0, 0.0)
    pack = (hi + pltpu.roll(mid, N_EXPERTS, axis=1) + pltpu.roll(lo, 2 * N_EXPERTS, axis=1)
            + pltpu.roll(flag, 3 * N_EXPERTS, axis=1))
    hx_ref[:, d:] = pack.astype(BF16)


def _mix_out(x, y_r, y_m, mod, nw, wo_r, wo_m, w_route, b_route, tm):
    bsz, seq, d = x.shape
    tok = lambda n: pl.BlockSpec((None, tm, n), lambda b, i: (b, i, 0))
    full = lambda a: pl.BlockSpec(a.shape, lambda b, i: (0, 0))
    return pl.pallas_call(
        _mix_out_kernel,
        grid=(bsz, seq // tm),
        in_specs=[tok(d), tok(RWKV_WIDTH), tok(MLSTM_WIDTH),
                  pl.BlockSpec((None, 6, d), lambda b, i: (b, 0, 0)),
                  full(nw), full(wo_r), full(wo_m), full(w_route), full(b_route)],
        out_specs=[tok(d), tok(d + LANES)],
        out_shape=[jax.ShapeDtypeStruct((bsz, seq, d), F32),
                   jax.ShapeDtypeStruct((bsz, seq, d + LANES), BF16)],
        compiler_params=_params("arbitrary", "arbitrary"),
        name="mix_out",
    )(x, y_r, y_m, mod, nw, wo_r, wo_m, w_route, b_route)


ROW_CHUNK = 16
SORT_TILE = 512
EXPERT_TILE = 512
NO_SLOT = 1e9


def _moe_plan(flags, tms, ts):
    t, ne = flags.shape
    nt = t // tms
    i32 = jnp.int32
    tri = lambda n: (jnp.arange(n)[:, None] < jnp.arange(n)[None, :]).astype(F32)
    before = lambda a, b: jnp.dot(a, b, precision=HIGHEST).astype(i32)
    cnt = flags.reshape(nt, tms, ne).astype(i32).sum(axis=1)
    pc = (cnt + ROW_CHUNK - 1) // ROW_CHUNK * ROW_CHUNK
    pcf = pc.astype(F32)
    loc_start = before(pcf, tri(ne))
    len_e = pc.sum(axis=0)
    seg_e = (len_e + ts - 1) // ts * ts
    e_start = before(seg_e.astype(F32)[None, :], tri(ne))[0]
    e_end = e_start + seg_e
    dst_start = e_start[None, :] + before(tri(nt).T, pcf)
    n_tiles_max = _moe_max_rows(t, ne, tms, ts) // ts
    tile_row = jnp.arange(n_tiles_max, dtype=i32) * ts
    tile_expert = jnp.minimum((tile_row[:, None] >= e_end[None, :]).astype(i32).sum(axis=1), ne - 1)
    return dict(
        loc_start=loc_start.reshape(-1).astype(i32), n_chunks=(pc // ROW_CHUNK).reshape(-1).astype(i32),
        dst_start=dst_start.reshape(-1).astype(i32), tail_start=(e_start + len_e).astype(i32),
        tail_chunks=((seg_e - len_e) // ROW_CHUNK).astype(i32),
        n_used=(e_end[-1:] // ts).astype(i32), tile_expert=tile_expert,
        loc_start_vec=jnp.pad(loc_start.astype(F32), ((0, 0), (0, LANES - ne))).reshape(nt, 1, LANES))


def _moe_max_rows(t, ne, tms, ts):
    rows = 2 * t + (t // tms) * ne * (ROW_CHUNK - 1) + ne * (ts - 1)
    return (rows + ts - 1) // ts * ts


def _local_rows(tms):
    return 2 * tms + N_EXPERTS * ROW_CHUNK


def _tile_slots(blk, loc_start_vec):
    tms = blk.shape[0]
    lane = lax.broadcasted_iota(jnp.int32, blk.shape, 1)
    m = jnp.where(lane < N_EXPERTS, pltpu.roll(blk, N_EXPERTS, axis=1), 0.0)
    row = lax.broadcasted_iota(jnp.int32, (tms, tms), 0)
    col = lax.broadcasted_iota(jnp.int32, (tms, tms), 1)
    before = jnp.dot((col < row).astype(BF16), m.astype(BF16), preferred_element_type=F32)
    slot = loc_start_vec + before
    slot_a = jnp.min(jnp.where(m > 0.0, slot, NO_SLOT), axis=-1, keepdims=True)
    slot_b = jnp.max(jnp.where(m > 0.0, slot, -NO_SLOT), axis=-1, keepdims=True)
    return slot_a, slot_b


def _moe_sort_kernel(ls_ref, nch_ref, ds_ref, tls_ref, tlc_ref, nu_ref, hx_ref, lsv_ref, xo_ref, xs_scr, z_scr,
                     sem, zsem):
    i = pl.program_id(0)
    tms, width = hx_ref.shape
    loc = xs_scr.shape[0]
    d = width - LANES
    slot_a, slot_b = _tile_slots(hx_ref[:, d:].astype(F32), lsv_ref[...])
    lane = lax.broadcasted_iota(jnp.int32, (tms, LANES), 1)
    packed = jnp.where(lane == 0, slot_a, jnp.where(lane == 1, slot_b, 0.0))
    ri = lax.broadcasted_iota(jnp.int32, (LANES, LANES), 0)
    ci = lax.broadcasted_iota(jnp.int32, (LANES, LANES), 1)
    slots_t = lax.dot_general((ri == ci).astype(F32), packed, (((1,), (1,)), ((), ())),
                              preferred_element_type=F32, precision=HIGHEST)
    srow = lax.broadcasted_iota(jnp.int32, (loc, tms), 0).astype(F32)
    perm = jnp.where((srow == slots_t[0:1, :]) | (srow == slots_t[1:2, :]), 1.0, 0.0).astype(BF16)
    xs_scr[...] = jnp.dot(perm, hx_ref[...], preferred_element_type=F32).astype(BF16)

    def copy(src, src_row, dst_row):
        return pltpu.make_async_copy(src.at[pl.ds(pl.multiple_of(src_row, ROW_CHUNK), ROW_CHUNK)],
                                     xo_ref.at[pl.ds(pl.multiple_of(dst_row, ROW_CHUNK), ROW_CHUNK)], sem)

    n_started = jnp.int32(0)
    for e in range(N_EXPERTS):
        n = nch_ref[i * N_EXPERTS + e]
        l0 = ls_ref[i * N_EXPERTS + e]
        d0 = ds_ref[i * N_EXPERTS + e]

        def run_body(j, carry, l0=l0, d0=d0):
            copy(xs_scr, l0 + j * ROW_CHUNK, d0 + j * ROW_CHUNK).start()
            return carry

        lax.fori_loop(0, n, run_body, 0)
        n_started = n_started + n

    z_scr[...] = jnp.zeros_like(z_scr)
    ts = z_scr.shape[0]
    n_tail = jnp.int32(0)
    for e in range(N_EXPERTS):
        n = jnp.where(i == 0, tlc_ref[e], 0)
        t0 = tls_ref[e]

        def tail_body(j, carry, t0=t0):
            copy(z_scr, 0, t0 + j * ROW_CHUNK).start()
            return carry

        lax.fori_loop(0, n, tail_body, 0)
        n_tail = n_tail + n

    def tile_copy(tile):
        return pltpu.make_async_copy(z_scr, xo_ref.at[pl.ds(pl.multiple_of(tile * ts, ts), ts)], zsem)

    n_spare = jnp.where(i == 0, xo_ref.shape[0] // ts - nu_ref[0], 0)

    def spare_body(j, carry):
        tile_copy(nu_ref[0] + j).start()
        return carry

    lax.fori_loop(0, n_spare, spare_body, 0)

    def wait_body(j, carry):
        copy(xs_scr, 0, 0).wait()
        return carry

    lax.fori_loop(0, n_started + n_tail, wait_body, 0)

    def spare_wait_body(j, carry):
        tile_copy(0).wait()
        return carry

    lax.fori_loop(0, n_spare, spare_wait_body, 0)


def _moe_sort(hx, plan, tms, ts):
    t, width = hx.shape
    nt = t // tms
    rows = _moe_max_rows(t, N_EXPERTS, tms, ts)
    return pl.pallas_call(
        _moe_sort_kernel,
        grid_spec=pltpu.PrefetchScalarGridSpec(
            num_scalar_prefetch=6,
            grid=(nt,),
            in_specs=[pl.BlockSpec((tms, width), lambda i, *_: (i, 0)),
                      pl.BlockSpec((None, 1, LANES), lambda i, *_: (i, 0, 0))],
            out_specs=pl.BlockSpec(memory_space=pl.ANY),
            scratch_shapes=[pltpu.VMEM((_local_rows(tms), width), BF16),
                            pltpu.VMEM((ts, width), BF16),
                            pltpu.SemaphoreType.DMA(()),
                            pltpu.SemaphoreType.DMA(())]),
        out_shape=jax.ShapeDtypeStruct((rows, width), BF16),
        compiler_params=_params("arbitrary"),
        name="moe_sort",
    )(plan["loc_start"], plan["n_chunks"], plan["dst_start"], plan["tail_start"], plan["tail_chunks"],
      plan["n_used"], hx, plan["loc_start_vec"])


def _moe_experts_kernel(te_ref, nu_ref, xs_ref, wg_ref, wu_ref, wd_ref, o_ref):
    j = pl.program_id(0)

    @pl.when(j < nu_ref[0])
    def _():
        d = xs_ref.shape[1] - LANES
        e = te_ref[j]
        x = xs_ref[:, :d]
        wblk = xs_ref[:, d:].astype(F32)
        lane = lax.broadcasted_iota(jnp.int32, wblk.shape, 1)
        mine = (lane == e) | (lane == e + N_EXPERTS) | (lane == e + 2 * N_EXPERTS)
        cw = jnp.sum(jnp.where(mine, wblk, 0.0), axis=-1, keepdims=True)
        hg = jnp.dot(x, wg_ref[...], preferred_element_type=F32)
        hu = jnp.dot(x, wu_ref[...], preferred_element_type=F32)
        act = hg * _sigmoid(hg) * hu * cw
        o_ref[...] = jnp.dot(act.astype(BF16), wd_ref[...], preferred_element_type=F32).astype(BF16)

    @pl.when(j >= nu_ref[0])
    def _():
        o_ref[...] = jnp.zeros_like(o_ref)


def _moe_experts(xs, plan, w_gate, w_up, w_down, ts):
    rows, width = xs.shape
    ne, d, f = w_gate.shape
    tile = lambda j, te, nu: (jnp.minimum(j, nu[0] - 1), 0)
    return pl.pallas_call(
        _moe_experts_kernel,
        grid_spec=pltpu.PrefetchScalarGridSpec(
            num_scalar_prefetch=2,
            grid=(rows // ts,),
            in_specs=[pl.BlockSpec((ts, width), tile),
                      pl.BlockSpec((None, d, f), lambda j, te, nu: (te[j], 0, 0)),
                      pl.BlockSpec((None, d, f), lambda j, te, nu: (te[j], 0, 0)),
                      pl.BlockSpec((None, f, d), lambda j, te, nu: (te[j], 0, 0))],
            out_specs=pl.BlockSpec((ts, d), lambda j, te, nu: (j, 0))),
        out_shape=jax.ShapeDtypeStruct((rows, d), BF16),
        compiler_params=_params("arbitrary"),
        name="moe_experts",
    )(plan["tile_expert"], plan["n_used"], xs, w_gate, w_up, w_down)


def _moe_combine_kernel(ls_ref, nch_ref, ds_ref, wblk_ref, lsv_ref, x1_ref, mod_ref, fw_ref, ys_hbm,
                        o_ref, ys_scr, sem):
    i = pl.program_id(0)
    tms = x1_ref.shape[0]
    loc = ys_scr.shape[0]

    @pl.when(i == 0)
    def _():
        ys_scr[...] = jnp.zeros_like(ys_scr)

    def copy(src_row, dst_row):
        return pltpu.make_async_copy(ys_hbm.at[pl.ds(pl.multiple_of(src_row, ROW_CHUNK), ROW_CHUNK)],
                                     ys_scr.at[pl.ds(pl.multiple_of(dst_row, ROW_CHUNK), ROW_CHUNK)], sem)

    n_started = jnp.int32(0)
    for e in range(N_EXPERTS):
        n = nch_ref[i * N_EXPERTS + e]
        l0 = ls_ref[i * N_EXPERTS + e]
        d0 = ds_ref[i * N_EXPERTS + e]

        def run_body(j, carry, l0=l0, d0=d0):
            copy(d0 + j * ROW_CHUNK, l0 + j * ROW_CHUNK).start()
            return carry

        lax.fori_loop(0, n, run_body, 0)
        n_started = n_started + n

    slot_a, slot_b = _tile_slots(wblk_ref[...].astype(F32), lsv_ref[...])
    scol = lax.broadcasted_iota(jnp.int32, (tms, loc), 1).astype(F32)
    perm = jnp.where((scol == slot_a) | (scol == slot_b), 1.0, 0.0).astype(BF16)

    def wait_body(j, carry):
        copy(0, 0).wait()
        return carry

    lax.fori_loop(0, n_started, wait_body, 0)
    y = jnp.dot(perm, ys_scr[...], preferred_element_type=F32)
    x2 = x1_ref[...] + mod_ref[5:6, :] * y
    o_ref[...] = x2 * lax.rsqrt(jnp.mean(x2 * x2, axis=-1, keepdims=True) + EPS) * fw_ref[...]


def _moe_combine(hx, ys, x1, mod, fw, plan, tms, tiles_per_seq):
    t, d = x1.shape
    nt = t // tms
    return pl.pallas_call(
        _moe_combine_kernel,
        grid_spec=pltpu.PrefetchScalarGridSpec(
            num_scalar_prefetch=3,
            grid=(nt,),
            in_specs=[pl.BlockSpec((tms, LANES), lambda i, *_: (i, d // LANES)),
                      pl.BlockSpec((None, 1, LANES), lambda i, *_: (i, 0, 0)),
                      pl.BlockSpec((tms, d), lambda i, *_: (i, 0)),
                      pl.BlockSpec((None, 6, d), lambda i, *_: (i // tiles_per_seq, 0, 0)),
                      pl.BlockSpec((1, d), lambda i, *_: (0, 0)),
                      pl.BlockSpec(memory_space=pl.ANY)],
            out_specs=pl.BlockSpec((tms, d), lambda i, *_: (i, 0)),
            scratch_shapes=[pltpu.VMEM((_local_rows(tms), d), BF16),
                            pltpu.SemaphoreType.DMA(())]),
        out_shape=jax.ShapeDtypeStruct((t, d), F32),
        compiler_params=_params("arbitrary"),
        name="moe_combine",
    )(plan["loc_start"], plan["n_chunks"], plan["dst_start"], hx, plan["loc_start_vec"], x1, mod, fw, ys)


def _pad_cols(w, n):
    return jnp.pad(w, ((0, 0), (0, n - w.shape[1])))


def kernel(x, c, ada_w, ada_b, mix_norm_w, w_in, rwkv_mu, rwkv_w0, rwkv_w_up, rwkv_a0, rwkv_a_up, rwkv_g_up, rwkv_k_k, rwkv_k_a, rwkv_r_k, rwkv_gn_w, rwkv_gn_b, mlstm_conv_w, mlstm_conv_b, mlstm_i_b, mlstm_f_b, mlstm_hn_w, w_out, ffn_norm_w, moe_w_group, moe_b_group, moe_w_router, moe_b_router, moe_w_gate, moe_w_up, moe_w_down, final_norm_w):
    bsz, seq, d = x.shape
    assert ada_w.shape[0] == 1, "the fused final norm assumes a single layer"
    assert seq % CHUNK == 0
    l = 0
    tm = min(512, seq)
    tms = min(SORT_TILE, seq)
    mod = _ada(c, ada_w[l], ada_b[l]).reshape(bsz, 6, d)
    w_r = w_in[l][:, :RWKV_COLS].astype(BF16)
    w_m = _pad_cols(w_in[l][:, RWKV_COLS:], MLSTM_PAD_COLS).astype(BF16)
    pr, pm = _in_proj(x, mod, mix_norm_w[l].reshape(1, d), w_r, w_m, tm)
    y_r = _rwkv(pr, rwkv_mu[l], rwkv_w0[l], rwkv_w_up[l], rwkv_a0[l], rwkv_a_up[l], rwkv_g_up[l],
                rwkv_k_k[l], rwkv_k_a[l], rwkv_r_k[l], rwkv_gn_w[l], rwkv_gn_b[l], min(RWKV_STEP_ROWS, seq))
    y_m = _mlstm(pm, mlstm_conv_w[l], mlstm_conv_b[l], mlstm_i_b[l], mlstm_f_b[l], mlstm_hn_w[l],
                 min(MLSTM_STEP_ROWS, seq))
    wo = w_out[l].astype(BF16)
    w_route = _pad_cols(jnp.concatenate([moe_w_router[l], moe_w_group[l]], axis=1), LANES)
    b_route = _pad_cols(jnp.concatenate([moe_b_router[l], moe_b_group[l]]).reshape(1, -1), LANES)
    x1, hx = _mix_out(
        x, y_r, y_m, mod, ffn_norm_w[l].reshape(1, d), wo[:RWKV_WIDTH], wo[RWKV_WIDTH:], w_route, b_route, tm)
    hx = hx.reshape(bsz * seq, d + LANES)
    flags = hx[:, d + 3 * N_EXPERTS:] != 0
    plan = _moe_plan(flags, tms, EXPERT_TILE)
    xs = _moe_sort(hx, plan, tms, EXPERT_TILE)
    ys = _moe_experts(xs, plan, moe_w_gate[l].astype(BF16), moe_w_up[l].astype(BF16),
                      moe_w_down[l].astype(BF16), EXPERT_TILE)
    out = _moe_combine(hx, ys, x1.reshape(bsz * seq, d), mod, final_norm_w.reshape(1, d), plan, tms, seq // tms)
    return out.reshape(bsz, seq, d)
```

```python
import jax
import jax.numpy as jnp
from jax import lax
from jax.experimental import pallas as pl
from jax.experimental.pallas import tpu as pltpu

F32 = jnp.float32
BF16 = jnp.bfloat16
HIGHEST = lax.Precision.HIGHEST

CHUNK = 64
RWKV_CHUNK = 128
NEUMANN_BLOCK = 16
RWKV_STEP_ROWS = 512
RWKV_SKEW = 4
MLSTM_STEP_ROWS = 256
EPS = 1e-6
RWKV_HEAD = 64
RWKV_HEADS = 8
RWKV_WIDTH = RWKV_HEADS * RWKV_HEAD
DECAY_LORA = 64
ICLR_LORA = 64
GATE_LORA = 128
GN_EPS = 64e-5
MLSTM_HEADS = 4
MLSTM_DK = 64
MLSTM_DV = 128
MLSTM_QK = MLSTM_HEADS * MLSTM_DK
MLSTM_WIDTH = MLSTM_HEADS * MLSTM_DV
CONV_WIDTH = 4
GATE_SOFTCAP = 15.0
RWKV_COLS = 3 * RWKV_WIDTH + DECAY_LORA + ICLR_LORA + GATE_LORA
N_GROUPS = 4
EXPERTS_PER_GROUP = 8
N_EXPERTS = N_GROUPS * EXPERTS_PER_GROUP
LANES = 128
MLSTM_MAIN = 2 * MLSTM_QK + 2 * MLSTM_WIDTH
MLSTM_PAD_COLS = MLSTM_MAIN + LANES
VMEM_LIMIT = 56 * 1024 * 1024


def _mm(a, b):
    return jnp.dot(a.astype(BF16), b.astype(BF16), preferred_element_type=F32)


def _nt(a, b):
    return lax.dot_general(a.astype(BF16), b.astype(BF16), (((1,), (1,)), ((), ())),
                           preferred_element_type=F32)


def _tn(a, b):
    return lax.dot_general(a.astype(BF16), b.astype(BF16), (((0,), (0,)), ((), ())),
                           preferred_element_type=F32)


def _mm_f32(a, b):
    return jnp.dot(a, b, preferred_element_type=F32, precision=HIGHEST)


def _cumsum_rows(tril, x):
    hi = x.astype(BF16)
    r1 = x - hi.astype(F32)
    mid = r1.astype(BF16)
    lo = (r1 - mid.astype(F32)).astype(BF16)
    n = x.shape[1]
    out = jnp.dot(tril, jnp.concatenate([hi, mid, lo], axis=1), preferred_element_type=F32)
    return out[:, :n] + out[:, n:2 * n] + out[:, 2 * n:]


def _sigmoid(z):
    return 1.0 / (1.0 + jnp.exp(-z))


def _softplus(z):
    return jnp.maximum(z, 0.0) + jnp.log1p(jnp.exp(-jnp.abs(z)))


def _log2(n):
    assert n & (n - 1) == 0
    return n.bit_length() - 1


def _params(*sem):
    return pltpu.CompilerParams(dimension_semantics=sem, vmem_limit_bytes=VMEM_LIMIT)


def _ada_kernel(c_ref, w_ref, b_ref, o_ref):
    c = c_ref[...]
    o_ref[...] = _mm_f32(c * _sigmoid(c), w_ref[...]) + b_ref[...]


def _ada(c, w, b):
    bsz, d = c.shape
    n = w.shape[1]
    return pl.pallas_call(
        _ada_kernel,
        grid=(n // d,),
        in_specs=[pl.BlockSpec((bsz, d), lambda j: (0, 0)),
                  pl.BlockSpec((d, d), lambda j: (0, j)),
                  pl.BlockSpec((1, d), lambda j: (0, j))],
        out_specs=pl.BlockSpec((bsz, d), lambda j: (0, j)),
        out_shape=jax.ShapeDtypeStruct((bsz, n), F32),
        compiler_params=_params("arbitrary"),
        name="ada",
    )(c, w, b.reshape(1, n))


def _norm_mod(x, nw, shift, scale):
    y = x * lax.rsqrt(jnp.mean(x * x, axis=-1, keepdims=True) + EPS) * nw
    return y * (1.0 + scale) + shift


def _in_proj_kernel(x_ref, mod_ref, nw_ref, wr_ref, wm_ref, pr_ref, pm_ref):
    h = _norm_mod(x_ref[...], nw_ref[...], mod_ref[0:1, :], mod_ref[1:2, :]).astype(BF16)
    pr_ref[...] = jnp.dot(h, wr_ref[...], preferred_element_type=F32)
    pm_ref[...] = jnp.dot(h, wm_ref[...], preferred_element_type=F32)


def _in_proj(x, mod, nw, w_r, w_m, tm):
    bsz, seq, d = x.shape
    nr, nm = w_r.shape[1], w_m.shape[1]
    return pl.pallas_call(
        _in_proj_kernel,
        grid=(bsz, seq // tm),
        in_specs=[pl.BlockSpec((None, tm, d), lambda b, i: (b, i, 0)),
                  pl.BlockSpec((None, 6, d), lambda b, i: (b, 0, 0)),
                  pl.BlockSpec((1, d), lambda b, i: (0, 0)),
                  pl.BlockSpec((d, nr), lambda b, i: (0, 0)),
                  pl.BlockSpec((d, nm), lambda b, i: (0, 0))],
        out_specs=[pl.BlockSpec((None, tm, nr), lambda b, i: (b, i, 0)),
                   pl.BlockSpec((None, tm, nm), lambda b, i: (b, i, 0))],
        out_shape=[jax.ShapeDtypeStruct((bsz, seq, nr), F32),
                   jax.ShapeDtypeStruct((bsz, seq, nm), F32)],
        compiler_params=_params("arbitrary", "arbitrary"),
        name="in_proj",
    )(x, mod, nw, w_r, w_m)


def _rwkv_kernel(pr_ref, mu_ref, w0_ref, wup_ref, a0_ref, aup_ref, gup_ref, kk_ref, ka_ref,
                 rk_ref, gnw_ref, gnb_ref, hs_ref, o_ref, s_scr, prev_scr):
    C = RWKV_CHUNK
    N = RWKV_HEAD
    H = RWKV_HEADS
    rows = pr_ref.shape[0]
    nc = rows // C

    @pl.when(pl.program_id(1) == 0)
    def _():
        s_scr[...] = jnp.zeros_like(s_scr)
        prev_scr[...] = jnp.zeros_like(prev_scr)

    x = pr_ref[...]
    row1 = lax.broadcasted_iota(jnp.int32, x.shape, 0)
    prev = jnp.where(row1 == 0, prev_scr[...], pltpu.roll(x, 1, axis=0))
    prev_scr[...] = x[rows - 1:rows, :]
    u_all = x + (prev - x) * mu_ref[...]

    row = lax.broadcasted_iota(jnp.int32, (C, C), 0)
    col = lax.broadcasted_iota(jnp.int32, (C, C), 1)
    same = lambda blk: (row >> _log2(blk)) == (col >> _log2(blk))
    eye = (row == col).astype(F32)
    tril = (col <= row).astype(BF16)
    row2 = lax.broadcasted_iota(jnp.int32, (C, 2 * C), 0)
    col2 = lax.broadcasted_iota(jnp.int32, (C, 2 * C), 1) & (C - 1)
    strict2 = col2 < row2
    incl2 = col2 <= row2
    each = lambda f, *ls: [f(*t) for t in zip(*ls)]
    state = [s_scr[h] for h in range(H)]
    o1, o2, o3 = RWKV_WIDTH, 2 * RWKV_WIDTH, 3 * RWKV_WIDTH
    o4 = o3 + DECAY_LORA
    o5 = o4 + ICLR_LORA

    def solve(j):
        u = u_all[j * C:(j + 1) * C, :]
        r, k, v = u[:, :o1], u[:, o1:o2], u[:, o2:o3]
        wd, ad, gd = u[:, o3:o4], u[:, o4:o5], u[:, o5:]
        w_log = -_softplus(-(w0_ref[...] + _mm(jnp.tanh(wd), wup_ref[...]))) - 0.5
        lw = -jnp.exp(w_log)
        a = _sigmoid(a0_ref[...] + _mm(ad, aup_ref[...]))
        g = _mm(_sigmoid(gd), gup_ref[...])
        yield
        kk = k * kk_ref[...]
        k2 = k * (1.0 + (a - 1.0) * ka_ref[...])
        sq = kk * kk
        sq_hi = sq.astype(BF16)
        sq_lo = (sq - sq_hi.astype(F32)).astype(BF16)
        sums = (jnp.dot(sq_hi, hs_ref[...], preferred_element_type=F32)
                + jnp.dot(sq_lo, hs_ref[...], preferred_element_type=F32))
        kk = kk / jnp.maximum(jnp.sqrt(sums), 1e-12)
        rk = r * k2 * rk_ref[...]
        cum = _cumsum_rows(tril, lw)
        yield
        cl = cum[C - 1:C, :]
        p_inv = jnp.exp(-cum)
        p_rest = jnp.exp(cl - cum)
        b = kk * a
        ah_w = -kk * jnp.exp(cum - lw)
        rh_w = r * jnp.exp(cum)
        bt_w, kt_w = b * p_inv, k2 * p_inv
        b2_w, k2p_w = b * p_rest, k2 * p_rest
        p_end_w = jnp.exp(cl)
        cut = lambda t: [t[:, h * N:(h + 1) * N] for h in range(H)]
        ah, rh, bt, kt, b2, k2p, v_h, p_end, rk_h = (
            cut(t) for t in (ah_w, rh_w, bt_w, kt_w, b2_w, k2p_w, v, p_end_w, rk))
        yield
        ar = each(lambda a_, r_: jnp.concatenate([a_, r_], axis=0), ah, rh)
        bk = each(lambda b_, k_: jnp.concatenate([b_, k_], axis=0), bt, kt)
        x1 = each(_nt, ar, bk)
        l_both = each(lambda t: jnp.where(strict2, t[:C, :], 0.0), x1)
        m_both = each(lambda t: jnp.where(incl2, t[C:, :], 0.0), x1)
        l_ab = each(lambda t: t[:, :C], l_both)
        yield
        blk = NEUMANN_BLOCK
        n1 = each(lambda t: jnp.where(same(blk), t, 0.0), l_ab)
        xi = each(lambda t: eye + t, n1)
        nk = n1
        power = 2
        while power < blk:
            nk = each(lambda t: _mm(t, t), nk)
            xi = each(lambda x_, t: x_ + _mm(x_, t), xi, nk)
            power *= 2
            yield
        while blk < C:
            pair = same(2 * blk) & jnp.logical_not(same(blk))
            xe = each(lambda x_, t: _mm(x_, jnp.where(pair, t, 0.0)), xi, l_ab)
            xi = each(lambda x_, t: x_ + _mm(t, x_), xi, xe)
            blk *= 2
            yield
        zero = jnp.zeros((C, N), F32)
        z = each(lambda l_, v_: _mm(l_, jnp.concatenate([zero, v_], axis=0)), l_both, v_h)
        a2 = each(_mm, xi, ah)
        u0 = each(_mm, xi, z)
        yield
        uv = each(lambda u_, v_: jnp.concatenate([u_, v_], axis=0), u0, v_h)
        r2 = each(lambda r_, m_, a_: r_ + _mm(m_[:, :C], a_), rh, m_both, a2)
        y0 = each(_mm, m_both, uv)
        gt = each(_tn, a2, b2)
        ht = each(lambda uv_, b_, k_: _tn(uv_, jnp.concatenate([b_, k_], axis=0)), uv, b2, k2p)
        yield
        y = []
        for h in range(H):
            y.append(_nt(r2[h], state[h]) + y0[h])
            state[h] = state[h] * p_end[h] + _mm(state[h], gt[h]) + ht[h]
        yield
        yc = each(lambda t: t - jnp.mean(t, axis=-1, keepdims=True), y)
        var = each(lambda t: jnp.mean(t * t, axis=-1, keepdims=True), yc)
        rk_sum = each(lambda t: jnp.sum(t, axis=-1, keepdims=True), rk_h)
        yield
        for h in range(H):
            sl = slice(h * N, (h + 1) * N)
            yn = yc[h] * lax.rsqrt(var[h] + GN_EPS) * gnw_ref[:, sl] + gnb_ref[:, sl]
            o_ref[j * C:(j + 1) * C, sl] = (yn + rk_sum[h] * v_h[h]) * g[:, sl]

    pending = [solve(j) for j in range(nc)]
    active = []
    tick = 0
    while pending or active:
        if pending and tick % RWKV_SKEW == 0:
            active.append(pending.pop(0))
        for gen in list(active):
            try:
                next(gen)
            except StopIteration:
                active.remove(gen)
        tick += 1
    for h in range(H):
        s_scr[h] = state[h]


def _rwkv(pr, mu, w0, w_up, a0, a_up, g_up, k_k, k_a, r_k, gn_w, gn_b, rows):
    bsz, seq, cols = pr.shape
    row = lambda p: p.reshape(1, -1)
    vec = lambda n: pl.BlockSpec((1, n), lambda b, c: (0, 0))
    mat = lambda m: pl.BlockSpec(m.shape, lambda b, c: (0, 0))
    lane_head = jnp.arange(RWKV_WIDTH) // RWKV_HEAD
    same_head = (lane_head[:, None] == lane_head[None, :]).astype(BF16)
    return pl.pallas_call(
        _rwkv_kernel,
        grid=(bsz, seq // rows),
        in_specs=[pl.BlockSpec((None, rows, cols), lambda b, c: (b, c, 0)),
                  vec(cols), vec(RWKV_WIDTH), mat(w_up), vec(RWKV_WIDTH), mat(a_up), mat(g_up),
                  vec(RWKV_WIDTH), vec(RWKV_WIDTH), vec(RWKV_WIDTH), vec(RWKV_WIDTH), vec(RWKV_WIDTH),
                  mat(same_head)],
        out_specs=pl.BlockSpec((None, rows, RWKV_WIDTH), lambda b, c: (b, c, 0)),
        out_shape=jax.ShapeDtypeStruct((bsz, seq, RWKV_WIDTH), F32),
        scratch_shapes=[pltpu.VMEM((RWKV_HEADS, RWKV_HEAD, RWKV_HEAD), F32),
                        pltpu.VMEM((1, cols), F32)],
        compiler_params=_params("arbitrary", "arbitrary"),
        name="rwkv",
    )(pr, row(mu), row(w0), w_up, row(a0), a_up, g_up, row(k_k), row(k_a), row(r_k), row(gn_w), row(gn_b),
      same_head)


def _mlstm_kernel(pm_ref, cw_ref, cb_ref, gb_ref, hn_ref, o_ref, ext_scr, c_scr, n_scr, m_scr):
    C = CHUNK
    DK, DV = MLSTM_DK, MLSTM_DV
    QK2 = 2 * MLSTM_QK
    rows = pm_ref.shape[0]
    nc = rows // C

    @pl.when(pl.program_id(1) == 0)
    def _():
        ext_scr[0:8, :] = jnp.zeros((8, QK2), F32)
        c_scr[...] = jnp.zeros_like(c_scr)
        n_scr[...] = jnp.zeros_like(n_scr)
        m_scr[...] = jnp.zeros_like(m_scr)

    ext_scr[8:8 + rows, :] = pm_ref[:, 0:QK2]
    conv = cb_ref[...] + cw_ref[0:1, :] * ext_scr[5:5 + rows, :]
    for i in range(1, CONV_WIDTH):
        conv = conv + cw_ref[i:i + 1, :] * ext_scr[5 + i:5 + i + rows, :]
    ext_scr[0:8, :] = ext_scr[rows:rows + 8, :]
    qk = conv * _sigmoid(conv)
    q = qk[:, :MLSTM_QK] * (DK ** -0.5)
    k = qk[:, MLSTM_QK:]
    v = pm_ref[:, QK2:QK2 + MLSTM_WIDTH]
    og = pm_ref[:, QK2 + MLSTM_WIDTH:MLSTM_MAIN]

    z = GATE_SOFTCAP * jnp.tanh((pm_ref[:, MLSTM_MAIN:] + gb_ref[...]) / GATE_SOFTCAP)
    log_f = jnp.minimum(z, 0.0) - jnp.log1p(jnp.exp(-jnp.abs(z)))
    rr = lax.broadcasted_iota(jnp.int32, (rows, rows), 0)
    cc = lax.broadcasted_iota(jnp.int32, (rows, rows), 1)
    in_chunk = (cc <= rr) & ((rr >> 6) == (cc >> 6))
    lane = lax.broadcasted_iota(jnp.int32, (rows, LANES), 1)
    gates = jnp.where(lane < MLSTM_HEADS, z, _mm_f32(in_chunk.astype(F32), log_f))
    ri = lax.broadcasted_iota(jnp.int32, (LANES, LANES), 0)
    ci = lax.broadcasted_iota(jnp.int32, (LANES, LANES), 1)
    gates_t = lax.dot_general((ri == ci).astype(F32), gates, (((1,), (1,)), ((), ())),
                              preferred_element_type=F32, precision=HIGHEST)
    row = lax.broadcasted_iota(jnp.int32, (C, C), 0)
    col = lax.broadcasted_iota(jnp.int32, (C, C), 1)
    incl = col <= row

    units = [(j, h) for j in range(nc) for h in range(MLSTM_HEADS)]
    each = lambda f, *ls: [f(*t) for t in zip(*ls)]
    li_col = [gates[j * C:(j + 1) * C, h:h + 1] for j, h in units]
    b_col = [gates[j * C:(j + 1) * C, MLSTM_HEADS + h:MLSTM_HEADS + h + 1] for j, h in units]
    li_row = [gates_t[h:h + 1, j * C:(j + 1) * C] for j, h in units]
    b_row = [gates_t[MLSTM_HEADS + h:MLSTM_HEADS + h + 1, j * C:(j + 1) * C] for j, h in units]
    q_h = [q[j * C:(j + 1) * C, h * DK:(h + 1) * DK] for j, h in units]
    k_h = [k[j * C:(j + 1) * C, h * DK:(h + 1) * DK] for j, h in units]
    v_h = [v[j * C:(j + 1) * C, h * DV:(h + 1) * DV] for j, h in units]
    b_last = each(lambda bc: bc[C - 1:C, :], b_col)
    end_log = each(lambda bl, bc, lc: bl - bc + lc, b_last, b_col, li_col)
    end_max = each(lambda el: jnp.max(el, axis=0, keepdims=True), end_log)
    m_run = [m_scr[h][0:1, 0:1] for h in range(MLSTM_HEADS)]
    m, m_new = [], []
    for i, (j, h) in enumerate(units):
        m.append(m_run[h])
        m_run[h] = jnp.maximum(b_last[i] + m_run[h], end_max[i])
        m_new.append(m_run[h])
    qk_h = each(_nt, q_h, k_h)
    d_mat = each(lambda bc, br, lr: jnp.where(incl, bc - br + lr, -jnp.inf), b_col, b_row, li_row)
    inter = each(lambda bc, m_: bc + m_, b_col, m)
    m_t = each(lambda i_, d_: jnp.maximum(i_, jnp.max(d_, axis=-1, keepdims=True)), inter, d_mat)
    scores = each(lambda s_, d_, mt: s_ * jnp.exp(d_ - mt), qk_h, d_mat, m_t)
    w_inter = each(lambda i_, mt: jnp.exp(i_ - mt), inter, m_t)
    sv_h = each(_mm, scores, v_h)
    carry_w = each(lambda bl, m_, mn: jnp.exp(bl + m_ - mn), b_last, m, m_new)
    kw = each(lambda k_, el, mn: k_ * jnp.exp(el - mn), k_h, end_log, m_new)
    kv_h = each(_tn, kw, v_h)
    c_run = [c_scr[h] for h in range(MLSTM_HEADS)]
    n_run = [n_scr[h][0:1, 0:DK] for h in range(MLSTM_HEADS)]
    c_mat, n_vec = [], []
    for i, (j, h) in enumerate(units):
        c_mat.append(c_run[h])
        n_vec.append(n_run[h])
        c_run[h] = carry_w[i] * c_run[h] + kv_h[i]
        n_run[h] = carry_w[i] * n_run[h] + jnp.sum(kw[i], axis=0, keepdims=True)
    for h in range(MLSTM_HEADS):
        c_scr[h] = c_run[h]
        n_scr[h] = jnp.broadcast_to(jnp.concatenate([n_run[h], jnp.zeros((1, LANES - DK), F32)], axis=1), (8, LANES))
        m_scr[h] = jnp.broadcast_to(m_run[h], (8, LANES))
    qc_h = each(_mm, q_h, c_mat)
    num = each(lambda s_, w_, c_: s_ + w_ * c_, sv_h, w_inter, qc_h)
    s_sum = each(lambda s_: jnp.sum(s_, axis=-1, keepdims=True), scores)
    qn = each(lambda q_, n_: jnp.sum(q_ * n_, axis=-1, keepdims=True), q_h, n_vec)
    den = each(lambda s_, w_, q_: s_ + w_ * q_, s_sum, w_inter, qn)
    hh = each(lambda n_, d_, mt: n_ / jnp.maximum(jnp.abs(d_), jnp.exp(-mt)), num, den, m_t)
    ms = each(lambda t: jnp.mean(t * t, axis=-1, keepdims=True), hh)
    for i, (j, h) in enumerate(units):
        hn = hh[i] * lax.rsqrt(ms[i] + EPS) * hn_ref[:, h * DV:(h + 1) * DV]
        o_ref[j * C:(j + 1) * C, h * DV:(h + 1) * DV] = hn * _sigmoid(og[j * C:(j + 1) * C, h * DV:(h + 1) * DV])


def _mlstm(pm, conv_w, conv_b, i_b, f_b, hn_w, rows):
    bsz, seq, cols = pm.shape
    gate_b = jnp.zeros((1, LANES), F32).at[0, :MLSTM_HEADS].set(i_b).at[0, MLSTM_HEADS:2 * MLSTM_HEADS].set(f_b)
    full = lambda a: pl.BlockSpec(a.shape, lambda b, c: (0, 0))
    conv_b = conv_b.reshape(1, -1)
    hn_w = hn_w.reshape(1, -1)
    return pl.pallas_call(
        _mlstm_kernel,
        grid=(bsz, seq // rows),
        in_specs=[pl.BlockSpec((None, rows, cols), lambda b, c: (b, c, 0)),
                  full(conv_w), full(conv_b), full(gate_b), full(hn_w)],
        out_specs=pl.BlockSpec((None, rows, MLSTM_WIDTH), lambda b, c: (b, c, 0)),
        out_shape=jax.ShapeDtypeStruct((bsz, seq, MLSTM_WIDTH), F32),
        scratch_shapes=[pltpu.VMEM((rows + 8, 2 * MLSTM_QK), F32),
                        pltpu.VMEM((MLSTM_HEADS, MLSTM_DK, MLSTM_DV), F32),
                        pltpu.VMEM((MLSTM_HEADS, 8, LANES), F32),
                        pltpu.VMEM((MLSTM_HEADS, 8, LANES), F32)],
        compiler_params=_params("arbitrary", "arbitrary"),
        name="mlstm",
    )(pm, conv_w, conv_b, gate_b, hn_w)


def _mix_out_kernel(x_ref, yr_ref, ym_ref, mod_ref, nw_ref, wor_ref, wom_ref, wr_ref, br_ref, x1_ref, hx_ref):
    d = x_ref.shape[-1]
    y = (jnp.dot(yr_ref[...].astype(BF16), wor_ref[...], preferred_element_type=F32)
         + jnp.dot(ym_ref[...].astype(BF16), wom_ref[...], preferred_element_type=F32))
    x1 = x_ref[...] + mod_ref[2:3, :] * y
    x1_ref[...] = x1
    h = _norm_mod(x1, nw_ref[...], mod_ref[3:4, :], mod_ref[4:5, :])
    h_hi = h.astype(BF16)
    hx_ref[:, :d] = h_hi

    h_lo = (h - h_hi.astype(F32)).astype(BF16)
    w = wr_ref[...]
    w_hi = w.astype(BF16)
    w_lo = (w - w_hi.astype(F32)).astype(BF16)
    hw = jnp.dot(h_hi, jnp.concatenate([w_hi, w_lo], axis=1), preferred_element_type=F32)
    logits = hw[:, :LANES] + hw[:, LANES:] + jnp.dot(h_lo, w_hi, preferred_element_type=F32) + br_ref[...]
    lane = lax.broadcasted_iota(jnp.int32, logits.shape, 1)
    neg = -jnp.inf
    el = logits
    gl = jnp.where((lane >= N_EXPERTS) & (lane < N_EXPERTS + N_GROUPS), logits, neg)
    g_max = jnp.max(gl, axis=-1, keepdims=True)
    g_idx = jnp.min(jnp.where(gl == g_max, lane, LANES), axis=-1, keepdims=True) - N_EXPERTS
    g_w = 1.0 / jnp.sum(jnp.exp(gl - g_max), axis=-1, keepdims=True)
    in_group = ((lane >> 3) == g_idx) & (lane < N_EXPERTS)
    sel = jnp.where(in_group, el, neg)
    v1 = jnp.max(sel, axis=-1, keepdims=True)
    i1 = jnp.min(jnp.where(sel == v1, lane, LANES), axis=-1, keepdims=True)
    rest = jnp.where(lane == i1, neg, sel)
    v2 = jnp.max(rest, axis=-1, keepdims=True)
    i2 = jnp.min(jnp.where(rest == v2, lane, LANES), axis=-1, keepdims=True)
    e2 = jnp.exp(v2 - v1)
    w1 = 1.0 / (1.0 + e2)
    w2 = e2 / (1.0 + e2)
    cw = jnp.where(lane == i1, w1 * g_w, jnp.where(lane == i2, w2 * g_w, 0.0))
    hi = cw.astype(BF16).astype(F32)
    mid = (cw - hi).astype(BF16).astype(F32)
    lo = (cw - hi - mid).astype(BF16).astype(F32)
    flag = jnp.where(hi != 0.0, 1.0, 0.0)
    pack = (hi + pltpu.roll(mid, N_EXPERTS, axis=1) + pltpu.roll(lo, 2 * N_EXPERTS, axis=1)
            + pltpu.roll(flag, 3 * N_EXPERTS, axis=1))
    hx_ref[:, d:] = pack.astype(BF16)


def _mix_out(x, y_r, y_m, mod, nw, wo_r, wo_m, w_route, b_route, tm):
    bsz, seq, d = x.shape
    tok = lambda n: pl.BlockSpec((None, tm, n), lambda b, i: (b, i, 0))
    full = lambda a: pl.BlockSpec(a.shape, lambda b, i: (0, 0))
    return pl.pallas_call(
        _mix_out_kernel,
        grid=(bsz, seq // tm),
        in_specs=[tok(d), tok(RWKV_WIDTH), tok(MLSTM_WIDTH),
                  pl.BlockSpec((None, 6, d), lambda b, i: (b, 0, 0)),
                  full(nw), full(wo_r), full(wo_m), full(w_route), full(b_route)],
        out_specs=[tok(d), tok(d + LANES)],
        out_shape=[jax.ShapeDtypeStruct((bsz, seq, d), F32),
                   jax.ShapeDtypeStruct((bsz, seq, d + LANES), BF16)],
        compiler_params=_params("arbitrary", "arbitrary"),
        name="mix_out",
    )(x, y_r, y_m, mod, nw, wo_r, wo_m, w_route, b_route)


ROW_CHUNK = 16
SORT_TILE = 512
EXPERT_TILE = 512
NO_SLOT = 1e9


def _moe_plan(flags, tms, ts):
    t, ne = flags.shape
    nt = t // tms
    i32 = jnp.int32
    tri = lambda n: (jnp.arange(n)[:, None] < jnp.arange(n)[None, :]).astype(F32)
    before = lambda a, b: jnp.dot(a, b, precision=HIGHEST).astype(i32)
    cnt = flags.reshape(nt, tms, ne).astype(i32).sum(axis=1)
    pc = (cnt + ROW_CHUNK - 1) // ROW_CHUNK * ROW_CHUNK
    pcf = pc.astype(F32)
    loc_start = before(pcf, tri(ne))
    len_e = pc.sum(axis=0)
    seg_e = (len_e + ts - 1) // ts * ts
    e_start = before(seg_e.astype(F32)[None, :], tri(ne))[0]
    e_end = e_start + seg_e
    dst_start = e_start[None, :] + before(tri(nt).T, pcf)
    n_tiles_max = _moe_max_rows(t, ne, tms, ts) // ts
    tile_row = jnp.arange(n_tiles_max, dtype=i32) * ts
    tile_expert = jnp.minimum((tile_row[:, None] >= e_end[None, :]).astype(i32).sum(axis=1), ne - 1)
    return dict(
        loc_start=loc_start.reshape(-1).astype(i32), n_chunks=(pc // ROW_CHUNK).reshape(-1).astype(i32),
        dst_start=dst_start.reshape(-1).astype(i32), tail_start=(e_start + len_e).astype(i32),
        tail_chunks=((seg_e - len_e) // ROW_CHUNK).astype(i32),
        n_used=(e_end[-1:] // ts).astype(i32), tile_expert=tile_expert,
        loc_start_vec=jnp.pad(loc_start.astype(F32), ((0, 0), (0, LANES - ne))).reshape(nt, 1, LANES))


def _moe_max_rows(t, ne, tms, ts):
    rows = 2 * t + (t // tms) * ne * (ROW_CHUNK - 1) + ne * (ts - 1)
    return (rows + ts - 1) // ts * ts


def _local_rows(tms):
    return 2 * tms + N_EXPERTS * ROW_CHUNK


def _tile_slots(blk, loc_start_vec):
    tms = blk.shape[0]
    lane = lax.broadcasted_iota(jnp.int32, blk.shape, 1)
    m = jnp.where(lane < N_EXPERTS, pltpu.roll(blk, N_EXPERTS, axis=1), 0.0)
    row = lax.broadcasted_iota(jnp.int32, (tms, tms), 0)
    col = lax.broadcasted_iota(jnp.int32, (tms, tms), 1)
    before = jnp.dot((col < row).astype(BF16), m.astype(BF16), preferred_element_type=F32)
    slot = loc_start_vec + before
    slot_a = jnp.min(jnp.where(m > 0.0, slot, NO_SLOT), axis=-1, keepdims=True)
    slot_b = jnp.max(jnp.where(m > 0.0, slot, -NO_SLOT), axis=-1, keepdims=True)
    return slot_a, slot_b


def _moe_sort_kernel(ls_ref, nch_ref, ds_ref, tls_ref, tlc_ref, nu_ref, hx_ref, lsv_ref, xo_ref, xs_scr, z_scr,
                     sem, zsem):
    i = pl.program_id(0)
    tms, width = hx_ref.shape
    loc = xs_scr.shape[0]
    d = width - LANES
    slot_a, slot_b = _tile_slots(hx_ref[:, d:].astype(F32), lsv_ref[...])
    lane = lax.broadcasted_iota(jnp.int32, (tms, LANES), 1)
    packed = jnp.where(lane == 0, slot_a, jnp.where(lane == 1, slot_b, 0.0))
    ri = lax.broadcasted_iota(jnp.int32, (LANES, LANES), 0)
    ci = lax.broadcasted_iota(jnp.int32, (LANES, LANES), 1)
    slots_t = lax.dot_general((ri == ci).astype(F32), packed, (((1,), (1,)), ((), ())),
                              preferred_element_type=F32, precision=HIGHEST)
    srow = lax.broadcasted_iota(jnp.int32, (loc, tms), 0).astype(F32)
    perm = jnp.where((srow == slots_t[0:1, :]) | (srow == slots_t[1:2, :]), 1.0, 0.0).astype(BF16)
    xs_scr[...] = jnp.dot(perm, hx_ref[...], preferred_element_type=F32).astype(BF16)

    def copy(src, src_row, dst_row):
        return pltpu.make_async_copy(src.at[pl.ds(pl.multiple_of(src_row, ROW_CHUNK), ROW_CHUNK)],
                                     xo_ref.at[pl.ds(pl.multiple_of(dst_row, ROW_CHUNK), ROW_CHUNK)], sem)

    n_started = jnp.int32(0)
    for e in range(N_EXPERTS):
        n = nch_ref[i * N_EXPERTS + e]
        l0 = ls_ref[i * N_EXPERTS + e]
        d0 = ds_ref[i * N_EXPERTS + e]

        def run_body(j, carry, l0=l0, d0=d0):
            copy(xs_scr, l0 + j * ROW_CHUNK, d0 + j * ROW_CHUNK).start()
            return carry

        lax.fori_loop(0, n, run_body, 0)
        n_started = n_started + n

    z_scr[...] = jnp.zeros_like(z_scr)
    ts = z_scr.shape[0]
    n_tail = jnp.int32(0)
    for e in range(N_EXPERTS):
        n = jnp.where(i == 0, tlc_ref[e], 0)
        t0 = tls_ref[e]

        def tail_body(j, carry, t0=t0):
            copy(z_scr, 0, t0 + j * ROW_CHUNK).start()
            return carry

        lax.fori_loop(0, n, tail_body, 0)
        n_tail = n_tail + n

    def tile_copy(tile):
        return pltpu.make_async_copy(z_scr, xo_ref.at[pl.ds(pl.multiple_of(tile * ts, ts), ts)], zsem)

    n_spare = jnp.where(i == 0, xo_ref.shape[0] // ts - nu_ref[0], 0)

    def spare_body(j, carry):
        tile_copy(nu_ref[0] + j).start()
        return carry

    lax.fori_loop(0, n_spare, spare_body, 0)

    def wait_body(j, carry):
        copy(xs_scr, 0, 0).wait()
        return carry

    lax.fori_loop(0, n_started + n_tail, wait_body, 0)

    def spare_wait_body(j, carry):
        tile_copy(0).wait()
        return carry

    lax.fori_loop(0, n_spare, spare_wait_body, 0)


def _moe_sort(hx, plan, tms, ts):
    t, width = hx.shape
    nt = t // tms
    rows = _moe_max_rows(t, N_EXPERTS, tms, ts)
    return pl.pallas_call(
        _moe_sort_kernel,
        grid_spec=pltpu.PrefetchScalarGridSpec(
            num_scalar_prefetch=6,
            grid=(nt,),
            in_specs=[pl.BlockSpec((tms, width), lambda i, *_: (i, 0)),
                      pl.BlockSpec((None, 1, LANES), lambda i, *_: (i, 0, 0))],
            out_specs=pl.BlockSpec(memory_space=pl.ANY),
            scratch_shapes=[pltpu.VMEM((_local_rows(tms), width), BF16),
                            pltpu.VMEM((ts, width), BF16),
                            pltpu.SemaphoreType.DMA(()),
                            pltpu.SemaphoreType.DMA(())]),
        out_shape=jax.ShapeDtypeStruct((rows, width), BF16),
        compiler_params=_params("arbitrary"),
        name="moe_sort",
    )(plan["loc_start"], plan["n_chunks"], plan["dst_start"], plan["tail_start"], plan["tail_chunks"],
      plan["n_used"], hx, plan["loc_start_vec"])


def _moe_experts_kernel(te_ref, nu_ref, xs_ref, wg_ref, wu_ref, wd_ref, o_ref):
    j = pl.program_id(0)

    @pl.when(j < nu_ref[0])
    def _():
        d = xs_ref.shape[1] - LANES
        e = te_ref[j]
        x = xs_ref[:, :d]
        wblk = xs_ref[:, d:].astype(F32)
        lane = lax.broadcasted_iota(jnp.int32, wblk.shape, 1)
        mine = (lane == e) | (lane == e + N_EXPERTS) | (lane == e + 2 * N_EXPERTS)
        cw = jnp.sum(jnp.where(mine, wblk, 0.0), axis=-1, keepdims=True)
        hg = jnp.dot(x, wg_ref[...], preferred_element_type=F32)
        hu = jnp.dot(x, wu_ref[...], preferred_element_type=F32)
        act = hg * _sigmoid(hg) * hu * cw
        o_ref[...] = jnp.dot(act.astype(BF16), wd_ref[...], preferred_element_type=F32).astype(BF16)

    @pl.when(j >= nu_ref[0])
    def _():
        o_ref[...] = jnp.zeros_like(o_ref)


def _moe_experts(xs, plan, w_gate, w_up, w_down, ts):
    rows, width = xs.shape
    ne, d, f = w_gate.shape
    tile = lambda j, te, nu: (jnp.minimum(j, nu[0] - 1), 0)
    return pl.pallas_call(
        _moe_experts_kernel,
        grid_spec=pltpu.PrefetchScalarGridSpec(
            num_scalar_prefetch=2,
            grid=(rows // ts,),
            in_specs=[pl.BlockSpec((ts, width), tile),
                      pl.BlockSpec((None, d, f), lambda j, te, nu: (te[j], 0, 0)),
                      pl.BlockSpec((None, d, f), lambda j, te, nu: (te[j], 0, 0)),
                      pl.BlockSpec((None, f, d), lambda j, te, nu: (te[j], 0, 0))],
            out_specs=pl.BlockSpec((ts, d), lambda j, te, nu: (j, 0))),
        out_shape=jax.ShapeDtypeStruct((rows, d), BF16),
        compiler_params=_params("arbitrary"),
        name="moe_experts",
    )(plan["tile_expert"], plan["n_used"], xs, w_gate, w_up, w_down)


def _moe_combine_kernel(ls_ref, nch_ref, ds_ref, wblk_ref, lsv_ref, x1_ref, mod_ref, fw_ref, ys_hbm,
                        o_ref, ys_scr, sems):
    i = pl.program_id(0)
    nt = pl.num_programs(0)
    tms = x1_ref.shape[0]
    loc = ys_scr.shape[1]

    def copy(slot, src_row, dst_row):
        return pltpu.make_async_copy(
            ys_hbm.at[pl.ds(pl.multiple_of(src_row, ROW_CHUNK), ROW_CHUNK)],
            ys_scr.at[slot, pl.ds(pl.multiple_of(dst_row, ROW_CHUNK), ROW_CHUNK)], sems.at[slot])

    def fetch(tile, slot):
        for e in range(N_EXPERTS):
            n = nch_ref[tile * N_EXPERTS + e]
            l0 = ls_ref[tile * N_EXPERTS + e]
            d0 = ds_ref[tile * N_EXPERTS + e]

            def run_body(j, carry, l0=l0, d0=d0):
                copy(slot, d0 + j * ROW_CHUNK, l0 + j * ROW_CHUNK).start()
                return carry

            lax.fori_loop(0, n, run_body, 0)

    @pl.when(i == 0)
    def _():
        ys_scr[...] = jnp.zeros_like(ys_scr)
        fetch(0, 0)

    slot = i % 2

    @pl.when(i + 1 < nt)
    def _():
        fetch(i + 1, 1 - slot)

    slot_a, slot_b = _tile_slots(wblk_ref[...].astype(F32), lsv_ref[...])
    scol = lax.broadcasted_iota(jnp.int32, (tms, loc), 1).astype(F32)
    perm = jnp.where((scol == slot_a) | (scol == slot_b), 1.0, 0.0).astype(BF16)

    n_mine = jnp.int32(0)
    for e in range(N_EXPERTS):
        n_mine = n_mine + nch_ref[i * N_EXPERTS + e]

    def wait_body(j, carry):
        copy(slot, 0, 0).wait()
        return carry

    lax.fori_loop(0, n_mine, wait_body, 0)
    y = jnp.dot(perm, ys_scr[slot], preferred_element_type=F32)
    x2 = x1_ref[...] + mod_ref[5:6, :] * y
    o_ref[...] = x2 * lax.rsqrt(jnp.mean(x2 * x2, axis=-1, keepdims=True) + EPS) * fw_ref[...]


def _moe_combine(hx, ys, x1, mod, fw, plan, tms, tiles_per_seq):
    t, d = x1.shape
    nt = t // tms
    return pl.pallas_call(
        _moe_combine_kernel,
        grid_spec=pltpu.PrefetchScalarGridSpec(
            num_scalar_prefetch=3,
            grid=(nt,),
            in_specs=[pl.BlockSpec((tms, LANES), lambda i, *_: (i, d // LANES)),
                      pl.BlockSpec((None, 1, LANES), lambda i, *_: (i, 0, 0)),
                      pl.BlockSpec((tms, d), lambda i, *_: (i, 0)),
                      pl.BlockSpec((None, 6, d), lambda i, *_: (i // tiles_per_seq, 0, 0)),
                      pl.BlockSpec((1, d), lambda i, *_: (0, 0)),
                      pl.BlockSpec(memory_space=pl.ANY)],
            out_specs=pl.BlockSpec((tms, d), lambda i, *_: (i, 0)),
            scratch_shapes=[pltpu.VMEM((2, _local_rows(tms), d), BF16),
                            pltpu.SemaphoreType.DMA((2,))]),
        out_shape=jax.ShapeDtypeStruct((t, d), F32),
        compiler_params=_params("arbitrary"),
        name="moe_combine",
    )(plan["loc_start"], plan["n_chunks"], plan["dst_start"], hx, plan["loc_start_vec"], x1, mod, fw, ys)


def _pad_cols(w, n):
    return jnp.pad(w, ((0, 0), (0, n - w.shape[1])))


def kernel(x, c, ada_w, ada_b, mix_norm_w, w_in, rwkv_mu, rwkv_w0, rwkv_w_up, rwkv_a0, rwkv_a_up, rwkv_g_up, rwkv_k_k, rwkv_k_a, rwkv_r_k, rwkv_gn_w, rwkv_gn_b, mlstm_conv_w, mlstm_conv_b, mlstm_i_b, mlstm_f_b, mlstm_hn_w, w_out, ffn_norm_w, moe_w_group, moe_b_group, moe_w_router, moe_b_router, moe_w_gate, moe_w_up, moe_w_down, final_norm_w):
    bsz, seq, d = x.shape
    assert ada_w.shape[0] == 1, "the fused final norm assumes a single layer"
    assert seq % CHUNK == 0
    l = 0
    tm = min(512, seq)
    tms = min(SORT_TILE, seq)
    mod = _ada(c, ada_w[l], ada_b[l]).reshape(bsz, 6, d)
    w_r = w_in[l][:, :RWKV_COLS].astype(BF16)
    w_m = _pad_cols(w_in[l][:, RWKV_COLS:], MLSTM_PAD_COLS).astype(BF16)
    pr, pm = _in_proj(x, mod, mix_norm_w[l].reshape(1, d), w_r, w_m, tm)
    y_r = _rwkv(pr, rwkv_mu[l], rwkv_w0[l], rwkv_w_up[l], rwkv_a0[l], rwkv_a_up[l], rwkv_g_up[l],
                rwkv_k_k[l], rwkv_k_a[l], rwkv_r_k[l], rwkv_gn_w[l], rwkv_gn_b[l], min(RWKV_STEP_ROWS, seq))
    y_m = _mlstm(pm, mlstm_conv_w[l], mlstm_conv_b[l], mlstm_i_b[l], mlstm_f_b[l], mlstm_hn_w[l],
                 min(MLSTM_STEP_ROWS, seq))
    wo = w_out[l].astype(BF16)
    w_route = _pad_cols(jnp.concatenate([moe_w_router[l], moe_w_group[l]], axis=1), LANES)
    b_route = _pad_cols(jnp.concatenate([moe_b_router[l], moe_b_group[l]]).reshape(1, -1), LANES)
    x1, hx = _mix_out(
        x, y_r, y_m, mod, ffn_norm_w[l].reshape(1, d), wo[:RWKV_WIDTH], wo[RWKV_WIDTH:], w_route, b_route, tm)
    hx = hx.reshape(bsz * seq, d + LANES)
    flags = hx[:, d + 3 * N_EXPERTS:] != 0
    plan = _moe_plan(flags, tms, EXPERT_TILE)
    xs = _moe_sort(hx, plan, tms, EXPERT_TILE)
    ys = _moe_experts(xs, plan, moe_w_gate[l].astype(BF16), moe_w_up[l].astype(BF16),
                      moe_w_down[l].astype(BF16), EXPERT_TILE)
    out = _moe_combine(hx, ys, x1.reshape(bsz * seq, d), mod, final_norm_w.reshape(1, d), plan, tms, seq // tms)
    return out.reshape(bsz, seq, d)
```

```python
import jax
import jax.numpy as jnp
from jax import lax
from jax.experimental import pallas as pl
from jax.experimental.pallas import tpu as pltpu

F32 = jnp.float32
BF16 = jnp.bfloat16
HIGHEST = lax.Precision.HIGHEST

CHUNK = 64
RWKV_CHUNK = 128
NEUMANN_BLOCK = 16
MIX_STEP_ROWS = 512
MLSTM_GROUP_ROWS = 128
MIX_SKEW = 2
EPS = 1e-6
RWKV_HEAD = 64
RWKV_HEADS = 8
RWKV_WIDTH = RWKV_HEADS * RWKV_HEAD
DECAY_LORA = 64
ICLR_LORA = 64
GATE_LORA = 128
GN_EPS = 64e-5
MLSTM_HEADS = 4
MLSTM_DK = 64
MLSTM_DV = 128
MLSTM_QK = MLSTM_HEADS * MLSTM_DK
MLSTM_WIDTH = MLSTM_HEADS * MLSTM_DV
CONV_WIDTH = 4
GATE_SOFTCAP = 15.0
RWKV_COLS = 3 * RWKV_WIDTH + DECAY_LORA + ICLR_LORA + GATE_LORA
N_GROUPS = 4
EXPERTS_PER_GROUP = 8
N_EXPERTS = N_GROUPS * EXPERTS_PER_GROUP
LANES = 128
MLSTM_MAIN = 2 * MLSTM_QK + 2 * MLSTM_WIDTH
MLSTM_PAD_COLS = MLSTM_MAIN + LANES
VMEM_LIMIT = 56 * 1024 * 1024


def _mm(a, b):
    return jnp.dot(a.astype(BF16), b.astype(BF16), preferred_element_type=F32)


def _nt(a, b):
    return lax.dot_general(a.astype(BF16), b.astype(BF16), (((1,), (1,)), ((), ())),
                           preferred_element_type=F32)


def _tn(a, b):
    return lax.dot_general(a.astype(BF16), b.astype(BF16), (((0,), (0,)), ((), ())),
                           preferred_element_type=F32)


def _mm_f32(a, b):
    return jnp.dot(a, b, preferred_element_type=F32, precision=HIGHEST)


def _cumsum_rows(tril, x):
    hi = x.astype(BF16)
    r1 = x - hi.astype(F32)
    mid = r1.astype(BF16)
    lo = (r1 - mid.astype(F32)).astype(BF16)
    n = x.shape[1]
    out = jnp.dot(tril, jnp.concatenate([hi, mid, lo], axis=1), preferred_element_type=F32)
    return out[:, :n] + out[:, n:2 * n] + out[:, 2 * n:]


def _sigmoid(z):
    return 1.0 / (1.0 + jnp.exp(-z))


def _softplus(z):
    return jnp.maximum(z, 0.0) + jnp.log1p(jnp.exp(-jnp.abs(z)))


def _log2(n):
    assert n & (n - 1) == 0
    return n.bit_length() - 1


def _params(*sem):
    return pltpu.CompilerParams(dimension_semantics=sem, vmem_limit_bytes=VMEM_LIMIT)


def _ada_kernel(c_ref, w_ref, b_ref, o_ref):
    c = c_ref[...]
    o_ref[...] = _mm_f32(c * _sigmoid(c), w_ref[...]) + b_ref[...]


def _ada(c, w, b):
    bsz, d = c.shape
    n = w.shape[1]
    return pl.pallas_call(
        _ada_kernel,
        grid=(n // d,),
        in_specs=[pl.BlockSpec((bsz, d), lambda j: (0, 0)),
                  pl.BlockSpec((d, d), lambda j: (0, j)),
                  pl.BlockSpec((1, d), lambda j: (0, j))],
        out_specs=pl.BlockSpec((bsz, d), lambda j: (0, j)),
        out_shape=jax.ShapeDtypeStruct((bsz, n), F32),
        compiler_params=_params("arbitrary"),
        name="ada",
    )(c, w, b.reshape(1, n))


def _norm_mod(x, nw, shift, scale):
    y = x * lax.rsqrt(jnp.mean(x * x, axis=-1, keepdims=True) + EPS) * nw
    return y * (1.0 + scale) + shift


def _in_proj_kernel(x_ref, mod_ref, nw_ref, wr_ref, wm_ref, pr_ref, pm_ref):
    h = _norm_mod(x_ref[...], nw_ref[...], mod_ref[0:1, :], mod_ref[1:2, :]).astype(BF16)
    pr_ref[...] = jnp.dot(h, wr_ref[...], preferred_element_type=F32)
    pm_ref[...] = jnp.dot(h, wm_ref[...], preferred_element_type=F32)


def _in_proj(x, mod, nw, w_r, w_m, tm):
    bsz, seq, d = x.shape
    nr, nm = w_r.shape[1], w_m.shape[1]
    return pl.pallas_call(
        _in_proj_kernel,
        grid=(bsz, seq // tm),
        in_specs=[pl.BlockSpec((None, tm, d), lambda b, i: (b, i, 0)),
                  pl.BlockSpec((None, 6, d), lambda b, i: (b, 0, 0)),
                  pl.BlockSpec((1, d), lambda b, i: (0, 0)),
                  pl.BlockSpec((d, nr), lambda b, i: (0, 0)),
                  pl.BlockSpec((d, nm), lambda b, i: (0, 0))],
        out_specs=[pl.BlockSpec((None, tm, nr), lambda b, i: (b, i, 0)),
                   pl.BlockSpec((None, tm, nm), lambda b, i: (b, i, 0))],
        out_shape=[jax.ShapeDtypeStruct((bsz, seq, nr), F32),
                   jax.ShapeDtypeStruct((bsz, seq, nm), F32)],
        compiler_params=_params("arbitrary", "arbitrary"),
        name="in_proj",
    )(x, mod, nw, w_r, w_m)


def _rwkv_programs(pr_ref, mu_ref, w0_ref, wup_ref, a0_ref, aup_ref, gup_ref, kk_ref, ka_ref,
                   rk_ref, gnw_ref, gnb_ref, hs_ref, o_ref, s_scr, prev_scr):
    C = RWKV_CHUNK
    N = RWKV_HEAD
    H = RWKV_HEADS
    rows = pr_ref.shape[0]
    nc = rows // C

    @pl.when(pl.program_id(1) == 0)
    def _():
        s_scr[...] = jnp.zeros_like(s_scr)
        prev_scr[...] = jnp.zeros_like(prev_scr)

    x = pr_ref[...]
    row1 = lax.broadcasted_iota(jnp.int32, x.shape, 0)
    prev = jnp.where(row1 == 0, prev_scr[...], pltpu.roll(x, 1, axis=0))
    prev_scr[...] = x[rows - 1:rows, :]
    u_all = x + (prev - x) * mu_ref[...]

    row = lax.broadcasted_iota(jnp.int32, (C, C), 0)
    col = lax.broadcasted_iota(jnp.int32, (C, C), 1)
    same = lambda blk: (row >> _log2(blk)) == (col >> _log2(blk))
    eye = (row == col).astype(F32)
    tril = (col <= row).astype(BF16)
    row2 = lax.broadcasted_iota(jnp.int32, (C, 2 * C), 0)
    col2 = lax.broadcasted_iota(jnp.int32, (C, 2 * C), 1) & (C - 1)
    strict2 = col2 < row2
    incl2 = col2 <= row2
    each = lambda f, *ls: [f(*t) for t in zip(*ls)]
    state = [s_scr[h] for h in range(H)]
    o1, o2, o3 = RWKV_WIDTH, 2 * RWKV_WIDTH, 3 * RWKV_WIDTH
    o4 = o3 + DECAY_LORA
    o5 = o4 + ICLR_LORA

    def solve(j):
        u = u_all[j * C:(j + 1) * C, :]
        r, k, v = u[:, :o1], u[:, o1:o2], u[:, o2:o3]
        wd, ad, gd = u[:, o3:o4], u[:, o4:o5], u[:, o5:]
        w_log = -_softplus(-(w0_ref[...] + _mm(jnp.tanh(wd), wup_ref[...]))) - 0.5
        lw = -jnp.exp(w_log)
        a = _sigmoid(a0_ref[...] + _mm(ad, aup_ref[...]))
        g = _mm(_sigmoid(gd), gup_ref[...])
        yield
        kk = k * kk_ref[...]
        k2 = k * (1.0 + (a - 1.0) * ka_ref[...])
        sq = kk * kk
        sq_hi = sq.astype(BF16)
        sq_lo = (sq - sq_hi.astype(F32)).astype(BF16)
        sums = (jnp.dot(sq_hi, hs_ref[...], preferred_element_type=F32)
                + jnp.dot(sq_lo, hs_ref[...], preferred_element_type=F32))
        kk = kk / jnp.maximum(jnp.sqrt(sums), 1e-12)
        rk = r * k2 * rk_ref[...]
        cum = _cumsum_rows(tril, lw)
        yield
        cl = cum[C - 1:C, :]
        p_inv = jnp.exp(-cum)
        p_rest = jnp.exp(cl - cum)
        b = kk * a
        ah_w = -kk * jnp.exp(cum - lw)
        rh_w = r * jnp.exp(cum)
        bt_w, kt_w = b * p_inv, k2 * p_inv
        b2_w, k2p_w = b * p_rest, k2 * p_rest
        p_end_w = jnp.exp(cl)
        cut = lambda t: [t[:, h * N:(h + 1) * N] for h in range(H)]
        ah, rh, bt, kt, b2, k2p, v_h, p_end, rk_h = (
            cut(t) for t in (ah_w, rh_w, bt_w, kt_w, b2_w, k2p_w, v, p_end_w, rk))
        yield
        ar = each(lambda a_, r_: jnp.concatenate([a_, r_], axis=0), ah, rh)
        bk = each(lambda b_, k_: jnp.concatenate([b_, k_], axis=0), bt, kt)
        x1 = each(_nt, ar, bk)
        l_both = each(lambda t: jnp.where(strict2, t[:C, :], 0.0), x1)
        m_both = each(lambda t: jnp.where(incl2, t[C:, :], 0.0), x1)
        l_ab = each(lambda t: t[:, :C], l_both)
        yield
        blk = NEUMANN_BLOCK
        n1 = each(lambda t: jnp.where(same(blk), t, 0.0), l_ab)
        xi = each(lambda t: eye + t, n1)
        nk = n1
        power = 2
        while power < blk:
            nk = each(lambda t: _mm(t, t), nk)
            xi = each(lambda x_, t: x_ + _mm(x_, t), xi, nk)
            power *= 2
            yield
        while blk < C:
            pair = same(2 * blk) & jnp.logical_not(same(blk))
            xe = each(lambda x_, t: _mm(x_, jnp.where(pair, t, 0.0)), xi, l_ab)
            xi = each(lambda x_, t: x_ + _mm(t, x_), xi, xe)
            blk *= 2
            yield
        zero = jnp.zeros((C, N), F32)
        z = each(lambda l_, v_: _mm(l_, jnp.concatenate([zero, v_], axis=0)), l_both, v_h)
        a2 = each(_mm, xi, ah)
        u0 = each(_mm, xi, z)
        yield
        uv = each(lambda u_, v_: jnp.concatenate([u_, v_], axis=0), u0, v_h)
        r2 = each(lambda r_, m_, a_: r_ + _mm(m_[:, :C], a_), rh, m_both, a2)
        y0 = each(_mm, m_both, uv)
        gt = each(_tn, a2, b2)
        ht = each(lambda uv_, b_, k_: _tn(uv_, jnp.concatenate([b_, k_], axis=0)), uv, b2, k2p)
        yield
        y = []
        for h in range(H):
            y.append(_nt(r2[h], state[h]) + y0[h])
            state[h] = state[h] * p_end[h] + _mm(state[h], gt[h]) + ht[h]
        yield
        yc = each(lambda t: t - jnp.mean(t, axis=-1, keepdims=True), y)
        var = each(lambda t: jnp.mean(t * t, axis=-1, keepdims=True), yc)
        rk_sum = each(lambda t: jnp.sum(t, axis=-1, keepdims=True), rk_h)
        yield
        for h in range(H):
            sl = slice(h * N, (h + 1) * N)
            yn = yc[h] * lax.rsqrt(var[h] + GN_EPS) * gnw_ref[:, sl] + gnb_ref[:, sl]
            o_ref[j * C:(j + 1) * C, sl] = (yn + rk_sum[h] * v_h[h]) * g[:, sl]

    def finish():
        for h in range(H):
            s_scr[h] = state[h]

    return [solve(j) for j in range(nc)], finish


def _mlstm_programs(pm_ref, cw_ref, cb_ref, gb_ref, hn_ref, o_ref, ext_scr, c_scr, n_scr, m_scr):
    C = CHUNK
    G = MLSTM_GROUP_ROWS
    H = MLSTM_HEADS
    DK, DV = MLSTM_DK, MLSTM_DV
    QK2 = 2 * MLSTM_QK
    rows = pm_ref.shape[0]

    @pl.when(pl.program_id(1) == 0)
    def _():
        ext_scr[0:8, :] = jnp.zeros((8, QK2), F32)
        c_scr[...] = jnp.zeros_like(c_scr)
        n_scr[...] = jnp.zeros_like(n_scr)
        m_scr[...] = jnp.zeros_like(m_scr)

    ext_scr[8:8 + rows, :] = pm_ref[:, 0:QK2]
    rg = lax.broadcasted_iota(jnp.int32, (G, G), 0)
    cg = lax.broadcasted_iota(jnp.int32, (G, G), 1)
    in_chunk = ((cg <= rg) & ((rg >> _log2(C)) == (cg >> _log2(C)))).astype(BF16)
    lane = lax.broadcasted_iota(jnp.int32, (G, LANES), 1)
    ri = lax.broadcasted_iota(jnp.int32, (LANES, LANES), 0)
    ci = lax.broadcasted_iota(jnp.int32, (LANES, LANES), 1)
    eye = (ri == ci).astype(F32)
    row = lax.broadcasted_iota(jnp.int32, (C, C), 0)
    col = lax.broadcasted_iota(jnp.int32, (C, C), 1)
    incl = col <= row
    each = lambda f, *ls: [f(*t) for t in zip(*ls)]
    m_run = [m_scr[h][0:1, 0:1] for h in range(H)]
    c_run = [c_scr[h] for h in range(H)]
    n_run = [n_scr[h][0:1, 0:DK] for h in range(H)]

    def solve(p):
        r0 = p * G
        conv = cb_ref[...] + cw_ref[0:1, :] * ext_scr[5 + r0:5 + r0 + G, :]
        for i in range(1, CONV_WIDTH):
            conv = conv + cw_ref[i:i + 1, :] * ext_scr[5 + i + r0:5 + i + r0 + G, :]
        qk = conv * _sigmoid(conv)
        q = qk[:, :MLSTM_QK] * (DK ** -0.5)
        k = qk[:, MLSTM_QK:]
        v = pm_ref[r0:r0 + G, QK2:QK2 + MLSTM_WIDTH]
        og = pm_ref[r0:r0 + G, QK2 + MLSTM_WIDTH:MLSTM_MAIN]
        z = GATE_SOFTCAP * jnp.tanh((pm_ref[r0:r0 + G, MLSTM_MAIN:] + gb_ref[...]) / GATE_SOFTCAP)
        log_f = jnp.minimum(z, 0.0) - jnp.log1p(jnp.exp(-jnp.abs(z)))
        gates = jnp.where(lane < H, z, _cumsum_rows(in_chunk, log_f))
        gates_t = lax.dot_general(eye, gates, (((1,), (1,)), ((), ())),
                                  preferred_element_type=F32, precision=HIGHEST)
        yield
        units = [(j, h) for j in range(G // C) for h in range(H)]
        li_col = [gates[j * C:(j + 1) * C, h:h + 1] for j, h in units]
        b_col = [gates[j * C:(j + 1) * C, H + h:H + h + 1] for j, h in units]
        li_row = [gates_t[h:h + 1, j * C:(j + 1) * C] for j, h in units]
        b_row = [gates_t[H + h:H + h + 1, j * C:(j + 1) * C] for j, h in units]
        q_h = [q[j * C:(j + 1) * C, h * DK:(h + 1) * DK] for j, h in units]
        k_h = [k[j * C:(j + 1) * C, h * DK:(h + 1) * DK] for j, h in units]
        v_h = [v[j * C:(j + 1) * C, h * DV:(h + 1) * DV] for j, h in units]
        b_last = each(lambda bc: bc[C - 1:C, :], b_col)
        end_log = each(lambda bl, bc, lc: bl - bc + lc, b_last, b_col, li_col)
        end_max = each(lambda el: jnp.max(el, axis=0, keepdims=True), end_log)
        m, m_new = [], []
        for i, (j, h) in enumerate(units):
            m.append(m_run[h])
            m_run[h] = jnp.maximum(b_last[i] + m_run[h], end_max[i])
            m_new.append(m_run[h])
        yield
        qk_h = each(_nt, q_h, k_h)
        d_mat = each(lambda bc, br, lr: jnp.where(incl, bc - br + lr, -jnp.inf), b_col, b_row, li_row)
        inter = each(lambda bc, m_: bc + m_, b_col, m)
        m_t = each(lambda i_, d_: jnp.maximum(i_, jnp.max(d_, axis=-1, keepdims=True)), inter, d_mat)
        yield
        scores = each(lambda s_, d_, mt: s_ * jnp.exp(d_ - mt), qk_h, d_mat, m_t)
        w_inter = each(lambda i_, mt: jnp.exp(i_ - mt), inter, m_t)
        sv_h = each(_mm, scores, v_h)
        carry_w = each(lambda bl, m_, mn: jnp.exp(bl + m_ - mn), b_last, m, m_new)
        kw = each(lambda k_, el, mn: k_ * jnp.exp(el - mn), k_h, end_log, m_new)
        kv_h = each(_tn, kw, v_h)
        yield
        c_mat, n_vec = [], []
        for i, (j, h) in enumerate(units):
            c_mat.append(c_run[h])
            n_vec.append(n_run[h])
            c_run[h] = carry_w[i] * c_run[h] + kv_h[i]
            n_run[h] = carry_w[i] * n_run[h] + jnp.sum(kw[i], axis=0, keepdims=True)
        qc_h = each(_mm, q_h, c_mat)
        yield
        num = each(lambda s_, w_, c_: s_ + w_ * c_, sv_h, w_inter, qc_h)
        s_sum = each(lambda s_: jnp.sum(s_, axis=-1, keepdims=True), scores)
        qn = each(lambda q_, n_: jnp.sum(q_ * n_, axis=-1, keepdims=True), q_h, n_vec)
        den = each(lambda s_, w_, q_: s_ + w_ * q_, s_sum, w_inter, qn)
        hh = each(lambda n_, d_, mt: n_ / jnp.maximum(jnp.abs(d_), jnp.exp(-mt)), num, den, m_t)
        ms = each(lambda t: jnp.mean(t * t, axis=-1, keepdims=True), hh)
        yield
        for i, (j, h) in enumerate(units):
            rs = slice(r0 + j * C, r0 + (j + 1) * C)
            hn = hh[i] * lax.rsqrt(ms[i] + EPS) * hn_ref[:, h * DV:(h + 1) * DV]
            o_ref[rs, h * DV:(h + 1) * DV] = hn * _sigmoid(og[j * C:(j + 1) * C, h * DV:(h + 1) * DV])

    def finish():
        ext_scr[0:8, :] = ext_scr[rows:rows + 8, :]
        for h in range(H):
            c_scr[h] = c_run[h]
            n_scr[h] = jnp.broadcast_to(jnp.concatenate([n_run[h], jnp.zeros((1, LANES - DK), F32)], axis=1),
                                        (8, LANES))
            m_scr[h] = jnp.broadcast_to(m_run[h], (8, LANES))

    return [solve(p) for p in range(rows // G)], finish


def _mixers_kernel(pr_ref, mu_ref, w0_ref, wup_ref, a0_ref, aup_ref, gup_ref, kk_ref, ka_ref, rk_ref, gnw_ref,
                   gnb_ref, hs_ref, pm_ref, cw_ref, cb_ref, gb_ref, hn_ref, yr_ref, ym_ref,
                   s_scr, prev_scr, ext_scr, c_scr, n_scr, m_scr):
    r_progs, r_finish = _rwkv_programs(pr_ref, mu_ref, w0_ref, wup_ref, a0_ref, aup_ref, gup_ref, kk_ref, ka_ref,
                                       rk_ref, gnw_ref, gnb_ref, hs_ref, yr_ref, s_scr, prev_scr)
    m_progs, m_finish = _mlstm_programs(pm_ref, cw_ref, cb_ref, gb_ref, hn_ref, ym_ref, ext_scr, c_scr, n_scr,
                                        m_scr)
    pending = [g for pair in zip(r_progs, m_progs) for g in pair]
    active = []
    tick = 0
    while pending or active:
        if pending and tick % MIX_SKEW == 0:
            active.append(pending.pop(0))
        for gen in list(active):
            try:
                next(gen)
            except StopIteration:
                active.remove(gen)
        tick += 1
    r_finish()
    m_finish()


def _mixers(pr, pm, mu, w0, w_up, a0, a_up, g_up, k_k, k_a, r_k, gn_w, gn_b, conv_w, conv_b, i_b, f_b, hn_w, rows):
    bsz, seq, cols = pr.shape
    mcols = pm.shape[2]
    row = lambda p: p.reshape(1, -1)
    full = lambda a: pl.BlockSpec(a.shape, lambda b, c: (0,) * a.ndim)
    tok = lambda n: pl.BlockSpec((None, rows, n), lambda b, c: (b, c, 0))
    lane_head = jnp.arange(RWKV_WIDTH) // RWKV_HEAD
    same_head = (lane_head[:, None] == lane_head[None, :]).astype(BF16)
    gate_b = jnp.zeros((1, LANES), F32).at[0, :MLSTM_HEADS].set(i_b).at[0, MLSTM_HEADS:2 * MLSTM_HEADS].set(f_b)
    r_params = [row(mu), row(w0), w_up, row(a0), a_up, g_up, row(k_k), row(k_a), row(r_k), row(gn_w), row(gn_b),
                same_head]
    m_params = [conv_w, row(conv_b), gate_b, row(hn_w)]
    return pl.pallas_call(
        _mixers_kernel,
        grid=(bsz, seq // rows),
        in_specs=[tok(cols)] + [full(a) for a in r_params] + [tok(mcols)] + [full(a) for a in m_params],
        out_specs=[tok(RWKV_WIDTH), tok(MLSTM_WIDTH)],
        out_shape=[jax.ShapeDtypeStruct((bsz, seq, RWKV_WIDTH), F32),
                   jax.ShapeDtypeStruct((bsz, seq, MLSTM_WIDTH), F32)],
        scratch_shapes=[pltpu.VMEM((RWKV_HEADS, RWKV_HEAD, RWKV_HEAD), F32),
                        pltpu.VMEM((1, cols), F32),
                        pltpu.VMEM((rows + 8, 2 * MLSTM_QK), F32),
                        pltpu.VMEM((MLSTM_HEADS, MLSTM_DK, MLSTM_DV), F32),
                        pltpu.VMEM((MLSTM_HEADS, 8, LANES), F32),
                        pltpu.VMEM((MLSTM_HEADS, 8, LANES), F32)],
        compiler_params=_params("arbitrary", "arbitrary"),
        name="mixers",
    )(pr, *r_params, pm, *m_params)


def _mix_out_kernel(x_ref, yr_ref, ym_ref, mod_ref, nw_ref, wor_ref, wom_ref, wr_ref, br_ref, x1_ref, hx_ref):
    d = x_ref.shape[-1]
    y = (jnp.dot(yr_ref[...].astype(BF16), wor_ref[...], preferred_element_type=F32)
         + jnp.dot(ym_ref[...].astype(BF16), wom_ref[...], preferred_element_type=F32))
    x1 = x_ref[...] + mod_ref[2:3, :] * y
    x1_ref[...] = x1
    h = _norm_mod(x1, nw_ref[...], mod_ref[3:4, :], mod_ref[4:5, :])
    h_hi = h.astype(BF16)
    hx_ref[:, :d] = h_hi

    h_lo = (h - h_hi.astype(F32)).astype(BF16)
    w = wr_ref[...]
    w_hi = w.astype(BF16)
    w_lo = (w - w_hi.astype(F32)).astype(BF16)
    hw = jnp.dot(h_hi, jnp.concatenate([w_hi, w_lo], axis=1), preferred_element_type=F32)
    logits = hw[:, :LANES] + hw[:, LANES:] + jnp.dot(h_lo, w_hi, preferred_element_type=F32) + br_ref[...]
    lane = lax.broadcasted_iota(jnp.int32, logits.shape, 1)
    neg = -jnp.inf
    el = logits
    gl = jnp.where((lane >= N_EXPERTS) & (lane < N_EXPERTS + N_GROUPS), logits, neg)
    g_max = jnp.max(gl, axis=-1, keepdims=True)
    g_idx = jnp.min(jnp.where(gl == g_max, lane, LANES), axis=-1, keepdims=True) - N_EXPERTS
    g_w = 1.0 / jnp.sum(jnp.exp(gl - g_max), axis=-1, keepdims=True)
    in_group = ((lane >> 3) == g_idx) & (lane < N_EXPERTS)
    sel = jnp.where(in_group, el, neg)
    v1 = jnp.max(sel, axis=-1, keepdims=True)
    i1 = jnp.min(jnp.where(sel == v1, lane, LANES), axis=-1, keepdims=True)
    rest = jnp.where(lane == i1, neg, sel)
    v2 = jnp.max(rest, axis=-1, keepdims=True)
    i2 = jnp.min(jnp.where(rest == v2, lane, LANES), axis=-1, keepdims=True)
    e2 = jnp.exp(v2 - v1)
    w1 = 1.0 / (1.0 + e2)
    w2 = e2 / (1.0 + e2)
    cw = jnp.where(lane == i1, w1 * g_w, jnp.where(lane == i2, w2 * g_w, 0.0))
    hi = cw.astype(BF16).astype(F32)
    mid = (cw - hi).astype(BF16).astype(F32)
    lo = (cw - hi - mid).astype(BF16).astype(F32)
    flag = jnp.where(hi != 0.0, 1.0, 0.0)
    pack = (hi + pltpu.roll(mid, N_EXPERTS, axis=1) + pltpu.roll(lo, 2 * N_EXPERTS, axis=1)
            + pltpu.roll(flag, 3 * N_EXPERTS, axis=1))
    hx_ref[:, d:] = pack.astype(BF16)


def _mix_out(x, y_r, y_m, mod, nw, wo_r, wo_m, w_route, b_route, tm):
    bsz, seq, d = x.shape
    tok = lambda n: pl.BlockSpec((None, tm, n), lambda b, i: (b, i, 0))
    full = lambda a: pl.BlockSpec(a.shape, lambda b, i: (0, 0))
    return pl.pallas_call(
        _mix_out_kernel,
        grid=(bsz, seq // tm),
        in_specs=[tok(d), tok(RWKV_WIDTH), tok(MLSTM_WIDTH),
                  pl.BlockSpec((None, 6, d), lambda b, i: (b, 0, 0)),
                  full(nw), full(wo_r), full(wo_m), full(w_route), full(b_route)],
        out_specs=[tok(d), tok(d + LANES)],
        out_shape=[jax.ShapeDtypeStruct((bsz, seq, d), F32),
                   jax.ShapeDtypeStruct((bsz, seq, d + LANES), BF16)],
        compiler_params=_params("arbitrary", "arbitrary"),
        name="mix_out",
    )(x, y_r, y_m, mod, nw, wo_r, wo_m, w_route, b_route)


ROW_CHUNK = 16
SORT_TILE = 512
EXPERT_TILE = 512
NO_SLOT = 1e9


def _moe_plan(flags, tms, ts):
    t, ne = flags.shape
    nt = t // tms
    i32 = jnp.int32
    tri = lambda n: (jnp.arange(n)[:, None] < jnp.arange(n)[None, :]).astype(F32)
    before = lambda a, b: jnp.dot(a, b, precision=HIGHEST).astype(i32)
    cnt = flags.reshape(nt, tms, ne).astype(i32).sum(axis=1)
    pc = (cnt + ROW_CHUNK - 1) // ROW_CHUNK * ROW_CHUNK
    pcf = pc.astype(F32)
    loc_start = before(pcf, tri(ne))
    len_e = pc.sum(axis=0)
    seg_e = (len_e + ts - 1) // ts * ts
    e_start = before(seg_e.astype(F32)[None, :], tri(ne))[0]
    e_end = e_start + seg_e
    dst_start = e_start[None, :] + before(tri(nt).T, pcf)
    n_tiles_max = _moe_max_rows(t, ne, tms, ts) // ts
    tile_row = jnp.arange(n_tiles_max, dtype=i32) * ts
    tile_expert = jnp.minimum((tile_row[:, None] >= e_end[None, :]).astype(i32).sum(axis=1), ne - 1)
    return dict(
        loc_start=loc_start.reshape(-1).astype(i32), n_chunks=(pc // ROW_CHUNK).reshape(-1).astype(i32),
        dst_start=dst_start.reshape(-1).astype(i32), tail_start=(e_start + len_e).astype(i32),
        tail_chunks=((seg_e - len_e) // ROW_CHUNK).astype(i32),
        n_used=(e_end[-1:] // ts).astype(i32), tile_expert=tile_expert,
        loc_start_vec=jnp.pad(loc_start.astype(F32), ((0, 0), (0, LANES - ne))).reshape(nt, 1, LANES))


def _moe_max_rows(t, ne, tms, ts):
    rows = 2 * t + (t // tms) * ne * (ROW_CHUNK - 1) + ne * (ts - 1)
    return (rows + ts - 1) // ts * ts


def _local_rows(tms):
    return 2 * tms + N_EXPERTS * ROW_CHUNK


def _tile_slots(blk, loc_start_vec):
    tms = blk.shape[0]
    lane = lax.broadcasted_iota(jnp.int32, blk.shape, 1)
    m = jnp.where(lane < N_EXPERTS, pltpu.roll(blk, N_EXPERTS, axis=1), 0.0)
    row = lax.broadcasted_iota(jnp.int32, (tms, tms), 0)
    col = lax.broadcasted_iota(jnp.int32, (tms, tms), 1)
    before = jnp.dot((col < row).astype(BF16), m.astype(BF16), preferred_element_type=F32)
    slot = loc_start_vec + before
    slot_a = jnp.min(jnp.where(m > 0.0, slot, NO_SLOT), axis=-1, keepdims=True)
    slot_b = jnp.max(jnp.where(m > 0.0, slot, -NO_SLOT), axis=-1, keepdims=True)
    return slot_a, slot_b


def _moe_sort_kernel(ls_ref, nch_ref, ds_ref, tls_ref, tlc_ref, nu_ref, hx_ref, lsv_ref, xo_ref, xs_scr, z_scr,
                     sem, zsem):
    i = pl.program_id(0)
    tms, width = hx_ref.shape
    loc = xs_scr.shape[0]
    d = width - LANES
    slot_a, slot_b = _tile_slots(hx_ref[:, d:].astype(F32), lsv_ref[...])
    lane = lax.broadcasted_iota(jnp.int32, (tms, LANES), 1)
    packed = jnp.where(lane == 0, slot_a, jnp.where(lane == 1, slot_b, 0.0))
    ri = lax.broadcasted_iota(jnp.int32, (LANES, LANES), 0)
    ci = lax.broadcasted_iota(jnp.int32, (LANES, LANES), 1)
    slots_t = lax.dot_general((ri == ci).astype(F32), packed, (((1,), (1,)), ((), ())),
                              preferred_element_type=F32, precision=HIGHEST)
    srow = lax.broadcasted_iota(jnp.int32, (loc, tms), 0).astype(F32)
    perm = jnp.where((srow == slots_t[0:1, :]) | (srow == slots_t[1:2, :]), 1.0, 0.0).astype(BF16)
    xs_scr[...] = jnp.dot(perm, hx_ref[...], preferred_element_type=F32).astype(BF16)

    def copy(src, src_row, dst_row):
        return pltpu.make_async_copy(src.at[pl.ds(pl.multiple_of(src_row, ROW_CHUNK), ROW_CHUNK)],
                                     xo_ref.at[pl.ds(pl.multiple_of(dst_row, ROW_CHUNK), ROW_CHUNK)], sem)

    n_started = jnp.int32(0)
    for e in range(N_EXPERTS):
        n = nch_ref[i * N_EXPERTS + e]
        l0 = ls_ref[i * N_EXPERTS + e]
        d0 = ds_ref[i * N_EXPERTS + e]

        def run_body(j, carry, l0=l0, d0=d0):
            copy(xs_scr, l0 + j * ROW_CHUNK, d0 + j * ROW_CHUNK).start()
            return carry

        lax.fori_loop(0, n, run_body, 0)
        n_started = n_started + n

    z_scr[...] = jnp.zeros_like(z_scr)
    ts = z_scr.shape[0]
    n_tail = jnp.int32(0)
    for e in range(N_EXPERTS):
        n = jnp.where(i == 0, tlc_ref[e], 0)
        t0 = tls_ref[e]

        def tail_body(j, carry, t0=t0):
            copy(z_scr, 0, t0 + j * ROW_CHUNK).start()
            return carry

        lax.fori_loop(0, n, tail_body, 0)
        n_tail = n_tail + n

    def tile_copy(tile):
        return pltpu.make_async_copy(z_scr, xo_ref.at[pl.ds(pl.multiple_of(tile * ts, ts), ts)], zsem)

    n_spare = jnp.where(i == 0, xo_ref.shape[0] // ts - nu_ref[0], 0)

    def spare_body(j, carry):
        tile_copy(nu_ref[0] + j).start()
        return carry

    lax.fori_loop(0, n_spare, spare_body, 0)

    def wait_body(j, carry):
        copy(xs_scr, 0, 0).wait()
        return carry

    lax.fori_loop(0, n_started + n_tail, wait_body, 0)

    def spare_wait_body(j, carry):
        tile_copy(0).wait()
        return carry

    lax.fori_loop(0, n_spare, spare_wait_body, 0)


def _moe_sort(hx, plan, tms, ts):
    t, width = hx.shape
    nt = t // tms
    rows = _moe_max_rows(t, N_EXPERTS, tms, ts)
    return pl.pallas_call(
        _moe_sort_kernel,
        grid_spec=pltpu.PrefetchScalarGridSpec(
            num_scalar_prefetch=6,
            grid=(nt,),
            in_specs=[pl.BlockSpec((tms, width), lambda i, *_: (i, 0)),
                      pl.BlockSpec((None, 1, LANES), lambda i, *_: (i, 0, 0))],
            out_specs=pl.BlockSpec(memory_space=pl.ANY),
            scratch_shapes=[pltpu.VMEM((_local_rows(tms), width), BF16),
                            pltpu.VMEM((ts, width), BF16),
                            pltpu.SemaphoreType.DMA(()),
                            pltpu.SemaphoreType.DMA(())]),
        out_shape=jax.ShapeDtypeStruct((rows, width), BF16),
        compiler_params=_params("arbitrary"),
        name="moe_sort",
    )(plan["loc_start"], plan["n_chunks"], plan["dst_start"], plan["tail_start"], plan["tail_chunks"],
      plan["n_used"], hx, plan["loc_start_vec"])


def _moe_experts_kernel(te_ref, nu_ref, xs_ref, wg_ref, wu_ref, wd_ref, o_ref):
    j = pl.program_id(0)

    @pl.when(j < nu_ref[0])
    def _():
        d = xs_ref.shape[1] - LANES
        e = te_ref[j]
        x = xs_ref[:, :d]
        wblk = xs_ref[:, d:].astype(F32)
        lane = lax.broadcasted_iota(jnp.int32, wblk.shape, 1)
        mine = (lane == e) | (lane == e + N_EXPERTS) | (lane == e + 2 * N_EXPERTS)
        cw = jnp.sum(jnp.where(mine, wblk, 0.0), axis=-1, keepdims=True)
        hg = jnp.dot(x, wg_ref[...], preferred_element_type=F32)
        hu = jnp.dot(x, wu_ref[...], preferred_element_type=F32)
        act = hg * _sigmoid(hg) * hu * cw
        o_ref[...] = jnp.dot(act.astype(BF16), wd_ref[...], preferred_element_type=F32).astype(BF16)

    @pl.when(j >= nu_ref[0])
    def _():
        o_ref[...] = jnp.zeros_like(o_ref)


def _moe_experts(xs, plan, w_gate, w_up, w_down, ts):
    rows, width = xs.shape
    ne, d, f = w_gate.shape
    tile = lambda j, te, nu: (jnp.minimum(j, nu[0] - 1), 0)
    return pl.pallas_call(
        _moe_experts_kernel,
        grid_spec=pltpu.PrefetchScalarGridSpec(
            num_scalar_prefetch=2,
            grid=(rows // ts,),
            in_specs=[pl.BlockSpec((ts, width), tile),
                      pl.BlockSpec((None, d, f), lambda j, te, nu: (te[j], 0, 0)),
                      pl.BlockSpec((None, d, f), lambda j, te, nu: (te[j], 0, 0)),
                      pl.BlockSpec((None, f, d), lambda j, te, nu: (te[j], 0, 0))],
            out_specs=pl.BlockSpec((ts, d), lambda j, te, nu: (j, 0))),
        out_shape=jax.ShapeDtypeStruct((rows, d), BF16),
        compiler_params=_params("arbitrary"),
        name="moe_experts",
    )(plan["tile_expert"], plan["n_used"], xs, w_gate, w_up, w_down)


def _moe_combine_kernel(ls_ref, nch_ref, ds_ref, wblk_ref, lsv_ref, x1_ref, mod_ref, fw_ref, ys_hbm,
                        o_ref, ys_scr, sems):
    i = pl.program_id(0)
    nt = pl.num_programs(0)
    tms = x1_ref.shape[0]
    loc = ys_scr.shape[1]

    def copy(slot, src_row, dst_row):
        return pltpu.make_async_copy(
            ys_hbm.at[pl.ds(pl.multiple_of(src_row, ROW_CHUNK), ROW_CHUNK)],
            ys_scr.at[slot, pl.ds(pl.multiple_of(dst_row, ROW_CHUNK), ROW_CHUNK)], sems.at[slot])

    def fetch(tile, slot):
        for e in range(N_EXPERTS):
            n = nch_ref[tile * N_EXPERTS + e]
            l0 = ls_ref[tile * N_EXPERTS + e]
            d0 = ds_ref[tile * N_EXPERTS + e]

            def run_body(j, carry, l0=l0, d0=d0):
                copy(slot, d0 + j * ROW_CHUNK, l0 + j * ROW_CHUNK).start()
                return carry

            lax.fori_loop(0, n, run_body, 0)

    @pl.when(i == 0)
    def _():
        ys_scr[...] = jnp.zeros_like(ys_scr)
        fetch(0, 0)

    slot = i % 2

    @pl.when(i + 1 < nt)
    def _():
        fetch(i + 1, 1 - slot)

    slot_a, slot_b = _tile_slots(wblk_ref[...].astype(F32), lsv_ref[...])
    scol = lax.broadcasted_iota(jnp.int32, (tms, loc), 1).astype(F32)
    perm = jnp.where((scol == slot_a) | (scol == slot_b), 1.0, 0.0).astype(BF16)

    n_mine = jnp.int32(0)
    for e in range(N_EXPERTS):
        n_mine = n_mine + nch_ref[i * N_EXPERTS + e]

    def wait_body(j, carry):
        copy(slot, 0, 0).wait()
        return carry

    lax.fori_loop(0, n_mine, wait_body, 0)
    y = jnp.dot(perm, ys_scr[slot], preferred_element_type=F32)
    x2 = x1_ref[...] + mod_ref[5:6, :] * y
    o_ref[...] = x2 * lax.rsqrt(jnp.mean(x2 * x2, axis=-1, keepdims=True) + EPS) * fw_ref[...]


def _moe_combine(hx, ys, x1, mod, fw, plan, tms, tiles_per_seq):
    t, d = x1.shape
    nt = t // tms
    return pl.pallas_call(
        _moe_combine_kernel,
        grid_spec=pltpu.PrefetchScalarGridSpec(
            num_scalar_prefetch=3,
            grid=(nt,),
            in_specs=[pl.BlockSpec((tms, LANES), lambda i, *_: (i, d // LANES)),
                      pl.BlockSpec((None, 1, LANES), lambda i, *_: (i, 0, 0)),
                      pl.BlockSpec((tms, d), lambda i, *_: (i, 0)),
                      pl.BlockSpec((None, 6, d), lambda i, *_: (i // tiles_per_seq, 0, 0)),
                      pl.BlockSpec((1, d), lambda i, *_: (0, 0)),
                      pl.BlockSpec(memory_space=pl.ANY)],
            out_specs=pl.BlockSpec((tms, d), lambda i, *_: (i, 0)),
            scratch_shapes=[pltpu.VMEM((2, _local_rows(tms), d), BF16),
                            pltpu.SemaphoreType.DMA((2,))]),
        out_shape=jax.ShapeDtypeStruct((t, d), F32),
        compiler_params=_params("arbitrary"),
        name="moe_combine",
    )(plan["loc_start"], plan["n_chunks"], plan["dst_start"], hx, plan["loc_start_vec"], x1, mod, fw, ys)


def _pad_cols(w, n):
    return jnp.pad(w, ((0, 0), (0, n - w.shape[1])))


def kernel(x, c, ada_w, ada_b, mix_norm_w, w_in, rwkv_mu, rwkv_w0, rwkv_w_up, rwkv_a0, rwkv_a_up, rwkv_g_up, rwkv_k_k, rwkv_k_a, rwkv_r_k, rwkv_gn_w, rwkv_gn_b, mlstm_conv_w, mlstm_conv_b, mlstm_i_b, mlstm_f_b, mlstm_hn_w, w_out, ffn_norm_w, moe_w_group, moe_b_group, moe_w_router, moe_b_router, moe_w_gate, moe_w_up, moe_w_down, final_norm_w):
    bsz, seq, d = x.shape
    assert ada_w.shape[0] == 1, "the fused final norm assumes a single layer"
    assert seq % CHUNK == 0
    l = 0
    tm = min(512, seq)
    tms = min(SORT_TILE, seq)
    mod = _ada(c, ada_w[l], ada_b[l]).reshape(bsz, 6, d)
    w_r = w_in[l][:, :RWKV_COLS].astype(BF16)
    w_m = _pad_cols(w_in[l][:, RWKV_COLS:], MLSTM_PAD_COLS).astype(BF16)
    pr, pm = _in_proj(x, mod, mix_norm_w[l].reshape(1, d), w_r, w_m, tm)
    y_r, y_m = _mixers(pr, pm, rwkv_mu[l], rwkv_w0[l], rwkv_w_up[l], rwkv_a0[l], rwkv_a_up[l], rwkv_g_up[l],
                       rwkv_k_k[l], rwkv_k_a[l], rwkv_r_k[l], rwkv_gn_w[l], rwkv_gn_b[l], mlstm_conv_w[l],
                       mlstm_conv_b[l], mlstm_i_b[l], mlstm_f_b[l], mlstm_hn_w[l], min(MIX_STEP_ROWS, seq))
    wo = w_out[l].astype(BF16)
    w_route = _pad_cols(jnp.concatenate([moe_w_router[l], moe_w_group[l]], axis=1), LANES)
    b_route = _pad_cols(jnp.concatenate([moe_b_router[l], moe_b_group[l]]).reshape(1, -1), LANES)
    x1, hx = _mix_out(
        x, y_r, y_m, mod, ffn_norm_w[l].reshape(1, d), wo[:RWKV_WIDTH], wo[RWKV_WIDTH:], w_route, b_route, tm)
    hx = hx.reshape(bsz * seq, d + LANES)
    flags = hx[:, d + 3 * N_EXPERTS:] != 0
    plan = _moe_plan(flags, tms, EXPERT_TILE)
    xs = _moe_sort(hx, plan, tms, EXPERT_TILE)
    ys = _moe_experts(xs, plan, moe_w_gate[l].astype(BF16), moe_w_up[l].astype(BF16),
                      moe_w_down[l].astype(BF16), EXPERT_TILE)
    out = _moe_combine(hx, ys, x1.reshape(bsz * seq, d), mod, final_norm_w.reshape(1, d), plan, tms, seq // tms)
    return out.reshape(bsz, seq, d)
```

```python
import jax
import jax.numpy as jnp
from jax import lax
from jax.experimental import pallas as pl
from jax.experimental.pallas import tpu as pltpu

F32 = jnp.float32
BF16 = jnp.bfloat16
HIGHEST = lax.Precision.HIGHEST

CHUNK = 64
RWKV_CHUNK = 128
NEUMANN_BLOCK = 16
MIX_STEP_ROWS = 512
MLSTM_GROUP_ROWS = 128
MIX_SKEW = 2
EPS = 1e-6
RWKV_HEAD = 64
RWKV_HEADS = 8
RWKV_WIDTH = RWKV_HEADS * RWKV_HEAD
DECAY_LORA = 64
ICLR_LORA = 64
GATE_LORA = 128
GN_EPS = 64e-5
MLSTM_HEADS = 4
MLSTM_DK = 64
MLSTM_DV = 128
MLSTM_QK = MLSTM_HEADS * MLSTM_DK
MLSTM_WIDTH = MLSTM_HEADS * MLSTM_DV
CONV_WIDTH = 4
GATE_SOFTCAP = 15.0
RWKV_COLS = 3 * RWKV_WIDTH + DECAY_LORA + ICLR_LORA + GATE_LORA
N_GROUPS = 4
EXPERTS_PER_GROUP = 8
N_EXPERTS = N_GROUPS * EXPERTS_PER_GROUP
LANES = 128
MLSTM_MAIN = 2 * MLSTM_QK + 2 * MLSTM_WIDTH
MLSTM_PAD_COLS = MLSTM_MAIN + LANES
VMEM_LIMIT = 56 * 1024 * 1024


def _mm(a, b):
    return jnp.dot(a.astype(BF16), b.astype(BF16), preferred_element_type=F32)


def _nt(a, b):
    return lax.dot_general(a.astype(BF16), b.astype(BF16), (((1,), (1,)), ((), ())),
                           preferred_element_type=F32)


def _tn(a, b):
    return lax.dot_general(a.astype(BF16), b.astype(BF16), (((0,), (0,)), ((), ())),
                           preferred_element_type=F32)


def _mm_f32(a, b):
    return jnp.dot(a, b, preferred_element_type=F32, precision=HIGHEST)


def _cumsum_rows(tril, x):
    hi = x.astype(BF16)
    r1 = x - hi.astype(F32)
    mid = r1.astype(BF16)
    lo = (r1 - mid.astype(F32)).astype(BF16)
    n = x.shape[1]
    out = jnp.dot(tril, jnp.concatenate([hi, mid, lo], axis=1), preferred_element_type=F32)
    return out[:, :n] + out[:, n:2 * n] + out[:, 2 * n:]


def _sigmoid(z):
    return 1.0 / (1.0 + jnp.exp(-z))


def _softplus(z):
    return jnp.maximum(z, 0.0) + jnp.log1p(jnp.exp(-jnp.abs(z)))


def _log2(n):
    assert n & (n - 1) == 0
    return n.bit_length() - 1


def _params(*sem):
    return pltpu.CompilerParams(dimension_semantics=sem, vmem_limit_bytes=VMEM_LIMIT)


def _ada_kernel(c_ref, w_ref, b_ref, o_ref):
    c = c_ref[...]
    o_ref[...] = _mm_f32(c * _sigmoid(c), w_ref[...]) + b_ref[...]


def _ada(c, w, b):
    bsz, d = c.shape
    n = w.shape[1]
    return pl.pallas_call(
        _ada_kernel,
        grid=(n // d,),
        in_specs=[pl.BlockSpec((bsz, d), lambda j: (0, 0)),
                  pl.BlockSpec((d, d), lambda j: (0, j)),
                  pl.BlockSpec((1, d), lambda j: (0, j))],
        out_specs=pl.BlockSpec((bsz, d), lambda j: (0, j)),
        out_shape=jax.ShapeDtypeStruct((bsz, n), F32),
        compiler_params=_params("arbitrary"),
        name="ada",
    )(c, w, b.reshape(1, n))


def _norm_mod(x, nw, shift, scale):
    y = x * lax.rsqrt(jnp.mean(x * x, axis=-1, keepdims=True) + EPS) * nw
    return y * (1.0 + scale) + shift


def _in_proj_kernel(x_ref, mod_ref, nw_ref, wr_ref, wm_ref, pr_ref, pm_ref):
    h = _norm_mod(x_ref[...], nw_ref[...], mod_ref[0:1, :], mod_ref[1:2, :]).astype(BF16)
    pr_ref[...] = jnp.dot(h, wr_ref[...], preferred_element_type=F32)
    pm_ref[...] = jnp.dot(h, wm_ref[...], preferred_element_type=F32)


def _in_proj(x, mod, nw, w_r, w_m, tm):
    bsz, seq, d = x.shape
    nr, nm = w_r.shape[1], w_m.shape[1]
    return pl.pallas_call(
        _in_proj_kernel,
        grid=(bsz, seq // tm),
        in_specs=[pl.BlockSpec((None, tm, d), lambda b, i: (b, i, 0)),
                  pl.BlockSpec((None, 6, d), lambda b, i: (b, 0, 0)),
                  pl.BlockSpec((1, d), lambda b, i: (0, 0)),
                  pl.BlockSpec((d, nr), lambda b, i: (0, 0)),
                  pl.BlockSpec((d, nm), lambda b, i: (0, 0))],
        out_specs=[pl.BlockSpec((None, tm, nr), lambda b, i: (b, i, 0)),
                   pl.BlockSpec((None, tm, nm), lambda b, i: (b, i, 0))],
        out_shape=[jax.ShapeDtypeStruct((bsz, seq, nr), F32),
                   jax.ShapeDtypeStruct((bsz, seq, nm), F32)],
        compiler_params=_params("arbitrary", "arbitrary"),
        name="in_proj",
    )(x, mod, nw, w_r, w_m)


def _rwkv_programs(pr_ref, mu_ref, w0_ref, wup_ref, a0_ref, aup_ref, gup_ref, kk_ref, ka_ref,
                   rk_ref, gnw_ref, gnb_ref, hs_ref, o_ref, s_scr, prev_scr):
    C = RWKV_CHUNK
    N = RWKV_HEAD
    H = RWKV_HEADS
    rows = pr_ref.shape[0]
    nc = rows // C

    @pl.when(pl.program_id(1) == 0)
    def _():
        s_scr[...] = jnp.zeros_like(s_scr)
        prev_scr[...] = jnp.zeros_like(prev_scr)

    x = pr_ref[...]
    row1 = lax.broadcasted_iota(jnp.int32, x.shape, 0)
    prev = jnp.where(row1 == 0, prev_scr[...], pltpu.roll(x, 1, axis=0))
    prev_scr[...] = x[rows - 1:rows, :]
    u_all = x + (prev - x) * mu_ref[...]

    row = lax.broadcasted_iota(jnp.int32, (C, C), 0)
    col = lax.broadcasted_iota(jnp.int32, (C, C), 1)
    same = lambda blk: (row >> _log2(blk)) == (col >> _log2(blk))
    eye = (row == col).astype(F32)
    tril = (col <= row).astype(BF16)
    row2 = lax.broadcasted_iota(jnp.int32, (C, 2 * C), 0)
    col2 = lax.broadcasted_iota(jnp.int32, (C, 2 * C), 1) & (C - 1)
    strict2 = col2 < row2
    incl2 = col2 <= row2
    each = lambda f, *ls: [f(*t) for t in zip(*ls)]
    state = [s_scr[h] for h in range(H)]
    o1, o2, o3 = RWKV_WIDTH, 2 * RWKV_WIDTH, 3 * RWKV_WIDTH
    o4 = o3 + DECAY_LORA
    o5 = o4 + ICLR_LORA

    def solve(j):
        u = u_all[j * C:(j + 1) * C, :]
        r, k, v = u[:, :o1], u[:, o1:o2], u[:, o2:o3]
        wd, ad, gd = u[:, o3:o4], u[:, o4:o5], u[:, o5:]
        w_log = -_softplus(-(w0_ref[...] + _mm(jnp.tanh(wd), wup_ref[...]))) - 0.5
        lw = -jnp.exp(w_log)
        a = _sigmoid(a0_ref[...] + _mm(ad, aup_ref[...]))
        g = _mm(_sigmoid(gd), gup_ref[...])
        yield
        kk = k * kk_ref[...]
        k2 = k * (1.0 + (a - 1.0) * ka_ref[...])
        sq = kk * kk
        sq_hi = sq.astype(BF16)
        sq_lo = (sq - sq_hi.astype(F32)).astype(BF16)
        sums = (jnp.dot(sq_hi, hs_ref[...], preferred_element_type=F32)
                + jnp.dot(sq_lo, hs_ref[...], preferred_element_type=F32))
        kk = kk / jnp.maximum(jnp.sqrt(sums), 1e-12)
        rk = r * k2 * rk_ref[...]
        cum = _cumsum_rows(tril, lw)
        yield
        cl = cum[C - 1:C, :]
        p_inv = jnp.exp(-cum)
        p_rest = jnp.exp(cl - cum)
        b = kk * a
        ah_w = -kk * jnp.exp(cum - lw)
        rh_w = r * jnp.exp(cum)
        bt_w, kt_w = b * p_inv, k2 * p_inv
        b2_w, k2p_w = b * p_rest, k2 * p_rest
        p_end_w = jnp.exp(cl)
        cut = lambda t: [t[:, h * N:(h + 1) * N] for h in range(H)]
        ah, rh, bt, kt, b2, k2p, v_h, p_end, rk_h = (
            cut(t) for t in (ah_w, rh_w, bt_w, kt_w, b2_w, k2p_w, v, p_end_w, rk))
        yield
        ar = each(lambda a_, r_: jnp.concatenate([a_, r_], axis=0), ah, rh)
        bk = each(lambda b_, k_: jnp.concatenate([b_, k_], axis=0), bt, kt)
        x1 = each(_nt, ar, bk)
        l_both = each(lambda t: jnp.where(strict2, t[:C, :], 0.0), x1)
        m_both = each(lambda t: jnp.where(incl2, t[C:, :], 0.0), x1)
        l_ab = each(lambda t: t[:, :C], l_both)
        yield
        blk = NEUMANN_BLOCK
        n1 = each(lambda t: jnp.where(same(blk), t, 0.0), l_ab)
        xi = each(lambda t: eye + t, n1)
        nk = n1
        power = 2
        while power < blk:
            nk = each(lambda t: _mm(t, t), nk)
            xi = each(lambda x_, t: x_ + _mm(x_, t), xi, nk)
            power *= 2
            yield
        while blk < C:
            pair = same(2 * blk) & jnp.logical_not(same(blk))
            xe = each(lambda x_, t: _mm(x_, jnp.where(pair, t, 0.0)), xi, l_ab)
            xi = each(lambda x_, t: x_ + _mm(t, x_), xi, xe)
            blk *= 2
            yield
        zero = jnp.zeros((C, N), F32)
        z = each(lambda l_, v_: _mm(l_, jnp.concatenate([zero, v_], axis=0)), l_both, v_h)
        a2 = each(_mm, xi, ah)
        u0 = each(_mm, xi, z)
        yield
        uv = each(lambda u_, v_: jnp.concatenate([u_, v_], axis=0), u0, v_h)
        r2 = each(lambda r_, m_, a_: r_ + _mm(m_[:, :C], a_), rh, m_both, a2)
        y0 = each(_mm, m_both, uv)
        gt = each(_tn, a2, b2)
        ht = each(lambda uv_, b_, k_: _tn(uv_, jnp.concatenate([b_, k_], axis=0)), uv, b2, k2p)
        yield
        y = []
        for h in range(H):
            y.append(_nt(r2[h], state[h]) + y0[h])
            state[h] = state[h] * p_end[h] + _mm(state[h], gt[h]) + ht[h]
        yield
        yc = each(lambda t: t - jnp.mean(t, axis=-1, keepdims=True), y)
        var = each(lambda t: jnp.mean(t * t, axis=-1, keepdims=True), yc)
        rk_sum = each(lambda t: jnp.sum(t, axis=-1, keepdims=True), rk_h)
        yield
        for h in range(H):
            sl = slice(h * N, (h + 1) * N)
            yn = yc[h] * lax.rsqrt(var[h] + GN_EPS) * gnw_ref[:, sl] + gnb_ref[:, sl]
            o_ref[j * C:(j + 1) * C, sl] = (yn + rk_sum[h] * v_h[h]) * g[:, sl]

    def finish():
        for h in range(H):
            s_scr[h] = state[h]

    return [solve(j) for j in range(nc)], finish


def _mlstm_programs(pm_ref, cw_ref, cb_ref, gb_ref, hn_ref, o_ref, ext_scr, c_scr, n_scr, m_scr):
    C = CHUNK
    G = MLSTM_GROUP_ROWS
    H = MLSTM_HEADS
    DK, DV = MLSTM_DK, MLSTM_DV
    QK2 = 2 * MLSTM_QK
    rows = pm_ref.shape[0]

    @pl.when(pl.program_id(1) == 0)
    def _():
        ext_scr[0:8, :] = jnp.zeros((8, QK2), F32)
        c_scr[...] = jnp.zeros_like(c_scr)
        n_scr[...] = jnp.zeros_like(n_scr)
        m_scr[...] = jnp.zeros_like(m_scr)

    ext_scr[8:8 + rows, :] = pm_ref[:, 0:QK2]
    rg = lax.broadcasted_iota(jnp.int32, (G, G), 0)
    cg = lax.broadcasted_iota(jnp.int32, (G, G), 1)
    in_chunk = ((cg <= rg) & ((rg >> _log2(C)) == (cg >> _log2(C)))).astype(BF16)
    lane = lax.broadcasted_iota(jnp.int32, (G, LANES), 1)
    ri = lax.broadcasted_iota(jnp.int32, (LANES, LANES), 0)
    ci = lax.broadcasted_iota(jnp.int32, (LANES, LANES), 1)
    eye = (ri == ci).astype(F32)
    row = lax.broadcasted_iota(jnp.int32, (C, C), 0)
    col = lax.broadcasted_iota(jnp.int32, (C, C), 1)
    incl = col <= row
    each = lambda f, *ls: [f(*t) for t in zip(*ls)]
    m_run = [m_scr[h][0:1, 0:1] for h in range(H)]
    c_run = [c_scr[h] for h in range(H)]
    n_run = [n_scr[h][0:1, 0:DK] for h in range(H)]

    def solve(p):
        r0 = p * G
        conv = cb_ref[...] + cw_ref[0:1, :] * ext_scr[5 + r0:5 + r0 + G, :]
        for i in range(1, CONV_WIDTH):
            conv = conv + cw_ref[i:i + 1, :] * ext_scr[5 + i + r0:5 + i + r0 + G, :]
        qk = conv * _sigmoid(conv)
        q = qk[:, :MLSTM_QK] * (DK ** -0.5)
        k = qk[:, MLSTM_QK:]
        v = pm_ref[r0:r0 + G, QK2:QK2 + MLSTM_WIDTH]
        og = pm_ref[r0:r0 + G, QK2 + MLSTM_WIDTH:MLSTM_MAIN]
        z = GATE_SOFTCAP * jnp.tanh((pm_ref[r0:r0 + G, MLSTM_MAIN:] + gb_ref[...]) / GATE_SOFTCAP)
        log_f = jnp.minimum(z, 0.0) - jnp.log1p(jnp.exp(-jnp.abs(z)))
        gates = jnp.where(lane < H, z, _cumsum_rows(in_chunk, log_f))
        gates_t = lax.dot_general(eye, gates, (((1,), (1,)), ((), ())),
                                  preferred_element_type=F32, precision=HIGHEST)
        yield
        units = [(j, h) for j in range(G // C) for h in range(H)]
        li_col = [gates[j * C:(j + 1) * C, h:h + 1] for j, h in units]
        b_col = [gates[j * C:(j + 1) * C, H + h:H + h + 1] for j, h in units]
        li_row = [gates_t[h:h + 1, j * C:(j + 1) * C] for j, h in units]
        b_row = [gates_t[H + h:H + h + 1, j * C:(j + 1) * C] for j, h in units]
        q_h = [q[j * C:(j + 1) * C, h * DK:(h + 1) * DK] for j, h in units]
        k_h = [k[j * C:(j + 1) * C, h * DK:(h + 1) * DK] for j, h in units]
        v_h = [v[j * C:(j + 1) * C, h * DV:(h + 1) * DV] for j, h in units]
        b_last = each(lambda bc: bc[C - 1:C, :], b_col)
        end_log = each(lambda bl, bc, lc: bl - bc + lc, b_last, b_col, li_col)
        end_max = each(lambda el: jnp.max(el, axis=0, keepdims=True), end_log)
        m, m_new = [], []
        for i, (j, h) in enumerate(units):
            m.append(m_run[h])
            m_run[h] = jnp.maximum(b_last[i] + m_run[h], end_max[i])
            m_new.append(m_run[h])
        yield
        qk_h = each(_nt, q_h, k_h)
        d_mat = each(lambda bc, br, lr: jnp.where(incl, bc - br + lr, -jnp.inf), b_col, b_row, li_row)
        inter = each(lambda bc, m_: bc + m_, b_col, m)
        m_t = each(lambda i_, d_: jnp.maximum(i_, jnp.max(d_, axis=-1, keepdims=True)), inter, d_mat)
        yield
        scores = each(lambda s_, d_, mt: s_ * jnp.exp(d_ - mt), qk_h, d_mat, m_t)
        w_inter = each(lambda i_, mt: jnp.exp(i_ - mt), inter, m_t)
        sv_h = each(_mm, scores, v_h)
        carry_w = each(lambda bl, m_, mn: jnp.exp(bl + m_ - mn), b_last, m, m_new)
        kw = each(lambda k_, el, mn: k_ * jnp.exp(el - mn), k_h, end_log, m_new)
        kv_h = each(_tn, kw, v_h)
        yield
        c_mat, n_vec = [], []
        for i, (j, h) in enumerate(units):
            c_mat.append(c_run[h])
            n_vec.append(n_run[h])
            c_run[h] = carry_w[i] * c_run[h] + kv_h[i]
            n_run[h] = carry_w[i] * n_run[h] + jnp.sum(kw[i], axis=0, keepdims=True)
        qc_h = each(_mm, q_h, c_mat)
        yield
        num = each(lambda s_, w_, c_: s_ + w_ * c_, sv_h, w_inter, qc_h)
        s_sum = each(lambda s_: jnp.sum(s_, axis=-1, keepdims=True), scores)
        qn = each(lambda q_, n_: jnp.sum(q_ * n_, axis=-1, keepdims=True), q_h, n_vec)
        den = each(lambda s_, w_, q_: s_ + w_ * q_, s_sum, w_inter, qn)
        hh = each(lambda n_, d_, mt: n_ / jnp.maximum(jnp.abs(d_), jnp.exp(-mt)), num, den, m_t)
        ms = each(lambda t: jnp.mean(t * t, axis=-1, keepdims=True), hh)
        yield
        for i, (j, h) in enumerate(units):
            rs = slice(r0 + j * C, r0 + (j + 1) * C)
            hn = hh[i] * lax.rsqrt(ms[i] + EPS) * hn_ref[:, h * DV:(h + 1) * DV]
            o_ref[rs, h * DV:(h + 1) * DV] = hn * _sigmoid(og[j * C:(j + 1) * C, h * DV:(h + 1) * DV])

    def finish():
        ext_scr[0:8, :] = ext_scr[rows:rows + 8, :]
        for h in range(H):
            c_scr[h] = c_run[h]
            n_scr[h] = jnp.broadcast_to(jnp.concatenate([n_run[h], jnp.zeros((1, LANES - DK), F32)], axis=1),
                                        (8, LANES))
            m_scr[h] = jnp.broadcast_to(m_run[h], (8, LANES))

    return [solve(p) for p in range(rows // G)], finish


def _mixers_kernel(pr_ref, mu_ref, w0_ref, wup_ref, a0_ref, aup_ref, gup_ref, kk_ref, ka_ref, rk_ref, gnw_ref,
                   gnb_ref, hs_ref, pm_ref, cw_ref, cb_ref, gb_ref, hn_ref, yr_ref, ym_ref,
                   s_scr, prev_scr, ext_scr, c_scr, n_scr, m_scr):
    r_progs, r_finish = _rwkv_programs(pr_ref, mu_ref, w0_ref, wup_ref, a0_ref, aup_ref, gup_ref, kk_ref, ka_ref,
                                       rk_ref, gnw_ref, gnb_ref, hs_ref, yr_ref, s_scr, prev_scr)
    m_progs, m_finish = _mlstm_programs(pm_ref, cw_ref, cb_ref, gb_ref, hn_ref, ym_ref, ext_scr, c_scr, n_scr,
                                        m_scr)
    pending = [g for pair in zip(r_progs, m_progs) for g in pair]
    active = []
    tick = 0
    while pending or active:
        if pending and tick % MIX_SKEW == 0:
            active.append(pending.pop(0))
        for gen in list(active):
            try:
                next(gen)
            except StopIteration:
                active.remove(gen)
        tick += 1
    r_finish()
    m_finish()


def _mixers(pr, pm, mu, w0, w_up, a0, a_up, g_up, k_k, k_a, r_k, gn_w, gn_b, conv_w, conv_b, i_b, f_b, hn_w, rows):
    bsz, seq, cols = pr.shape
    mcols = pm.shape[2]
    row = lambda p: p.reshape(1, -1)
    full = lambda a: pl.BlockSpec(a.shape, lambda b, c: (0,) * a.ndim)
    tok = lambda n: pl.BlockSpec((None, rows, n), lambda b, c: (b, c, 0))
    lane_head = jnp.arange(RWKV_WIDTH) // RWKV_HEAD
    same_head = (lane_head[:, None] == lane_head[None, :]).astype(BF16)
    gate_b = jnp.zeros((1, LANES), F32).at[0, :MLSTM_HEADS].set(i_b).at[0, MLSTM_HEADS:2 * MLSTM_HEADS].set(f_b)
    r_params = [row(mu), row(w0), w_up, row(a0), a_up, g_up, row(k_k), row(k_a), row(r_k), row(gn_w), row(gn_b),
                same_head]
    m_params = [conv_w, row(conv_b), gate_b, row(hn_w)]
    return pl.pallas_call(
        _mixers_kernel,
        grid=(bsz, seq // rows),
        in_specs=[tok(cols)] + [full(a) for a in r_params] + [tok(mcols)] + [full(a) for a in m_params],
        out_specs=[tok(RWKV_WIDTH), tok(MLSTM_WIDTH)],
        out_shape=[jax.ShapeDtypeStruct((bsz, seq, RWKV_WIDTH), F32),
                   jax.ShapeDtypeStruct((bsz, seq, MLSTM_WIDTH), F32)],
        scratch_shapes=[pltpu.VMEM((RWKV_HEADS, RWKV_HEAD, RWKV_HEAD), F32),
                        pltpu.VMEM((1, cols), F32),
                        pltpu.VMEM((rows + 8, 2 * MLSTM_QK), F32),
                        pltpu.VMEM((MLSTM_HEADS, MLSTM_DK, MLSTM_DV), F32),
                        pltpu.VMEM((MLSTM_HEADS, 8, LANES), F32),
                        pltpu.VMEM((MLSTM_HEADS, 8, LANES), F32)],
        compiler_params=_params("arbitrary", "arbitrary"),
        name="mixers",
    )(pr, *r_params, pm, *m_params)


def _mix_out_kernel(x_ref, yr_ref, ym_ref, mod_ref, nw_ref, wor_ref, wom_ref, wr_ref, br_ref, x1_ref, hx_ref):
    d = x_ref.shape[-1]
    y = (jnp.dot(yr_ref[...].astype(BF16), wor_ref[...], preferred_element_type=F32)
         + jnp.dot(ym_ref[...].astype(BF16), wom_ref[...], preferred_element_type=F32))
    x1 = x_ref[...] + mod_ref[2:3, :] * y
    x1_ref[...] = x1
    h = _norm_mod(x1, nw_ref[...], mod_ref[3:4, :], mod_ref[4:5, :])
    h_hi = h.astype(BF16)
    hx_ref[:, :d] = h_hi

    h_lo = (h - h_hi.astype(F32)).astype(BF16)
    w = wr_ref[...]
    w_hi = w.astype(BF16)
    w_lo = (w - w_hi.astype(F32)).astype(BF16)
    hw = jnp.dot(h_hi, jnp.concatenate([w_hi, w_lo], axis=1), preferred_element_type=F32)
    logits = hw[:, :LANES] + hw[:, LANES:] + jnp.dot(h_lo, w_hi, preferred_element_type=F32) + br_ref[...]
    lane = lax.broadcasted_iota(jnp.int32, logits.shape, 1)
    neg = -jnp.inf
    el = logits
    gl = jnp.where((lane >= N_EXPERTS) & (lane < N_EXPERTS + N_GROUPS), logits, neg)
    g_max = jnp.max(gl, axis=-1, keepdims=True)
    g_idx = jnp.min(jnp.where(gl == g_max, lane, LANES), axis=-1, keepdims=True) - N_EXPERTS
    g_w = 1.0 / jnp.sum(jnp.exp(gl - g_max), axis=-1, keepdims=True)
    in_group = ((lane >> 3) == g_idx) & (lane < N_EXPERTS)
    sel = jnp.where(in_group, el, neg)
    v1 = jnp.max(sel, axis=-1, keepdims=True)
    i1 = jnp.min(jnp.where(sel == v1, lane, LANES), axis=-1, keepdims=True)
    rest = jnp.where(lane == i1, neg, sel)
    v2 = jnp.max(rest, axis=-1, keepdims=True)
    i2 = jnp.min(jnp.where(rest == v2, lane, LANES), axis=-1, keepdims=True)
    e2 = jnp.exp(v2 - v1)
    w1 = 1.0 / (1.0 + e2)
    w2 = e2 / (1.0 + e2)
    cw = jnp.where(lane == i1, w1 * g_w, jnp.where(lane == i2, w2 * g_w, 0.0))
    hi = cw.astype(BF16).astype(F32)
    mid = (cw - hi).astype(BF16).astype(F32)
    lo = (cw - hi - mid).astype(BF16).astype(F32)
    flag = jnp.where(hi != 0.0, 1.0, 0.0)
    pack = (hi + pltpu.roll(mid, N_EXPERTS, axis=1) + pltpu.roll(lo, 2 * N_EXPERTS, axis=1)
            + pltpu.roll(flag, 3 * N_EXPERTS, axis=1))
    hx_ref[:, d:] = pack.astype(BF16)


def _mix_out(x, y_r, y_m, mod, nw, wo_r, wo_m, w_route, b_route, tm):
    bsz, seq, d = x.shape
    tok = lambda n: pl.BlockSpec((None, tm, n), lambda b, i: (b, i, 0))
    full = lambda a: pl.BlockSpec(a.shape, lambda b, i: (0, 0))
    return pl.pallas_call(
        _mix_out_kernel,
        grid=(bsz, seq // tm),
        in_specs=[tok(d), tok(RWKV_WIDTH), tok(MLSTM_WIDTH),
                  pl.BlockSpec((None, 6, d), lambda b, i: (b, 0, 0)),
                  full(nw), full(wo_r), full(wo_m), full(w_route), full(b_route)],
        out_specs=[tok(d), tok(d + LANES)],
        out_shape=[jax.ShapeDtypeStruct((bsz, seq, d), F32),
                   jax.ShapeDtypeStruct((bsz, seq, d + LANES), BF16)],
        compiler_params=_params("arbitrary", "arbitrary"),
        name="mix_out",
    )(x, y_r, y_m, mod, nw, wo_r, wo_m, w_route, b_route)


ROW_CHUNK = 16
SORT_TILE = 512
EXPERT_TILE = 512
NO_SLOT = 1e9


def _moe_plan(flags, tms, ts):
    t, ne = flags.shape
    nt = t // tms
    i32 = jnp.int32
    tri = lambda n: (jnp.arange(n)[:, None] < jnp.arange(n)[None, :]).astype(F32)
    before = lambda a, b: jnp.dot(a, b, precision=HIGHEST).astype(i32)
    cnt = flags.reshape(nt, tms, ne).astype(i32).sum(axis=1)
    pc = (cnt + ROW_CHUNK - 1) // ROW_CHUNK * ROW_CHUNK
    pcf = pc.astype(F32)
    loc_start = before(pcf, tri(ne))
    len_e = pc.sum(axis=0)
    seg_e = (len_e + ts - 1) // ts * ts
    e_start = before(seg_e.astype(F32)[None, :], tri(ne))[0]
    e_end = e_start + seg_e
    dst_start = e_start[None, :] + before(tri(nt).T, pcf)
    n_tiles_max = _moe_max_rows(t, ne, tms, ts) // ts
    tile_row = jnp.arange(n_tiles_max, dtype=i32) * ts
    tile_expert = jnp.minimum((tile_row[:, None] >= e_end[None, :]).astype(i32).sum(axis=1), ne - 1)
    max_chunks = _local_rows(tms) // ROW_CHUNK
    chunk_row = jnp.arange(max_chunks, dtype=i32)[None, :, None] * ROW_CHUNK
    run_end = (loc_start + pc)[:, None, :]
    run_of = jnp.minimum((chunk_row >= run_end).astype(i32).sum(axis=2), ne - 1)
    chunk_dst = jnp.take_along_axis(dst_start - loc_start, run_of, axis=1) + chunk_row[:, :, 0]
    return dict(
        chunk_dst=chunk_dst.reshape(-1).astype(i32), tile_chunks=(pc.sum(axis=1) // ROW_CHUNK).astype(i32),
        tail_start=(e_start + len_e).astype(i32),
        tail_chunks=((seg_e - len_e) // ROW_CHUNK).astype(i32),
        n_used=(e_end[-1:] // ts).astype(i32), tile_expert=tile_expert,
        loc_start_vec=jnp.pad(loc_start.astype(F32), ((0, 0), (0, LANES - ne))).reshape(nt, 1, LANES))


def _moe_max_rows(t, ne, tms, ts):
    rows = 2 * t + (t // tms) * ne * (ROW_CHUNK - 1) + ne * (ts - 1)
    return (rows + ts - 1) // ts * ts


def _local_rows(tms):
    return 2 * tms + N_EXPERTS * ROW_CHUNK


def _tile_slots(blk, loc_start_vec):
    tms = blk.shape[0]
    lane = lax.broadcasted_iota(jnp.int32, blk.shape, 1)
    m = jnp.where(lane < N_EXPERTS, pltpu.roll(blk, N_EXPERTS, axis=1), 0.0)
    row = lax.broadcasted_iota(jnp.int32, (tms, tms), 0)
    col = lax.broadcasted_iota(jnp.int32, (tms, tms), 1)
    before = jnp.dot((col < row).astype(BF16), m.astype(BF16), preferred_element_type=F32)
    slot = loc_start_vec + before
    slot_a = jnp.min(jnp.where(m > 0.0, slot, NO_SLOT), axis=-1, keepdims=True)
    slot_b = jnp.max(jnp.where(m > 0.0, slot, -NO_SLOT), axis=-1, keepdims=True)
    return slot_a, slot_b


def _moe_sort_kernel(cd_ref, tc_ref, tls_ref, tlc_ref, nu_ref, hx_ref, lsv_ref, xo_ref, xs_scr, z_scr,
                     sem, zsem):
    i = pl.program_id(0)
    tms, width = hx_ref.shape
    loc = xs_scr.shape[0]
    d = width - LANES
    slot_a, slot_b = _tile_slots(hx_ref[:, d:].astype(F32), lsv_ref[...])
    lane = lax.broadcasted_iota(jnp.int32, (tms, LANES), 1)
    packed = jnp.where(lane == 0, slot_a, jnp.where(lane == 1, slot_b, 0.0))
    ri = lax.broadcasted_iota(jnp.int32, (LANES, LANES), 0)
    ci = lax.broadcasted_iota(jnp.int32, (LANES, LANES), 1)
    slots_t = lax.dot_general((ri == ci).astype(F32), packed, (((1,), (1,)), ((), ())),
                              preferred_element_type=F32, precision=HIGHEST)
    srow = lax.broadcasted_iota(jnp.int32, (loc, tms), 0).astype(F32)
    perm = jnp.where((srow == slots_t[0:1, :]) | (srow == slots_t[1:2, :]), 1.0, 0.0).astype(BF16)
    xs_scr[...] = jnp.dot(perm, hx_ref[...], preferred_element_type=F32).astype(BF16)

    def copy(src, src_row, dst_row):
        return pltpu.make_async_copy(src.at[pl.ds(pl.multiple_of(src_row, ROW_CHUNK), ROW_CHUNK)],
                                     xo_ref.at[pl.ds(pl.multiple_of(dst_row, ROW_CHUNK), ROW_CHUNK)], sem)

    max_chunks = loc // ROW_CHUNK
    n_mine = tc_ref[i]

    def run_body(j, carry):
        copy(xs_scr, j * ROW_CHUNK, cd_ref[i * max_chunks + j]).start()
        return carry

    lax.fori_loop(0, n_mine, run_body, 0)

    def wait_body(j, carry):
        copy(xs_scr, 0, 0).wait()
        return carry

    @pl.when(i == 0)
    def _():
        z_scr[...] = jnp.zeros_like(z_scr)
        ts = z_scr.shape[0]
        n_tail = jnp.int32(0)
        for e in range(N_EXPERTS):
            n = tlc_ref[e]
            t0 = tls_ref[e]

            def tail_body(j, carry, t0=t0):
                copy(z_scr, 0, t0 + j * ROW_CHUNK).start()
                return carry

            lax.fori_loop(0, n, tail_body, 0)
            n_tail = n_tail + n

        def tile_copy(tile):
            return pltpu.make_async_copy(z_scr, xo_ref.at[pl.ds(pl.multiple_of(tile * ts, ts), ts)], zsem)

        n_spare = xo_ref.shape[0] // ts - nu_ref[0]

        def spare_body(j, carry):
            tile_copy(nu_ref[0] + j).start()
            return carry

        lax.fori_loop(0, n_spare, spare_body, 0)
        lax.fori_loop(0, n_tail, wait_body, 0)

        def spare_wait_body(j, carry):
            tile_copy(0).wait()
            return carry

        lax.fori_loop(0, n_spare, spare_wait_body, 0)

    lax.fori_loop(0, n_mine, wait_body, 0)


def _moe_sort(hx, plan, tms, ts):
    t, width = hx.shape
    nt = t // tms
    rows = _moe_max_rows(t, N_EXPERTS, tms, ts)
    return pl.pallas_call(
        _moe_sort_kernel,
        grid_spec=pltpu.PrefetchScalarGridSpec(
            num_scalar_prefetch=5,
            grid=(nt,),
            in_specs=[pl.BlockSpec((tms, width), lambda i, *_: (i, 0)),
                      pl.BlockSpec((None, 1, LANES), lambda i, *_: (i, 0, 0))],
            out_specs=pl.BlockSpec(memory_space=pl.ANY),
            scratch_shapes=[pltpu.VMEM((_local_rows(tms), width), BF16),
                            pltpu.VMEM((ts, width), BF16),
                            pltpu.SemaphoreType.DMA(()),
                            pltpu.SemaphoreType.DMA(())]),
        out_shape=jax.ShapeDtypeStruct((rows, width), BF16),
        compiler_params=_params("arbitrary"),
        name="moe_sort",
    )(plan["chunk_dst"], plan["tile_chunks"], plan["tail_start"], plan["tail_chunks"], plan["n_used"],
      hx, plan["loc_start_vec"])


def _moe_experts_kernel(te_ref, nu_ref, xs_ref, wg_ref, wu_ref, wd_ref, o_ref):
    j = pl.program_id(0)

    @pl.when(j < nu_ref[0])
    def _():
        d = xs_ref.shape[1] - LANES
        e = te_ref[j]
        x = xs_ref[:, :d]
        wblk = xs_ref[:, d:].astype(F32)
        lane = lax.broadcasted_iota(jnp.int32, wblk.shape, 1)
        mine = (lane == e) | (lane == e + N_EXPERTS) | (lane == e + 2 * N_EXPERTS)
        cw = jnp.sum(jnp.where(mine, wblk, 0.0), axis=-1, keepdims=True)
        hg = jnp.dot(x, wg_ref[...], preferred_element_type=F32)
        hu = jnp.dot(x, wu_ref[...], preferred_element_type=F32)
        act = hg * _sigmoid(hg) * hu * cw
        o_ref[...] = jnp.dot(act.astype(BF16), wd_ref[...], preferred_element_type=F32).astype(BF16)

    @pl.when(j >= nu_ref[0])
    def _():
        o_ref[...] = jnp.zeros_like(o_ref)


def _moe_experts(xs, plan, w_gate, w_up, w_down, ts):
    rows, width = xs.shape
    ne, d, f = w_gate.shape
    tile = lambda j, te, nu: (jnp.minimum(j, nu[0] - 1), 0)
    return pl.pallas_call(
        _moe_experts_kernel,
        grid_spec=pltpu.PrefetchScalarGridSpec(
            num_scalar_prefetch=2,
            grid=(rows // ts,),
            in_specs=[pl.BlockSpec((ts, width), tile),
                      pl.BlockSpec((None, d, f), lambda j, te, nu: (te[j], 0, 0)),
                      pl.BlockSpec((None, d, f), lambda j, te, nu: (te[j], 0, 0)),
                      pl.BlockSpec((None, f, d), lambda j, te, nu: (te[j], 0, 0))],
            out_specs=pl.BlockSpec((ts, d), lambda j, te, nu: (j, 0))),
        out_shape=jax.ShapeDtypeStruct((rows, d), BF16),
        compiler_params=_params("arbitrary"),
        name="moe_experts",
    )(plan["tile_expert"], plan["n_used"], xs, w_gate, w_up, w_down)


def _moe_combine_kernel(cd_ref, tc_ref, wblk_ref, lsv_ref, x1_ref, mod_ref, fw_ref, ys_hbm,
                        o_ref, ys_scr, sems):
    i = pl.program_id(0)
    nt = pl.num_programs(0)
    tms = x1_ref.shape[0]
    loc = ys_scr.shape[1]

    def copy(slot, src_row, dst_row):
        return pltpu.make_async_copy(
            ys_hbm.at[pl.ds(pl.multiple_of(src_row, ROW_CHUNK), ROW_CHUNK)],
            ys_scr.at[slot, pl.ds(pl.multiple_of(dst_row, ROW_CHUNK), ROW_CHUNK)], sems.at[slot])

    max_chunks = loc // ROW_CHUNK

    def fetch(tile, slot):
        def run_body(j, carry):
            copy(slot, cd_ref[tile * max_chunks + j], j * ROW_CHUNK).start()
            return carry

        lax.fori_loop(0, tc_ref[tile], run_body, 0)

    @pl.when(i == 0)
    def _():
        ys_scr[...] = jnp.zeros_like(ys_scr)
        fetch(0, 0)

    slot = i % 2

    @pl.when(i + 1 < nt)
    def _():
        fetch(i + 1, 1 - slot)

    slot_a, slot_b = _tile_slots(wblk_ref[...].astype(F32), lsv_ref[...])
    scol = lax.broadcasted_iota(jnp.int32, (tms, loc), 1).astype(F32)
    perm = jnp.where((scol == slot_a) | (scol == slot_b), 1.0, 0.0).astype(BF16)

    def wait_body(j, carry):
        copy(slot, 0, 0).wait()
        return carry

    lax.fori_loop(0, tc_ref[i], wait_body, 0)
    y = jnp.dot(perm, ys_scr[slot], preferred_element_type=F32)
    x2 = x1_ref[...] + mod_ref[5:6, :] * y
    o_ref[...] = x2 * lax.rsqrt(jnp.mean(x2 * x2, axis=-1, keepdims=True) + EPS) * fw_ref[...]


def _moe_combine(hx, ys, x1, mod, fw, plan, tms, tiles_per_seq):
    t, d = x1.shape
    nt = t // tms
    return pl.pallas_call(
        _moe_combine_kernel,
        grid_spec=pltpu.PrefetchScalarGridSpec(
            num_scalar_prefetch=2,
            grid=(nt,),
            in_specs=[pl.BlockSpec((tms, LANES), lambda i, *_: (i, d // LANES)),
                      pl.BlockSpec((None, 1, LANES), lambda i, *_: (i, 0, 0)),
                      pl.BlockSpec((tms, d), lambda i, *_: (i, 0)),
                      pl.BlockSpec((None, 6, d), lambda i, *_: (i // tiles_per_seq, 0, 0)),
                      pl.BlockSpec((1, d), lambda i, *_: (0, 0)),
                      pl.BlockSpec(memory_space=pl.ANY)],
            out_specs=pl.BlockSpec((tms, d), lambda i, *_: (i, 0)),
            scratch_shapes=[pltpu.VMEM((2, _local_rows(tms), d), BF16),
                            pltpu.SemaphoreType.DMA((2,))]),
        out_shape=jax.ShapeDtypeStruct((t, d), F32),
        compiler_params=_params("arbitrary"),
        name="moe_combine",
    )(plan["chunk_dst"], plan["tile_chunks"], hx, plan["loc_start_vec"], x1, mod, fw, ys)


def _pad_cols(w, n):
    return jnp.pad(w, ((0, 0), (0, n - w.shape[1])))


def kernel(x, c, ada_w, ada_b, mix_norm_w, w_in, rwkv_mu, rwkv_w0, rwkv_w_up, rwkv_a0, rwkv_a_up, rwkv_g_up, rwkv_k_k, rwkv_k_a, rwkv_r_k, rwkv_gn_w, rwkv_gn_b, mlstm_conv_w, mlstm_conv_b, mlstm_i_b, mlstm_f_b, mlstm_hn_w, w_out, ffn_norm_w, moe_w_group, moe_b_group, moe_w_router, moe_b_router, moe_w_gate, moe_w_up, moe_w_down, final_norm_w):
    bsz, seq, d = x.shape
    assert ada_w.shape[0] == 1, "the fused final norm assumes a single layer"
    assert seq % CHUNK == 0
    l = 0
    tm = min(512, seq)
    tms = min(SORT_TILE, seq)
    mod = _ada(c, ada_w[l], ada_b[l]).reshape(bsz, 6, d)
    w_r = w_in[l][:, :RWKV_COLS].astype(BF16)
    w_m = _pad_cols(w_in[l][:, RWKV_COLS:], MLSTM_PAD_COLS).astype(BF16)
    pr, pm = _in_proj(x, mod, mix_norm_w[l].reshape(1, d), w_r, w_m, tm)
    y_r, y_m = _mixers(pr, pm, rwkv_mu[l], rwkv_w0[l], rwkv_w_up[l], rwkv_a0[l], rwkv_a_up[l], rwkv_g_up[l],
                       rwkv_k_k[l], rwkv_k_a[l], rwkv_r_k[l], rwkv_gn_w[l], rwkv_gn_b[l], mlstm_conv_w[l],
                       mlstm_conv_b[l], mlstm_i_b[l], mlstm_f_b[l], mlstm_hn_w[l], min(MIX_STEP_ROWS, seq))
    wo = w_out[l].astype(BF16)
    w_route = _pad_cols(jnp.concatenate([moe_w_router[l], moe_w_group[l]], axis=1), LANES)
    b_route = _pad_cols(jnp.concatenate([moe_b_router[l], moe_b_group[l]]).reshape(1, -1), LANES)
    x1, hx = _mix_out(
        x, y_r, y_m, mod, ffn_norm_w[l].reshape(1, d), wo[:RWKV_WIDTH], wo[RWKV_WIDTH:], w_route, b_route, tm)
    hx = hx.reshape(bsz * seq, d + LANES)
    flags = hx[:, d + 3 * N_EXPERTS:] != 0
    plan = _moe_plan(flags, tms, EXPERT_TILE)
    xs = _moe_sort(hx, plan, tms, EXPERT_TILE)
    ys = _moe_experts(xs, plan, moe_w_gate[l].astype(BF16), moe_w_up[l].astype(BF16),
                      moe_w_down[l].astype(BF16), EXPERT_TILE)
    out = _moe_combine(hx, ys, x1.reshape(bsz * seq, d), mod, final_norm_w.reshape(1, d), plan, tms, seq // tms)
    return out.reshape(bsz, seq, d)
```

```python
import jax
import jax.numpy as jnp
from jax import lax
from jax.experimental import pallas as pl
from jax.experimental.pallas import tpu as pltpu

F32 = jnp.float32
BF16 = jnp.bfloat16
HIGHEST = lax.Precision.HIGHEST

CHUNK = 64
RWKV_CHUNK = 128
NEUMANN_BLOCK = 16
MIX_STEP_ROWS = 512
MLSTM_GROUP_ROWS = 128
MIX_SKEW = 2
EPS = 1e-6
RWKV_HEAD = 64
RWKV_HEADS = 8
RWKV_WIDTH = RWKV_HEADS * RWKV_HEAD
DECAY_LORA = 64
ICLR_LORA = 64
GATE_LORA = 128
GN_EPS = 64e-5
MLSTM_HEADS = 4
MLSTM_DK = 64
MLSTM_DV = 128
MLSTM_QK = MLSTM_HEADS * MLSTM_DK
MLSTM_WIDTH = MLSTM_HEADS * MLSTM_DV
CONV_WIDTH = 4
GATE_SOFTCAP = 15.0
RWKV_COLS = 3 * RWKV_WIDTH + DECAY_LORA + ICLR_LORA + GATE_LORA
N_GROUPS = 4
EXPERTS_PER_GROUP = 8
N_EXPERTS = N_GROUPS * EXPERTS_PER_GROUP
LANES = 128
MLSTM_MAIN = 2 * MLSTM_QK + 2 * MLSTM_WIDTH
MLSTM_PAD_COLS = MLSTM_MAIN + LANES
VMEM_LIMIT = 56 * 1024 * 1024


def _mm(a, b):
    return jnp.dot(a.astype(BF16), b.astype(BF16), preferred_element_type=F32)


def _nt(a, b):
    return lax.dot_general(a.astype(BF16), b.astype(BF16), (((1,), (1,)), ((), ())),
                           preferred_element_type=F32)


def _tn(a, b):
    return lax.dot_general(a.astype(BF16), b.astype(BF16), (((0,), (0,)), ((), ())),
                           preferred_element_type=F32)


def _mm_f32(a, b):
    return jnp.dot(a, b, preferred_element_type=F32, precision=HIGHEST)


def _cumsum_rows(tril, x):
    hi = x.astype(BF16)
    r1 = x - hi.astype(F32)
    mid = r1.astype(BF16)
    lo = (r1 - mid.astype(F32)).astype(BF16)
    n = x.shape[1]
    out = jnp.dot(tril, jnp.concatenate([hi, mid, lo], axis=1), preferred_element_type=F32)
    return out[:, :n] + out[:, n:2 * n] + out[:, 2 * n:]


def _sigmoid(z):
    return 1.0 / (1.0 + jnp.exp(-z))


def _softplus(z):
    return jnp.maximum(z, 0.0) + jnp.log1p(jnp.exp(-jnp.abs(z)))


def _log2(n):
    assert n & (n - 1) == 0
    return n.bit_length() - 1


def _params(*sem):
    return pltpu.CompilerParams(dimension_semantics=sem, vmem_limit_bytes=VMEM_LIMIT)


def _ada_kernel(c_ref, w_ref, b_ref, o_ref):
    c = c_ref[...]
    o_ref[...] = _mm_f32(c * _sigmoid(c), w_ref[...]) + b_ref[...]


def _ada(c, w, b):
    bsz, d = c.shape
    n = w.shape[1]
    return pl.pallas_call(
        _ada_kernel,
        grid=(n // d,),
        in_specs=[pl.BlockSpec((bsz, d), lambda j: (0, 0)),
                  pl.BlockSpec((d, d), lambda j: (0, j)),
                  pl.BlockSpec((1, d), lambda j: (0, j))],
        out_specs=pl.BlockSpec((bsz, d), lambda j: (0, j)),
        out_shape=jax.ShapeDtypeStruct((bsz, n), F32),
        compiler_params=_params("arbitrary"),
        name="ada",
    )(c, w, b.reshape(1, n))


def _norm_mod(x, nw, shift, scale):
    y = x * lax.rsqrt(jnp.mean(x * x, axis=-1, keepdims=True) + EPS) * nw
    return y * (1.0 + scale) + shift


def _in_proj_kernel(x_ref, mod_ref, nw_ref, wr_ref, wm_ref, pr_ref, pm_ref):
    h = _norm_mod(x_ref[...], nw_ref[...], mod_ref[0:1, :], mod_ref[1:2, :]).astype(BF16)
    pr_ref[...] = jnp.dot(h, wr_ref[...], preferred_element_type=F32)
    pm_ref[...] = jnp.dot(h, wm_ref[...], preferred_element_type=F32)


def _in_proj(x, mod, nw, w_r, w_m, tm):
    bsz, seq, d = x.shape
    nr, nm = w_r.shape[1], w_m.shape[1]
    return pl.pallas_call(
        _in_proj_kernel,
        grid=(bsz, seq // tm),
        in_specs=[pl.BlockSpec((None, tm, d), lambda b, i: (b, i, 0)),
                  pl.BlockSpec((None, 6, d), lambda b, i: (b, 0, 0)),
                  pl.BlockSpec((1, d), lambda b, i: (0, 0)),
                  pl.BlockSpec((d, nr), lambda b, i: (0, 0)),
                  pl.BlockSpec((d, nm), lambda b, i: (0, 0))],
        out_specs=[pl.BlockSpec((None, tm, nr), lambda b, i: (b, i, 0)),
                   pl.BlockSpec((None, tm, nm), lambda b, i: (b, i, 0))],
        out_shape=[jax.ShapeDtypeStruct((bsz, seq, nr), F32),
                   jax.ShapeDtypeStruct((bsz, seq, nm), F32)],
        compiler_params=_params("arbitrary", "arbitrary"),
        name="in_proj",
    )(x, mod, nw, w_r, w_m)


def _rwkv_programs(pr_ref, mu_ref, w0_ref, wup_ref, a0_ref, aup_ref, gup_ref, kk_ref, ka_ref,
                   rk_ref, gnw_ref, gnb_ref, hs_ref, o_ref, s_scr, prev_scr):
    C = RWKV_CHUNK
    N = RWKV_HEAD
    H = RWKV_HEADS
    rows = pr_ref.shape[0]
    nc = rows // C

    @pl.when(pl.program_id(1) == 0)
    def _():
        s_scr[...] = jnp.zeros_like(s_scr)
        prev_scr[...] = jnp.zeros_like(prev_scr)

    x = pr_ref[...]
    row1 = lax.broadcasted_iota(jnp.int32, x.shape, 0)
    prev = jnp.where(row1 == 0, prev_scr[...], pltpu.roll(x, 1, axis=0))
    prev_scr[...] = x[rows - 1:rows, :]
    u_all = x + (prev - x) * mu_ref[...]

    row = lax.broadcasted_iota(jnp.int32, (C, C), 0)
    col = lax.broadcasted_iota(jnp.int32, (C, C), 1)
    same = lambda blk: (row >> _log2(blk)) == (col >> _log2(blk))
    eye = (row == col).astype(F32)
    tril = (col <= row).astype(BF16)
    row2 = lax.broadcasted_iota(jnp.int32, (C, 2 * C), 0)
    col2 = lax.broadcasted_iota(jnp.int32, (C, 2 * C), 1) & (C - 1)
    strict2 = col2 < row2
    incl2 = col2 <= row2
    each = lambda f, *ls: [f(*t) for t in zip(*ls)]
    state = [s_scr[h] for h in range(H)]
    o1, o2, o3 = RWKV_WIDTH, 2 * RWKV_WIDTH, 3 * RWKV_WIDTH
    o4 = o3 + DECAY_LORA
    o5 = o4 + ICLR_LORA

    def solve(j):
        u = u_all[j * C:(j + 1) * C, :]
        r, k, v = u[:, :o1], u[:, o1:o2], u[:, o2:o3]
        wd, ad, gd = u[:, o3:o4], u[:, o4:o5], u[:, o5:]
        w_log = -_softplus(-(w0_ref[...] + _mm(jnp.tanh(wd), wup_ref[...]))) - 0.5
        lw = -jnp.exp(w_log)
        a = _sigmoid(a0_ref[...] + _mm(ad, aup_ref[...]))
        g = _mm(_sigmoid(gd), gup_ref[...])
        yield
        kk = k * kk_ref[...]
        k2 = k * (1.0 + (a - 1.0) * ka_ref[...])
        sq = kk * kk
        sq_hi = sq.astype(BF16)
        sq_lo = (sq - sq_hi.astype(F32)).astype(BF16)
        sums = (jnp.dot(sq_hi, hs_ref[...], preferred_element_type=F32)
                + jnp.dot(sq_lo, hs_ref[...], preferred_element_type=F32))
        kk = kk / jnp.maximum(jnp.sqrt(sums), 1e-12)
        rk = r * k2 * rk_ref[...]
        cum = _cumsum_rows(tril, lw)
        yield
        cl = cum[C - 1:C, :]
        p_inv = jnp.exp(-cum)
        p_rest = jnp.exp(cl - cum)
        b = kk * a
        ah_w = -kk * jnp.exp(cum - lw)
        rh_w = r * jnp.exp(cum)
        bt_w, kt_w = b * p_inv, k2 * p_inv
        b2_w, k2p_w = b * p_rest, k2 * p_rest
        p_end_w = jnp.exp(cl)
        cut = lambda t: [t[:, h * N:(h + 1) * N] for h in range(H)]
        ah, rh, bt, kt, b2, k2p, v_h, p_end, rk_h = (
            cut(t) for t in (ah_w, rh_w, bt_w, kt_w, b2_w, k2p_w, v, p_end_w, rk))
        yield
        ar = each(lambda a_, r_: jnp.concatenate([a_, r_], axis=0), ah, rh)
        bk = each(lambda b_, k_: jnp.concatenate([b_, k_], axis=0), bt, kt)
        x1 = each(_nt, ar, bk)
        l_both = each(lambda t: jnp.where(strict2, t[:C, :], 0.0), x1)
        m_both = each(lambda t: jnp.where(incl2, t[C:, :], 0.0), x1)
        l_ab = each(lambda t: t[:, :C], l_both)
        yield
        blk = NEUMANN_BLOCK
        n1 = each(lambda t: jnp.where(same(blk), t, 0.0), l_ab)
        xi = each(lambda t: eye + t, n1)
        nk = n1
        power = 2
        while power < blk:
            nk = each(lambda t: _mm(t, t), nk)
            xi = each(lambda x_, t: x_ + _mm(x_, t), xi, nk)
            power *= 2
            yield
        while blk < C:
            pair = same(2 * blk) & jnp.logical_not(same(blk))
            xe = each(lambda x_, t: _mm(x_, jnp.where(pair, t, 0.0)), xi, l_ab)
            xi = each(lambda x_, t: x_ + _mm(t, x_), xi, xe)
            blk *= 2
            yield
        zero = jnp.zeros((C, N), F32)
        z = each(lambda l_, v_: _mm(l_, jnp.concatenate([zero, v_], axis=0)), l_both, v_h)
        a2 = each(_mm, xi, ah)
        u0 = each(_mm, xi, z)
        yield
        uv = each(lambda u_, v_: jnp.concatenate([u_, v_], axis=0), u0, v_h)
        r2 = each(lambda r_, m_, a_: r_ + _mm(m_[:, :C], a_), rh, m_both, a2)
        y0 = each(_mm, m_both, uv)
        gt = each(_tn, a2, b2)
        ht = each(lambda uv_, b_, k_: _tn(uv_, jnp.concatenate([b_, k_], axis=0)), uv, b2, k2p)
        yield
        y = []
        for h in range(H):
            y.append(_nt(r2[h], state[h]) + y0[h])
            state[h] = state[h] * p_end[h] + _mm(state[h], gt[h]) + ht[h]
        yield
        yc = each(lambda t: t - jnp.mean(t, axis=-1, keepdims=True), y)
        var = each(lambda t: jnp.mean(t * t, axis=-1, keepdims=True), yc)
        rk_sum = each(lambda t: jnp.sum(t, axis=-1, keepdims=True), rk_h)
        yield
        for h in range(H):
            sl = slice(h * N, (h + 1) * N)
            yn = yc[h] * lax.rsqrt(var[h] + GN_EPS) * gnw_ref[:, sl] + gnb_ref[:, sl]
            o_ref[j * C:(j + 1) * C, sl] = (yn + rk_sum[h] * v_h[h]) * g[:, sl]

    def finish():
        for h in range(H):
            s_scr[h] = state[h]

    return [solve(j) for j in range(nc)], finish


def _mlstm_programs(pm_ref, cw_ref, cb_ref, gb_ref, hn_ref, o_ref, ext_scr, c_scr, n_scr, m_scr):
    C = CHUNK
    G = MLSTM_GROUP_ROWS
    H = MLSTM_HEADS
    DK, DV = MLSTM_DK, MLSTM_DV
    QK2 = 2 * MLSTM_QK
    rows = pm_ref.shape[0]

    @pl.when(pl.program_id(1) == 0)
    def _():
        ext_scr[0:8, :] = jnp.zeros((8, QK2), F32)
        c_scr[...] = jnp.zeros_like(c_scr)
        n_scr[...] = jnp.zeros_like(n_scr)
        m_scr[...] = jnp.zeros_like(m_scr)

    ext_scr[8:8 + rows, :] = pm_ref[:, 0:QK2]
    rg = lax.broadcasted_iota(jnp.int32, (G, G), 0)
    cg = lax.broadcasted_iota(jnp.int32, (G, G), 1)
    in_chunk = ((cg <= rg) & ((rg >> _log2(C)) == (cg >> _log2(C)))).astype(BF16)
    lane = lax.broadcasted_iota(jnp.int32, (G, LANES), 1)
    ri = lax.broadcasted_iota(jnp.int32, (LANES, LANES), 0)
    ci = lax.broadcasted_iota(jnp.int32, (LANES, LANES), 1)
    eye = (ri == ci).astype(F32)
    row = lax.broadcasted_iota(jnp.int32, (C, C), 0)
    col = lax.broadcasted_iota(jnp.int32, (C, C), 1)
    incl = col <= row
    each = lambda f, *ls: [f(*t) for t in zip(*ls)]
    m_run = [m_scr[h][0:1, 0:1] for h in range(H)]
    c_run = [c_scr[h] for h in range(H)]
    n_run = [n_scr[h][0:1, 0:DK] for h in range(H)]

    def solve(p):
        r0 = p * G
        conv = cb_ref[...] + cw_ref[0:1, :] * ext_scr[5 + r0:5 + r0 + G, :]
        for i in range(1, CONV_WIDTH):
            conv = conv + cw_ref[i:i + 1, :] * ext_scr[5 + i + r0:5 + i + r0 + G, :]
        qk = conv * _sigmoid(conv)
        q = qk[:, :MLSTM_QK] * (DK ** -0.5)
        k = qk[:, MLSTM_QK:]
        v = pm_ref[r0:r0 + G, QK2:QK2 + MLSTM_WIDTH]
        og = pm_ref[r0:r0 + G, QK2 + MLSTM_WIDTH:MLSTM_MAIN]
        z = GATE_SOFTCAP * jnp.tanh((pm_ref[r0:r0 + G, MLSTM_MAIN:] + gb_ref[...]) / GATE_SOFTCAP)
        log_f = jnp.minimum(z, 0.0) - jnp.log1p(jnp.exp(-jnp.abs(z)))
        gates = jnp.where(lane < H, z, _cumsum_rows(in_chunk, log_f))
        gates_t = lax.dot_general(eye, gates, (((1,), (1,)), ((), ())),
                                  preferred_element_type=F32, precision=HIGHEST)
        yield
        units = [(j, h) for j in range(G // C) for h in range(H)]
        li_col = [gates[j * C:(j + 1) * C, h:h + 1] for j, h in units]
        b_col = [gates[j * C:(j + 1) * C, H + h:H + h + 1] for j, h in units]
        li_row = [gates_t[h:h + 1, j * C:(j + 1) * C] for j, h in units]
        b_row = [gates_t[H + h:H + h + 1, j * C:(j + 1) * C] for j, h in units]
        q_h = [q[j * C:(j + 1) * C, h * DK:(h + 1) * DK] for j, h in units]
        k_h = [k[j * C:(j + 1) * C, h * DK:(h + 1) * DK] for j, h in units]
        v_h = [v[j * C:(j + 1) * C, h * DV:(h + 1) * DV] for j, h in units]
        b_last = each(lambda bc: bc[C - 1:C, :], b_col)
        end_log = each(lambda bl, bc, lc: bl - bc + lc, b_last, b_col, li_col)
        end_max = each(lambda el: jnp.max(el, axis=0, keepdims=True), end_log)
        m, m_new = [], []
        for i, (j, h) in enumerate(units):
            m.append(m_run[h])
            m_run[h] = jnp.maximum(b_last[i] + m_run[h], end_max[i])
            m_new.append(m_run[h])
        yield
        qk_h = each(_nt, q_h, k_h)
        d_mat = each(lambda bc, br, lr: jnp.where(incl, bc - br + lr, -jnp.inf), b_col, b_row, li_row)
        inter = each(lambda bc, m_: bc + m_, b_col, m)
        m_t = each(lambda i_, d_: jnp.maximum(i_, jnp.max(d_, axis=-1, keepdims=True)), inter, d_mat)
        yield
        scores = each(lambda s_, d_, mt: s_ * jnp.exp(d_ - mt), qk_h, d_mat, m_t)
        w_inter = each(lambda i_, mt: jnp.exp(i_ - mt), inter, m_t)
        sv_h = each(_mm, scores, v_h)
        carry_w = each(lambda bl, m_, mn: jnp.exp(bl + m_ - mn), b_last, m, m_new)
        kw = each(lambda k_, el, mn: k_ * jnp.exp(el - mn), k_h, end_log, m_new)
        kv_h = each(_tn, kw, v_h)
        yield
        c_mat, n_vec = [], []
        for i, (j, h) in enumerate(units):
            c_mat.append(c_run[h])
            n_vec.append(n_run[h])
            c_run[h] = carry_w[i] * c_run[h] + kv_h[i]
            n_run[h] = carry_w[i] * n_run[h] + jnp.sum(kw[i], axis=0, keepdims=True)
        qc_h = each(_mm, q_h, c_mat)
        yield
        num = each(lambda s_, w_, c_: s_ + w_ * c_, sv_h, w_inter, qc_h)
        s_sum = each(lambda s_: jnp.sum(s_, axis=-1, keepdims=True), scores)
        qn = each(lambda q_, n_: jnp.sum(q_ * n_, axis=-1, keepdims=True), q_h, n_vec)
        den = each(lambda s_, w_, q_: s_ + w_ * q_, s_sum, w_inter, qn)
        hh = each(lambda n_, d_, mt: n_ / jnp.maximum(jnp.abs(d_), jnp.exp(-mt)), num, den, m_t)
        ms = each(lambda t: jnp.mean(t * t, axis=-1, keepdims=True), hh)
        yield
        for i, (j, h) in enumerate(units):
            rs = slice(r0 + j * C, r0 + (j + 1) * C)
            hn = hh[i] * lax.rsqrt(ms[i] + EPS) * hn_ref[:, h * DV:(h + 1) * DV]
            o_ref[rs, h * DV:(h + 1) * DV] = hn * _sigmoid(og[j * C:(j + 1) * C, h * DV:(h + 1) * DV])

    def finish():
        ext_scr[0:8, :] = ext_scr[rows:rows + 8, :]
        for h in range(H):
            c_scr[h] = c_run[h]
            n_scr[h] = jnp.broadcast_to(jnp.concatenate([n_run[h], jnp.zeros((1, LANES - DK), F32)], axis=1),
                                        (8, LANES))
            m_scr[h] = jnp.broadcast_to(m_run[h], (8, LANES))

    return [solve(p) for p in range(rows // G)], finish


def _mixers_kernel(pr_ref, mu_ref, w0_ref, wup_ref, a0_ref, aup_ref, gup_ref, kk_ref, ka_ref, rk_ref, gnw_ref,
                   gnb_ref, hs_ref, pm_ref, cw_ref, cb_ref, gb_ref, hn_ref, yr_ref, ym_ref,
                   s_scr, prev_scr, ext_scr, c_scr, n_scr, m_scr):
    r_progs, r_finish = _rwkv_programs(pr_ref, mu_ref, w0_ref, wup_ref, a0_ref, aup_ref, gup_ref, kk_ref, ka_ref,
                                       rk_ref, gnw_ref, gnb_ref, hs_ref, yr_ref, s_scr, prev_scr)
    m_progs, m_finish = _mlstm_programs(pm_ref, cw_ref, cb_ref, gb_ref, hn_ref, ym_ref, ext_scr, c_scr, n_scr,
                                        m_scr)
    pending = [g for pair in zip(r_progs, m_progs) for g in pair]
    active = []
    tick = 0
    while pending or active:
        if pending and tick % MIX_SKEW == 0:
            active.append(pending.pop(0))
        for gen in list(active):
            try:
                next(gen)
            except StopIteration:
                active.remove(gen)
        tick += 1
    r_finish()
    m_finish()


def _mixers(pr, pm, mu, w0, w_up, a0, a_up, g_up, k_k, k_a, r_k, gn_w, gn_b, conv_w, conv_b, i_b, f_b, hn_w, rows):
    bsz, seq, cols = pr.shape
    mcols = pm.shape[2]
    row = lambda p: p.reshape(1, -1)
    full = lambda a: pl.BlockSpec(a.shape, lambda b, c: (0,) * a.ndim)
    tok = lambda n: pl.BlockSpec((None, rows, n), lambda b, c: (b, c, 0))
    lane_head = jnp.arange(RWKV_WIDTH) // RWKV_HEAD
    same_head = (lane_head[:, None] == lane_head[None, :]).astype(BF16)
    gate_b = jnp.zeros((1, LANES), F32).at[0, :MLSTM_HEADS].set(i_b).at[0, MLSTM_HEADS:2 * MLSTM_HEADS].set(f_b)
    r_params = [row(mu), row(w0), w_up, row(a0), a_up, g_up, row(k_k), row(k_a), row(r_k), row(gn_w), row(gn_b),
                same_head]
    m_params = [conv_w, row(conv_b), gate_b, row(hn_w)]
    return pl.pallas_call(
        _mixers_kernel,
        grid=(bsz, seq // rows),
        in_specs=[tok(cols)] + [full(a) for a in r_params] + [tok(mcols)] + [full(a) for a in m_params],
        out_specs=[tok(RWKV_WIDTH), tok(MLSTM_WIDTH)],
        out_shape=[jax.ShapeDtypeStruct((bsz, seq, RWKV_WIDTH), F32),
                   jax.ShapeDtypeStruct((bsz, seq, MLSTM_WIDTH), F32)],
        scratch_shapes=[pltpu.VMEM((RWKV_HEADS, RWKV_HEAD, RWKV_HEAD), F32),
                        pltpu.VMEM((1, cols), F32),
                        pltpu.VMEM((rows + 8, 2 * MLSTM_QK), F32),
                        pltpu.VMEM((MLSTM_HEADS, MLSTM_DK, MLSTM_DV), F32),
                        pltpu.VMEM((MLSTM_HEADS, 8, LANES), F32),
                        pltpu.VMEM((MLSTM_HEADS, 8, LANES), F32)],
        compiler_params=_params("arbitrary", "arbitrary"),
        name="mixers",
    )(pr, *r_params, pm, *m_params)


def _mix_out_kernel(x_ref, yr_ref, ym_ref, mod_ref, nw_ref, wor_ref, wom_ref, wr_ref, br_ref, x1_ref, hx_ref):
    d = x_ref.shape[-1]
    y = (jnp.dot(yr_ref[...].astype(BF16), wor_ref[...], preferred_element_type=F32)
         + jnp.dot(ym_ref[...].astype(BF16), wom_ref[...], preferred_element_type=F32))
    x1 = x_ref[...] + mod_ref[2:3, :] * y
    x1_ref[...] = x1
    h = _norm_mod(x1, nw_ref[...], mod_ref[3:4, :], mod_ref[4:5, :])
    h_hi = h.astype(BF16)
    hx_ref[:, :d] = h_hi

    h_lo = (h - h_hi.astype(F32)).astype(BF16)
    w = wr_ref[...]
    w_hi = w.astype(BF16)
    w_lo = (w - w_hi.astype(F32)).astype(BF16)
    hw = jnp.dot(h_hi, jnp.concatenate([w_hi, w_lo], axis=1), preferred_element_type=F32)
    logits = hw[:, :LANES] + hw[:, LANES:] + jnp.dot(h_lo, w_hi, preferred_element_type=F32) + br_ref[...]
    lane = lax.broadcasted_iota(jnp.int32, logits.shape, 1)
    neg = -jnp.inf
    el = logits
    gl = jnp.where((lane >= N_EXPERTS) & (lane < N_EXPERTS + N_GROUPS), logits, neg)
    g_max = jnp.max(gl, axis=-1, keepdims=True)
    g_idx = jnp.min(jnp.where(gl == g_max, lane, LANES), axis=-1, keepdims=True) - N_EXPERTS
    g_w = 1.0 / jnp.sum(jnp.exp(gl - g_max), axis=-1, keepdims=True)
    in_group = ((lane >> 3) == g_idx) & (lane < N_EXPERTS)
    sel = jnp.where(in_group, el, neg)
    v1 = jnp.max(sel, axis=-1, keepdims=True)
    i1 = jnp.min(jnp.where(sel == v1, lane, LANES), axis=-1, keepdims=True)
    rest = jnp.where(lane == i1, neg, sel)
    v2 = jnp.max(rest, axis=-1, keepdims=True)
    i2 = jnp.min(jnp.where(rest == v2, lane, LANES), axis=-1, keepdims=True)
    e2 = jnp.exp(v2 - v1)
    w1 = 1.0 / (1.0 + e2)
    w2 = e2 / (1.0 + e2)
    cw = jnp.where(lane == i1, w1 * g_w, jnp.where(lane == i2, w2 * g_w, 0.0))
    hi = cw.astype(BF16).astype(F32)
    mid = (cw - hi).astype(BF16).astype(F32)
    lo = (cw - hi - mid).astype(BF16).astype(F32)
    flag = jnp.where(hi != 0.0, 1.0, 0.0)
    pack = (hi + pltpu.roll(mid, N_EXPERTS, axis=1) + pltpu.roll(lo, 2 * N_EXPERTS, axis=1)
            + pltpu.roll(flag, 3 * N_EXPERTS, axis=1))
    hx_ref[:, d:] = pack.astype(BF16)


def _mix_out(x, y_r, y_m, mod, nw, wo_r, wo_m, w_route, b_route, tm):
    bsz, seq, d = x.shape
    tok = lambda n: pl.BlockSpec((None, tm, n), lambda b, i: (b, i, 0))
    full = lambda a: pl.BlockSpec(a.shape, lambda b, i: (0, 0))
    return pl.pallas_call(
        _mix_out_kernel,
        grid=(bsz, seq // tm),
        in_specs=[tok(d), tok(RWKV_WIDTH), tok(MLSTM_WIDTH),
                  pl.BlockSpec((None, 6, d), lambda b, i: (b, 0, 0)),
                  full(nw), full(wo_r), full(wo_m), full(w_route), full(b_route)],
        out_specs=[tok(d), tok(d + LANES)],
        out_shape=[jax.ShapeDtypeStruct((bsz, seq, d), F32),
                   jax.ShapeDtypeStruct((bsz, seq, d + LANES), BF16)],
        compiler_params=_params("arbitrary", "arbitrary"),
        name="mix_out",
    )(x, y_r, y_m, mod, nw, wo_r, wo_m, w_route, b_route)


ROW_CHUNK = 16
SORT_TILE = 512
EXPERT_TILE = 512
NO_SLOT = 1e9


def _moe_plan(flags, tms, ts):
    t, ne = flags.shape
    nt = t // tms
    i32 = jnp.int32
    tri = lambda n: (jnp.arange(n)[:, None] < jnp.arange(n)[None, :]).astype(F32)
    before = lambda a, b: jnp.dot(a, b, precision=HIGHEST).astype(i32)
    cnt = flags.reshape(nt, tms, ne).astype(i32).sum(axis=1)
    pc = (cnt + ROW_CHUNK - 1) // ROW_CHUNK * ROW_CHUNK
    pcf = pc.astype(F32)
    loc_start = before(pcf, tri(ne))
    len_e = pc.sum(axis=0)
    seg_e = (len_e + ts - 1) // ts * ts
    e_start = before(seg_e.astype(F32)[None, :], tri(ne))[0]
    e_end = e_start + seg_e
    dst_start = e_start[None, :] + before(tri(nt).T, pcf)
    n_tiles_max = _moe_max_rows(t, ne, tms, ts) // ts
    tile_row = jnp.arange(n_tiles_max, dtype=i32) * ts
    tile_expert = jnp.minimum((tile_row[:, None] >= e_end[None, :]).astype(i32).sum(axis=1), ne - 1)
    max_chunks = _local_rows(tms) // ROW_CHUNK
    chunk_row = jnp.arange(max_chunks, dtype=i32)[None, :, None] * ROW_CHUNK
    run_end = (loc_start + pc)[:, None, :]
    run_of = jnp.minimum((chunk_row >= run_end).astype(i32).sum(axis=2), ne - 1)
    run_pick = run_of[:, :, None] == jnp.arange(ne, dtype=i32)[None, None, :]
    chunk_dst = jnp.where(run_pick, (dst_start - loc_start)[:, None, :], 0).sum(axis=2) + chunk_row[:, :, 0]
    return dict(
        chunk_dst=chunk_dst.reshape(-1).astype(i32), tile_chunks=(pc.sum(axis=1) // ROW_CHUNK).astype(i32),
        tail_start=(e_start + len_e).astype(i32),
        tail_chunks=((seg_e - len_e) // ROW_CHUNK).astype(i32),
        n_used=(e_end[-1:] // ts).astype(i32), tile_expert=tile_expert,
        loc_start_vec=jnp.pad(loc_start.astype(F32), ((0, 0), (0, LANES - ne))).reshape(nt, 1, LANES))


def _moe_max_rows(t, ne, tms, ts):
    rows = 2 * t + (t // tms) * ne * (ROW_CHUNK - 1) + ne * (ts - 1)
    return (rows + ts - 1) // ts * ts


def _local_rows(tms):
    return 2 * tms + N_EXPERTS * ROW_CHUNK


def _tile_slots(blk, loc_start_vec):
    tms = blk.shape[0]
    lane = lax.broadcasted_iota(jnp.int32, blk.shape, 1)
    m = jnp.where(lane < N_EXPERTS, pltpu.roll(blk, N_EXPERTS, axis=1), 0.0)
    row = lax.broadcasted_iota(jnp.int32, (tms, tms), 0)
    col = lax.broadcasted_iota(jnp.int32, (tms, tms), 1)
    before = jnp.dot((col < row).astype(BF16), m.astype(BF16), preferred_element_type=F32)
    slot = loc_start_vec + before
    slot_a = jnp.min(jnp.where(m > 0.0, slot, NO_SLOT), axis=-1, keepdims=True)
    slot_b = jnp.max(jnp.where(m > 0.0, slot, -NO_SLOT), axis=-1, keepdims=True)
    return slot_a, slot_b


def _moe_sort_kernel(cd_ref, tc_ref, tls_ref, tlc_ref, nu_ref, hx_ref, lsv_ref, xo_ref, xs_scr, z_scr,
                     sems, zsem):
    i = pl.program_id(0)
    nt = pl.num_programs(0)
    tms, width = hx_ref.shape
    loc = xs_scr.shape[1]
    slot = i % 2
    d = width - LANES
    slot_a, slot_b = _tile_slots(hx_ref[:, d:].astype(F32), lsv_ref[...])
    lane = lax.broadcasted_iota(jnp.int32, (tms, LANES), 1)
    packed = jnp.where(lane == 0, slot_a, jnp.where(lane == 1, slot_b, 0.0))
    ri = lax.broadcasted_iota(jnp.int32, (LANES, LANES), 0)
    ci = lax.broadcasted_iota(jnp.int32, (LANES, LANES), 1)
    slots_t = lax.dot_general((ri == ci).astype(F32), packed, (((1,), (1,)), ((), ())),
                              preferred_element_type=F32, precision=HIGHEST)
    srow = lax.broadcasted_iota(jnp.int32, (loc, tms), 0).astype(F32)
    perm = jnp.where((srow == slots_t[0:1, :]) | (srow == slots_t[1:2, :]), 1.0, 0.0).astype(BF16)
    xs_scr[slot] = jnp.dot(perm, hx_ref[...], preferred_element_type=F32).astype(BF16)

    def copy(src, src_row, dst_row, sem):
        return pltpu.make_async_copy(src.at[pl.ds(pl.multiple_of(src_row, ROW_CHUNK), ROW_CHUNK)],
                                     xo_ref.at[pl.ds(pl.multiple_of(dst_row, ROW_CHUNK), ROW_CHUNK)], sem)

    max_chunks = loc // ROW_CHUNK

    def run_body(j, carry):
        copy(xs_scr.at[slot], j * ROW_CHUNK, cd_ref[i * max_chunks + j], sems.at[slot]).start()
        return carry

    lax.fori_loop(0, tc_ref[i], run_body, 0)

    def wait_runs(step):
        def body(j, carry):
            copy(z_scr, 0, 0, sems.at[step % 2]).wait()
            return carry

        lax.fori_loop(0, tc_ref[step], body, 0)

    @pl.when(i == 0)
    def _():
        z_scr[...] = jnp.zeros_like(z_scr)
        ts = z_scr.shape[0]
        n_tail = jnp.int32(0)
        for e in range(N_EXPERTS):
            n = tlc_ref[e]
            t0 = tls_ref[e]

            def tail_body(j, carry, t0=t0):
                copy(z_scr, 0, t0 + j * ROW_CHUNK, sems.at[2]).start()
                return carry

            lax.fori_loop(0, n, tail_body, 0)
            n_tail = n_tail + n

        def tile_copy(tile):
            return pltpu.make_async_copy(z_scr, xo_ref.at[pl.ds(pl.multiple_of(tile * ts, ts), ts)], zsem)

        n_spare = xo_ref.shape[0] // ts - nu_ref[0]

        def spare_body(j, carry):
            tile_copy(nu_ref[0] + j).start()
            return carry

        lax.fori_loop(0, n_spare, spare_body, 0)

        def tail_wait_body(j, carry):
            copy(z_scr, 0, 0, sems.at[2]).wait()
            return carry

        lax.fori_loop(0, n_tail, tail_wait_body, 0)

        def spare_wait_body(j, carry):
            tile_copy(0).wait()
            return carry

        lax.fori_loop(0, n_spare, spare_wait_body, 0)

    @pl.when(i > 0)
    def _():
        wait_runs(i - 1)

    @pl.when(i == nt - 1)
    def _():
        wait_runs(i)


def _moe_sort(hx, plan, tms, ts):
    t, width = hx.shape
    nt = t // tms
    rows = _moe_max_rows(t, N_EXPERTS, tms, ts)
    return pl.pallas_call(
        _moe_sort_kernel,
        grid_spec=pltpu.PrefetchScalarGridSpec(
            num_scalar_prefetch=5,
            grid=(nt,),
            in_specs=[pl.BlockSpec((tms, width), lambda i, *_: (i, 0)),
                      pl.BlockSpec((None, 1, LANES), lambda i, *_: (i, 0, 0))],
            out_specs=pl.BlockSpec(memory_space=pl.ANY),
            scratch_shapes=[pltpu.VMEM((2, _local_rows(tms), width), BF16),
                            pltpu.VMEM((ts, width), BF16),
                            pltpu.SemaphoreType.DMA((3,)),
                            pltpu.SemaphoreType.DMA(())]),
        out_shape=jax.ShapeDtypeStruct((rows, width), BF16),
        compiler_params=_params("arbitrary"),
        name="moe_sort",
    )(plan["chunk_dst"], plan["tile_chunks"], plan["tail_start"], plan["tail_chunks"], plan["n_used"],
      hx, plan["loc_start_vec"])


def _moe_experts_kernel(te_ref, nu_ref, xs_ref, wg_ref, wu_ref, wd_ref, o_ref):
    j = pl.program_id(0)

    @pl.when(j < nu_ref[0])
    def _():
        d = xs_ref.shape[1] - LANES
        e = te_ref[j]
        x = xs_ref[:, :d]
        wblk = xs_ref[:, d:].astype(F32)
        lane = lax.broadcasted_iota(jnp.int32, wblk.shape, 1)
        mine = (lane == e) | (lane == e + N_EXPERTS) | (lane == e + 2 * N_EXPERTS)
        cw = jnp.sum(jnp.where(mine, wblk, 0.0), axis=-1, keepdims=True)
        hg = jnp.dot(x, wg_ref[...], preferred_element_type=F32)
        hu = jnp.dot(x, wu_ref[...], preferred_element_type=F32)
        act = hg * _sigmoid(hg) * hu * cw
        o_ref[...] = jnp.dot(act.astype(BF16), wd_ref[...], preferred_element_type=F32).astype(BF16)

    @pl.when(j >= nu_ref[0])
    def _():
        o_ref[...] = jnp.zeros_like(o_ref)


def _moe_experts(xs, plan, w_gate, w_up, w_down, ts):
    rows, width = xs.shape
    ne, d, f = w_gate.shape
    tile = lambda j, te, nu: (jnp.minimum(j, nu[0] - 1), 0)
    return pl.pallas_call(
        _moe_experts_kernel,
        grid_spec=pltpu.PrefetchScalarGridSpec(
            num_scalar_prefetch=2,
            grid=(rows // ts,),
            in_specs=[pl.BlockSpec((ts, width), tile),
                      pl.BlockSpec((None, d, f), lambda j, te, nu: (te[j], 0, 0)),
                      pl.BlockSpec((None, d, f), lambda j, te, nu: (te[j], 0, 0)),
                      pl.BlockSpec((None, f, d), lambda j, te, nu: (te[j], 0, 0))],
            out_specs=pl.BlockSpec((ts, d), lambda j, te, nu: (j, 0))),
        out_shape=jax.ShapeDtypeStruct((rows, d), BF16),
        compiler_params=_params("arbitrary"),
        name="moe_experts",
    )(plan["tile_expert"], plan["n_used"], xs, w_gate, w_up, w_down)


def _moe_combine_kernel(cd_ref, tc_ref, wblk_ref, lsv_ref, x1_ref, mod_ref, fw_ref, ys_hbm,
                        o_ref, ys_scr, sems):
    i = pl.program_id(0)
    nt = pl.num_programs(0)
    tms = x1_ref.shape[0]
    loc = ys_scr.shape[1]

    def copy(slot, src_row, dst_row):
        return pltpu.make_async_copy(
            ys_hbm.at[pl.ds(pl.multiple_of(src_row, ROW_CHUNK), ROW_CHUNK)],
            ys_scr.at[slot, pl.ds(pl.multiple_of(dst_row, ROW_CHUNK), ROW_CHUNK)], sems.at[slot])

    max_chunks = loc // ROW_CHUNK

    def fetch(tile, slot):
        def run_body(j, carry):
            copy(slot, cd_ref[tile * max_chunks + j], j * ROW_CHUNK).start()
            return carry

        lax.fori_loop(0, tc_ref[tile], run_body, 0)

    @pl.when(i == 0)
    def _():
        ys_scr[...] = jnp.zeros_like(ys_scr)
        fetch(0, 0)

    slot = i % 2

    @pl.when(i + 1 < nt)
    def _():
        fetch(i + 1, 1 - slot)

    slot_a, slot_b = _tile_slots(wblk_ref[...].astype(F32), lsv_ref[...])
    scol = lax.broadcasted_iota(jnp.int32, (tms, loc), 1).astype(F32)
    perm = jnp.where((scol == slot_a) | (scol == slot_b), 1.0, 0.0).astype(BF16)

    def wait_body(j, carry):
        copy(slot, 0, 0).wait()
        return carry

    lax.fori_loop(0, tc_ref[i], wait_body, 0)
    y = jnp.dot(perm, ys_scr[slot], preferred_element_type=F32)
    x2 = x1_ref[...] + mod_ref[5:6, :] * y
    o_ref[...] = x2 * lax.rsqrt(jnp.mean(x2 * x2, axis=-1, keepdims=True) + EPS) * fw_ref[...]


def _moe_combine(hx, ys, x1, mod, fw, plan, tms, tiles_per_seq):
    t, d = x1.shape
    nt = t // tms
    return pl.pallas_call(
        _moe_combine_kernel,
        grid_spec=pltpu.PrefetchScalarGridSpec(
            num_scalar_prefetch=2,
            grid=(nt,),
            in_specs=[pl.BlockSpec((tms, LANES), lambda i, *_: (i, d // LANES)),
                      pl.BlockSpec((None, 1, LANES), lambda i, *_: (i, 0, 0)),
                      pl.BlockSpec((tms, d), lambda i, *_: (i, 0)),
                      pl.BlockSpec((None, 6, d), lambda i, *_: (i // tiles_per_seq, 0, 0)),
                      pl.BlockSpec((1, d), lambda i, *_: (0, 0)),
                      pl.BlockSpec(memory_space=pl.ANY)],
            out_specs=pl.BlockSpec((tms, d), lambda i, *_: (i, 0)),
            scratch_shapes=[pltpu.VMEM((2, _local_rows(tms), d), BF16),
                            pltpu.SemaphoreType.DMA((2,))]),
        out_shape=jax.ShapeDtypeStruct((t, d), F32),
        compiler_params=_params("arbitrary"),
        name="moe_combine",
    )(plan["chunk_dst"], plan["tile_chunks"], hx, plan["loc_start_vec"], x1, mod, fw, ys)


def _pad_cols(w, n):
    return jnp.pad(w, ((0, 0), (0, n - w.shape[1])))


def kernel(x, c, ada_w, ada_b, mix_norm_w, w_in, rwkv_mu, rwkv_w0, rwkv_w_up, rwkv_a0, rwkv_a_up, rwkv_g_up, rwkv_k_k, rwkv_k_a, rwkv_r_k, rwkv_gn_w, rwkv_gn_b, mlstm_conv_w, mlstm_conv_b, mlstm_i_b, mlstm_f_b, mlstm_hn_w, w_out, ffn_norm_w, moe_w_group, moe_b_group, moe_w_router, moe_b_router, moe_w_gate, moe_w_up, moe_w_down, final_norm_w):
    bsz, seq, d = x.shape
    assert ada_w.shape[0] == 1, "the fused final norm assumes a single layer"
    assert seq % CHUNK == 0
    l = 0
    tm = min(512, seq)
    tms = min(SORT_TILE, seq)
    mod = _ada(c, ada_w[l], ada_b[l]).reshape(bsz, 6, d)
    w_r = w_in[l][:, :RWKV_COLS].astype(BF16)
    w_m = _pad_cols(w_in[l][:, RWKV_COLS:], MLSTM_PAD_COLS).astype(BF16)
    pr, pm = _in_proj(x, mod, mix_norm_w[l].reshape(1, d), w_r, w_m, tm)
    y_r, y_m = _mixers(pr, pm, rwkv_mu[l], rwkv_w0[l], rwkv_w_up[l], rwkv_a0[l], rwkv_a_up[l], rwkv_g_up[l],
                       rwkv_k_k[l], rwkv_k_a[l], rwkv_r_k[l], rwkv_gn_w[l], rwkv_gn_b[l], mlstm_conv_w[l],
                       mlstm_conv_b[l], mlstm_i_b[l], mlstm_f_b[l], mlstm_hn_w[l], min(MIX_STEP_ROWS, seq))
    wo = w_out[l].astype(BF16)
    w_route = _pad_cols(jnp.concatenate([moe_w_router[l], moe_w_group[l]], axis=1), LANES)
    b_route = _pad_cols(jnp.concatenate([moe_b_router[l], moe_b_group[l]]).reshape(1, -1), LANES)
    x1, hx = _mix_out(
        x, y_r, y_m, mod, ffn_norm_w[l].reshape(1, d), wo[:RWKV_WIDTH], wo[RWKV_WIDTH:], w_route, b_route, tm)
    hx = hx.reshape(bsz * seq, d + LANES)
    flags = hx[:, d + 3 * N_EXPERTS:] != 0
    plan = _moe_plan(flags, tms, EXPERT_TILE)
    xs = _moe_sort(hx, plan, tms, EXPERT_TILE)
    ys = _moe_experts(xs, plan, moe_w_gate[l].astype(BF16), moe_w_up[l].astype(BF16),
                      moe_w_down[l].astype(BF16), EXPERT_TILE)
    out = _moe_combine(hx, ys, x1.reshape(bsz * seq, d), mod, final_norm_w.reshape(1, d), plan, tms, seq // tms)
    return out.reshape(bsz, seq, d)
```

```python
import jax
import jax.numpy as jnp
from jax import lax
from jax.experimental import pallas as pl
from jax.experimental.pallas import tpu as pltpu

F32 = jnp.float32
BF16 = jnp.bfloat16
HIGHEST = lax.Precision.HIGHEST

CHUNK = 64
RWKV_CHUNK = 128
NEUMANN_BLOCK = 16
MIX_STEP_ROWS = 512
MLSTM_GROUP_ROWS = 128
MIX_SKEW = 2
EPS = 1e-6
RWKV_HEAD = 64
RWKV_HEADS = 8
RWKV_WIDTH = RWKV_HEADS * RWKV_HEAD
DECAY_LORA = 64
ICLR_LORA = 64
GATE_LORA = 128
GN_EPS = 64e-5
MLSTM_HEADS = 4
MLSTM_DK = 64
MLSTM_DV = 128
MLSTM_QK = MLSTM_HEADS * MLSTM_DK
MLSTM_WIDTH = MLSTM_HEADS * MLSTM_DV
CONV_WIDTH = 4
GATE_SOFTCAP = 15.0
RWKV_COLS = 3 * RWKV_WIDTH + DECAY_LORA + ICLR_LORA + GATE_LORA
N_GROUPS = 4
EXPERTS_PER_GROUP = 8
N_EXPERTS = N_GROUPS * EXPERTS_PER_GROUP
LANES = 128
MLSTM_MAIN = 2 * MLSTM_QK + 2 * MLSTM_WIDTH
MLSTM_PAD_COLS = MLSTM_MAIN + LANES
VMEM_LIMIT = 56 * 1024 * 1024


def _mm(a, b):
    return jnp.dot(a.astype(BF16), b.astype(BF16), preferred_element_type=F32)


def _nt(a, b):
    return lax.dot_general(a.astype(BF16), b.astype(BF16), (((1,), (1,)), ((), ())),
                           preferred_element_type=F32)


def _tn(a, b):
    return lax.dot_general(a.astype(BF16), b.astype(BF16), (((0,), (0,)), ((), ())),
                           preferred_element_type=F32)


def _mm_f32(a, b):
    return jnp.dot(a, b, preferred_element_type=F32, precision=HIGHEST)


def _cumsum_rows(tril, x):
    hi = x.astype(BF16)
    r1 = x - hi.astype(F32)
    mid = r1.astype(BF16)
    lo = (r1 - mid.astype(F32)).astype(BF16)
    n = x.shape[1]
    out = jnp.dot(tril, jnp.concatenate([hi, mid, lo], axis=1), preferred_element_type=F32)
    return out[:, :n] + out[:, n:2 * n] + out[:, 2 * n:]


def _sigmoid(z):
    return 1.0 / (1.0 + jnp.exp(-z))


def _softplus(z):
    return jnp.maximum(z, 0.0) + jnp.log1p(jnp.exp(-jnp.abs(z)))


def _log2(n):
    assert n & (n - 1) == 0
    return n.bit_length() - 1


def _params(*sem):
    return pltpu.CompilerParams(dimension_semantics=sem, vmem_limit_bytes=VMEM_LIMIT)


def _ada_kernel(c_ref, w_ref, b_ref, o_ref):
    c = c_ref[...]
    o_ref[...] = _mm_f32(c * _sigmoid(c), w_ref[...]) + b_ref[...]


def _ada(c, w, b):
    bsz, d = c.shape
    n = w.shape[1]
    return pl.pallas_call(
        _ada_kernel,
        grid=(n // d,),
        in_specs=[pl.BlockSpec((bsz, d), lambda j: (0, 0)),
                  pl.BlockSpec((d, d), lambda j: (0, j)),
                  pl.BlockSpec((1, d), lambda j: (0, j))],
        out_specs=pl.BlockSpec((bsz, d), lambda j: (0, j)),
        out_shape=jax.ShapeDtypeStruct((bsz, n), F32),
        compiler_params=_params("arbitrary"),
        name="ada",
    )(c, w, b.reshape(1, n))


def _norm_mod(x, nw, shift, scale):
    y = x * lax.rsqrt(jnp.mean(x * x, axis=-1, keepdims=True) + EPS) * nw
    return y * (1.0 + scale) + shift


def _in_proj_kernel(x_ref, mod_ref, nw_ref, wr_ref, wm_ref, pr_ref, pm_ref):
    h = _norm_mod(x_ref[...], nw_ref[...], mod_ref[0:1, :], mod_ref[1:2, :]).astype(BF16)
    pr_ref[...] = jnp.dot(h, wr_ref[...], preferred_element_type=F32)
    pm_ref[...] = jnp.dot(h, wm_ref[...], preferred_element_type=F32)


def _in_proj(x, mod, nw, w_r, w_m, tm):
    bsz, seq, d = x.shape
    nr, nm = w_r.shape[1], w_m.shape[1]
    return pl.pallas_call(
        _in_proj_kernel,
        grid=(bsz, seq // tm),
        in_specs=[pl.BlockSpec((None, tm, d), lambda b, i: (b, i, 0)),
                  pl.BlockSpec((None, 6, d), lambda b, i: (b, 0, 0)),
                  pl.BlockSpec((1, d), lambda b, i: (0, 0)),
                  pl.BlockSpec((d, nr), lambda b, i: (0, 0)),
                  pl.BlockSpec((d, nm), lambda b, i: (0, 0))],
        out_specs=[pl.BlockSpec((None, tm, nr), lambda b, i: (b, i, 0)),
                   pl.BlockSpec((None, tm, nm), lambda b, i: (b, i, 0))],
        out_shape=[jax.ShapeDtypeStruct((bsz, seq, nr), F32),
                   jax.ShapeDtypeStruct((bsz, seq, nm), F32)],
        compiler_params=_params("arbitrary", "arbitrary"),
        name="in_proj",
    )(x, mod, nw, w_r, w_m)


def _rwkv_programs(pr_ref, mu_ref, w0_ref, wup_ref, a0_ref, aup_ref, gup_ref, kk_ref, ka_ref,
                   rk_ref, gnw_ref, gnb_ref, hs_ref, o_ref, s_scr, prev_scr):
    C = RWKV_CHUNK
    N = RWKV_HEAD
    H = RWKV_HEADS
    rows = pr_ref.shape[0]
    nc = rows // C

    @pl.when(pl.program_id(1) == 0)
    def _():
        s_scr[...] = jnp.zeros_like(s_scr)
        prev_scr[...] = jnp.zeros_like(prev_scr)

    x = pr_ref[...]
    row1 = lax.broadcasted_iota(jnp.int32, x.shape, 0)
    prev = jnp.where(row1 == 0, prev_scr[...], pltpu.roll(x, 1, axis=0))
    prev_scr[...] = x[rows - 1:rows, :]
    u_all = x + (prev - x) * mu_ref[...]

    row = lax.broadcasted_iota(jnp.int32, (C, C), 0)
    col = lax.broadcasted_iota(jnp.int32, (C, C), 1)
    same = lambda blk: (row >> _log2(blk)) == (col >> _log2(blk))
    eye = (row == col).astype(F32)
    tril = (col <= row).astype(BF16)
    row2 = lax.broadcasted_iota(jnp.int32, (C, 2 * C), 0)
    col2 = lax.broadcasted_iota(jnp.int32, (C, 2 * C), 1) & (C - 1)
    strict2 = col2 < row2
    incl2 = col2 <= row2
    each = lambda f, *ls: [f(*t) for t in zip(*ls)]
    state = [s_scr[h] for h in range(H)]
    o1, o2, o3 = RWKV_WIDTH, 2 * RWKV_WIDTH, 3 * RWKV_WIDTH
    o4 = o3 + DECAY_LORA
    o5 = o4 + ICLR_LORA

    def solve(j):
        u = u_all[j * C:(j + 1) * C, :]
        r, k, v = u[:, :o1], u[:, o1:o2], u[:, o2:o3]
        wd, ad, gd = u[:, o3:o4], u[:, o4:o5], u[:, o5:]
        w_log = -_softplus(-(w0_ref[...] + _mm(jnp.tanh(wd), wup_ref[...]))) - 0.5
        lw = -jnp.exp(w_log)
        a = _sigmoid(a0_ref[...] + _mm(ad, aup_ref[...]))
        g = _mm(_sigmoid(gd), gup_ref[...])
        yield
        kk = k * kk_ref[...]
        k2 = k * (1.0 + (a - 1.0) * ka_ref[...])
        sq = kk * kk
        sq_hi = sq.astype(BF16)
        sq_lo = (sq - sq_hi.astype(F32)).astype(BF16)
        sums = (jnp.dot(sq_hi, hs_ref[...], preferred_element_type=F32)
                + jnp.dot(sq_lo, hs_ref[...], preferred_element_type=F32))
        kk = kk / jnp.maximum(jnp.sqrt(sums), 1e-12)
        rk = r * k2 * rk_ref[...]
        cum = _cumsum_rows(tril, lw)
        yield
        cl = cum[C - 1:C, :]
        p_inv = jnp.exp(-cum)
        p_rest = jnp.exp(cl - cum)
        b = kk * a
        ah_w = -kk * jnp.exp(cum - lw)
        rh_w = r * jnp.exp(cum)
        bt_w, kt_w = b * p_inv, k2 * p_inv
        b2_w, k2p_w = b * p_rest, k2 * p_rest
        p_end_w = jnp.exp(cl)
        cut = lambda t: [t[:, h * N:(h + 1) * N] for h in range(H)]
        ah, rh, bt, kt, b2, k2p, v_h, p_end, rk_h = (
            cut(t) for t in (ah_w, rh_w, bt_w, kt_w, b2_w, k2p_w, v, p_end_w, rk))
        yield
        ar = each(lambda a_, r_: jnp.concatenate([a_, r_], axis=0), ah, rh)
        bk = each(lambda b_, k_: jnp.concatenate([b_, k_], axis=0), bt, kt)
        x1 = each(_nt, ar, bk)
        l_both = each(lambda t: jnp.where(strict2, t[:C, :], 0.0), x1)
        m_both = each(lambda t: jnp.where(incl2, t[C:, :], 0.0), x1)
        l_ab = each(lambda t: t[:, :C], l_both)
        yield
        blk = NEUMANN_BLOCK
        n1 = each(lambda t: jnp.where(same(blk), t, 0.0), l_ab)
        xi = each(lambda t: eye + t, n1)
        nk = n1
        power = 2
        while power < blk:
            nk = each(lambda t: _mm(t, t), nk)
            xi = each(lambda x_, t: x_ + _mm(x_, t), xi, nk)
            power *= 2
            yield
        while blk < C:
            pair = same(2 * blk) & jnp.logical_not(same(blk))
            xe = each(lambda x_, t: _mm(x_, jnp.where(pair, t, 0.0)), xi, l_ab)
            xi = each(lambda x_, t: x_ + _mm(t, x_), xi, xe)
            blk *= 2
            yield
        zero = jnp.zeros((C, N), F32)
        z = each(lambda l_, v_: _mm(l_, jnp.concatenate([zero, v_], axis=0)), l_both, v_h)
        a2 = each(_mm, xi, ah)
        u0 = each(_mm, xi, z)
        yield
        uv = each(lambda u_, v_: jnp.concatenate([u_, v_], axis=0), u0, v_h)
        r2 = each(lambda r_, m_, a_: r_ + _mm(m_[:, :C], a_), rh, m_both, a2)
        y0 = each(_mm, m_both, uv)
        gt = each(_tn, a2, b2)
        ht = each(lambda uv_, b_, k_: _tn(uv_, jnp.concatenate([b_, k_], axis=0)), uv, b2, k2p)
        yield
        y = []
        for h in range(H):
            y.append(_nt(r2[h], state[h]) + y0[h])
            state[h] = state[h] * p_end[h] + _mm(state[h], gt[h]) + ht[h]
        yield
        yc = each(lambda t: t - jnp.mean(t, axis=-1, keepdims=True), y)
        var = each(lambda t: jnp.mean(t * t, axis=-1, keepdims=True), yc)
        rk_sum = each(lambda t: jnp.sum(t, axis=-1, keepdims=True), rk_h)
        yield
        for h in range(H):
            sl = slice(h * N, (h + 1) * N)
            yn = yc[h] * lax.rsqrt(var[h] + GN_EPS) * gnw_ref[:, sl] + gnb_ref[:, sl]
            o_ref[j * C:(j + 1) * C, sl] = (yn + rk_sum[h] * v_h[h]) * g[:, sl]

    def finish():
        for h in range(H):
            s_scr[h] = state[h]

    return [solve(j) for j in range(nc)], finish


def _mlstm_programs(pm_ref, cw_ref, cb_ref, gb_ref, hn_ref, o_ref, ext_scr, c_scr, n_scr, m_scr):
    C = CHUNK
    G = MLSTM_GROUP_ROWS
    H = MLSTM_HEADS
    DK, DV = MLSTM_DK, MLSTM_DV
    QK2 = 2 * MLSTM_QK
    rows = pm_ref.shape[0]

    @pl.when(pl.program_id(1) == 0)
    def _():
        ext_scr[0:8, :] = jnp.zeros((8, QK2), F32)
        c_scr[...] = jnp.zeros_like(c_scr)
        n_scr[...] = jnp.zeros_like(n_scr)
        m_scr[...] = jnp.zeros_like(m_scr)

    ext_scr[8:8 + rows, :] = pm_ref[:, 0:QK2]
    rg = lax.broadcasted_iota(jnp.int32, (G, G), 0)
    cg = lax.broadcasted_iota(jnp.int32, (G, G), 1)
    in_chunk = ((cg <= rg) & ((rg >> _log2(C)) == (cg >> _log2(C)))).astype(BF16)
    lane = lax.broadcasted_iota(jnp.int32, (G, LANES), 1)
    ri = lax.broadcasted_iota(jnp.int32, (LANES, LANES), 0)
    ci = lax.broadcasted_iota(jnp.int32, (LANES, LANES), 1)
    eye = (ri == ci).astype(F32)
    row = lax.broadcasted_iota(jnp.int32, (C, C), 0)
    col = lax.broadcasted_iota(jnp.int32, (C, C), 1)
    incl = col <= row
    each = lambda f, *ls: [f(*t) for t in zip(*ls)]
    m_run = [m_scr[h][0:1, 0:1] for h in range(H)]
    c_run = [c_scr[h] for h in range(H)]
    n_run = [n_scr[h][0:1, 0:DK] for h in range(H)]

    def solve(p):
        r0 = p * G
        conv = cb_ref[...] + cw_ref[0:1, :] * ext_scr[5 + r0:5 + r0 + G, :]
        for i in range(1, CONV_WIDTH):
            conv = conv + cw_ref[i:i + 1, :] * ext_scr[5 + i + r0:5 + i + r0 + G, :]
        qk = conv * _sigmoid(conv)
        q = qk[:, :MLSTM_QK] * (DK ** -0.5)
        k = qk[:, MLSTM_QK:]
        v = pm_ref[r0:r0 + G, QK2:QK2 + MLSTM_WIDTH]
        og = pm_ref[r0:r0 + G, QK2 + MLSTM_WIDTH:MLSTM_MAIN]
        z = GATE_SOFTCAP * jnp.tanh((pm_ref[r0:r0 + G, MLSTM_MAIN:] + gb_ref[...]) / GATE_SOFTCAP)
        log_f = jnp.minimum(z, 0.0) - jnp.log1p(jnp.exp(-jnp.abs(z)))
        gates = jnp.where(lane < H, z, _cumsum_rows(in_chunk, log_f))
        gates_t = lax.dot_general(eye, gates, (((1,), (1,)), ((), ())),
                                  preferred_element_type=F32, precision=HIGHEST)
        yield
        units = [(j, h) for j in range(G // C) for h in range(H)]
        li_col = [gates[j * C:(j + 1) * C, h:h + 1] for j, h in units]
        b_col = [gates[j * C:(j + 1) * C, H + h:H + h + 1] for j, h in units]
        li_row = [gates_t[h:h + 1, j * C:(j + 1) * C] for j, h in units]
        b_row = [gates_t[H + h:H + h + 1, j * C:(j + 1) * C] for j, h in units]
        q_h = [q[j * C:(j + 1) * C, h * DK:(h + 1) * DK] for j, h in units]
        k_h = [k[j * C:(j + 1) * C, h * DK:(h + 1) * DK] for j, h in units]
        v_h = [v[j * C:(j + 1) * C, h * DV:(h + 1) * DV] for j, h in units]
        b_last = each(lambda bc: bc[C - 1:C, :], b_col)
        end_log = each(lambda bl, bc, lc: bl - bc + lc, b_last, b_col, li_col)
        end_max = each(lambda el: jnp.max(el, axis=0, keepdims=True), end_log)
        m, m_new = [], []
        for i, (j, h) in enumerate(units):
            m.append(m_run[h])
            m_run[h] = jnp.maximum(b_last[i] + m_run[h], end_max[i])
            m_new.append(m_run[h])
        yield
        qk_h = each(_nt, q_h, k_h)
        d_mat = each(lambda bc, br, lr: jnp.where(incl, bc - br + lr, -jnp.inf), b_col, b_row, li_row)
        inter = each(lambda bc, m_: bc + m_, b_col, m)
        m_t = each(lambda i_, d_: jnp.maximum(i_, jnp.max(d_, axis=-1, keepdims=True)), inter, d_mat)
        yield
        scores = each(lambda s_, d_, mt: s_ * jnp.exp(d_ - mt), qk_h, d_mat, m_t)
        w_inter = each(lambda i_, mt: jnp.exp(i_ - mt), inter, m_t)
        sv_h = each(_mm, scores, v_h)
        carry_w = each(lambda bl, m_, mn: jnp.exp(bl + m_ - mn), b_last, m, m_new)
        kw = each(lambda k_, el, mn: k_ * jnp.exp(el - mn), k_h, end_log, m_new)
        kv_h = each(_tn, kw, v_h)
        yield
        c_mat, n_vec = [], []
        for i, (j, h) in enumerate(units):
            c_mat.append(c_run[h])
            n_vec.append(n_run[h])
            c_run[h] = carry_w[i] * c_run[h] + kv_h[i]
            n_run[h] = carry_w[i] * n_run[h] + jnp.sum(kw[i], axis=0, keepdims=True)
        qc_h = each(_mm, q_h, c_mat)
        yield
        num = each(lambda s_, w_, c_: s_ + w_ * c_, sv_h, w_inter, qc_h)
        s_sum = each(lambda s_: jnp.sum(s_, axis=-1, keepdims=True), scores)
        qn = each(lambda q_, n_: jnp.sum(q_ * n_, axis=-1, keepdims=True), q_h, n_vec)
        den = each(lambda s_, w_, q_: s_ + w_ * q_, s_sum, w_inter, qn)
        hh = each(lambda n_, d_, mt: n_ / jnp.maximum(jnp.abs(d_), jnp.exp(-mt)), num, den, m_t)
        ms = each(lambda t: jnp.mean(t * t, axis=-1, keepdims=True), hh)
        yield
        for i, (j, h) in enumerate(units):
            rs = slice(r0 + j * C, r0 + (j + 1) * C)
            hn = hh[i] * lax.rsqrt(ms[i] + EPS) * hn_ref[:, h * DV:(h + 1) * DV]
            o_ref[rs, h * DV:(h + 1) * DV] = hn * _sigmoid(og[j * C:(j + 1) * C, h * DV:(h + 1) * DV])

    def finish():
        ext_scr[0:8, :] = ext_scr[rows:rows + 8, :]
        for h in range(H):
            c_scr[h] = c_run[h]
            n_scr[h] = jnp.broadcast_to(jnp.concatenate([n_run[h], jnp.zeros((1, LANES - DK), F32)], axis=1),
                                        (8, LANES))
            m_scr[h] = jnp.broadcast_to(m_run[h], (8, LANES))

    return [solve(p) for p in range(rows // G)], finish


def _mixers_kernel(pr_ref, mu_ref, w0_ref, wup_ref, a0_ref, aup_ref, gup_ref, kk_ref, ka_ref, rk_ref, gnw_ref,
                   gnb_ref, hs_ref, pm_ref, cw_ref, cb_ref, gb_ref, hn_ref, yr_ref, ym_ref,
                   s_scr, prev_scr, ext_scr, c_scr, n_scr, m_scr):
    r_progs, r_finish = _rwkv_programs(pr_ref, mu_ref, w0_ref, wup_ref, a0_ref, aup_ref, gup_ref, kk_ref, ka_ref,
                                       rk_ref, gnw_ref, gnb_ref, hs_ref, yr_ref, s_scr, prev_scr)
    m_progs, m_finish = _mlstm_programs(pm_ref, cw_ref, cb_ref, gb_ref, hn_ref, ym_ref, ext_scr, c_scr, n_scr,
                                        m_scr)
    pending = list(r_progs[:1])
    for r_prog, m_prog in zip(r_progs[1:], m_progs):
        pending += [r_prog, m_prog]
    pending += m_progs[len(r_progs) - 1:]
    active = []
    tick = 0
    while pending or active:
        if pending and tick % MIX_SKEW == 0:
            active.append(pending.pop(0))
        for gen in list(active):
            try:
                next(gen)
            except StopIteration:
                active.remove(gen)
        tick += 1
    r_finish()
    m_finish()


def _mixers(pr, pm, mu, w0, w_up, a0, a_up, g_up, k_k, k_a, r_k, gn_w, gn_b, conv_w, conv_b, i_b, f_b, hn_w, rows):
    bsz, seq, cols = pr.shape
    mcols = pm.shape[2]
    row = lambda p: p.reshape(1, -1)
    full = lambda a: pl.BlockSpec(a.shape, lambda b, c: (0,) * a.ndim)
    tok = lambda n: pl.BlockSpec((None, rows, n), lambda b, c: (b, c, 0))
    lane_head = jnp.arange(RWKV_WIDTH) // RWKV_HEAD
    same_head = (lane_head[:, None] == lane_head[None, :]).astype(BF16)
    gate_b = jnp.zeros((1, LANES), F32).at[0, :MLSTM_HEADS].set(i_b).at[0, MLSTM_HEADS:2 * MLSTM_HEADS].set(f_b)
    r_params = [row(mu), row(w0), w_up, row(a0), a_up, g_up, row(k_k), row(k_a), row(r_k), row(gn_w), row(gn_b),
                same_head]
    m_params = [conv_w, row(conv_b), gate_b, row(hn_w)]
    return pl.pallas_call(
        _mixers_kernel,
        grid=(bsz, seq // rows),
        in_specs=[tok(cols)] + [full(a) for a in r_params] + [tok(mcols)] + [full(a) for a in m_params],
        out_specs=[tok(RWKV_WIDTH), tok(MLSTM_WIDTH)],
        out_shape=[jax.ShapeDtypeStruct((bsz, seq, RWKV_WIDTH), F32),
                   jax.ShapeDtypeStruct((bsz, seq, MLSTM_WIDTH), F32)],
        scratch_shapes=[pltpu.VMEM((RWKV_HEADS, RWKV_HEAD, RWKV_HEAD), F32),
                        pltpu.VMEM((1, cols), F32),
                        pltpu.VMEM((rows + 8, 2 * MLSTM_QK), F32),
                        pltpu.VMEM((MLSTM_HEADS, MLSTM_DK, MLSTM_DV), F32),
                        pltpu.VMEM((MLSTM_HEADS, 8, LANES), F32),
                        pltpu.VMEM((MLSTM_HEADS, 8, LANES), F32)],
        compiler_params=_params("arbitrary", "arbitrary"),
        name="mixers",
    )(pr, *r_params, pm, *m_params)


def _mix_out_kernel(x_ref, yr_ref, ym_ref, mod_ref, nw_ref, wor_ref, wom_ref, wr_ref, br_ref, x1_ref, hx_ref):
    d = x_ref.shape[-1]
    y = (jnp.dot(yr_ref[...].astype(BF16), wor_ref[...], preferred_element_type=F32)
         + jnp.dot(ym_ref[...].astype(BF16), wom_ref[...], preferred_element_type=F32))
    x1 = x_ref[...] + mod_ref[2:3, :] * y
    x1_ref[...] = x1
    h = _norm_mod(x1, nw_ref[...], mod_ref[3:4, :], mod_ref[4:5, :])
    h_hi = h.astype(BF16)
    hx_ref[:, :d] = h_hi

    h_lo = (h - h_hi.astype(F32)).astype(BF16)
    w = wr_ref[...]
    w_hi = w.astype(BF16)
    w_lo = (w - w_hi.astype(F32)).astype(BF16)
    hw = jnp.dot(h_hi, jnp.concatenate([w_hi, w_lo], axis=1), preferred_element_type=F32)
    logits = hw[:, :LANES] + hw[:, LANES:] + jnp.dot(h_lo, w_hi, preferred_element_type=F32) + br_ref[...]
    lane = lax.broadcasted_iota(jnp.int32, logits.shape, 1)
    neg = -jnp.inf
    el = logits
    gl = jnp.where((lane >= N_EXPERTS) & (lane < N_EXPERTS + N_GROUPS), logits, neg)
    g_max = jnp.max(gl, axis=-1, keepdims=True)
    g_idx = jnp.min(jnp.where(gl == g_max, lane, LANES), axis=-1, keepdims=True) - N_EXPERTS
    g_w = 1.0 / jnp.sum(jnp.exp(gl - g_max), axis=-1, keepdims=True)
    in_group = ((lane >> 3) == g_idx) & (lane < N_EXPERTS)
    sel = jnp.where(in_group, el, neg)
    v1 = jnp.max(sel, axis=-1, keepdims=True)
    i1 = jnp.min(jnp.where(sel == v1, lane, LANES), axis=-1, keepdims=True)
    rest = jnp.where(lane == i1, neg, sel)
    v2 = jnp.max(rest, axis=-1, keepdims=True)
    i2 = jnp.min(jnp.where(rest == v2, lane, LANES), axis=-1, keepdims=True)
    e2 = jnp.exp(v2 - v1)
    w1 = 1.0 / (1.0 + e2)
    w2 = e2 / (1.0 + e2)
    cw = jnp.where(lane == i1, w1 * g_w, jnp.where(lane == i2, w2 * g_w, 0.0))
    hi = cw.astype(BF16).astype(F32)
    mid = (cw - hi).astype(BF16).astype(F32)
    lo = (cw - hi - mid).astype(BF16).astype(F32)
    flag = jnp.where(hi != 0.0, 1.0, 0.0)
    pack = (hi + pltpu.roll(mid, N_EXPERTS, axis=1) + pltpu.roll(lo, 2 * N_EXPERTS, axis=1)
            + pltpu.roll(flag, 3 * N_EXPERTS, axis=1))
    hx_ref[:, d:] = pack.astype(BF16)


def _mix_out(x, y_r, y_m, mod, nw, wo_r, wo_m, w_route, b_route, tm):
    bsz, seq, d = x.shape
    tok = lambda n: pl.BlockSpec((None, tm, n), lambda b, i: (b, i, 0))
    full = lambda a: pl.BlockSpec(a.shape, lambda b, i: (0, 0))
    return pl.pallas_call(
        _mix_out_kernel,
        grid=(bsz, seq // tm),
        in_specs=[tok(d), tok(RWKV_WIDTH), tok(MLSTM_WIDTH),
                  pl.BlockSpec((None, 6, d), lambda b, i: (b, 0, 0)),
                  full(nw), full(wo_r), full(wo_m), full(w_route), full(b_route)],
        out_specs=[tok(d), tok(d + LANES)],
        out_shape=[jax.ShapeDtypeStruct((bsz, seq, d), F32),
                   jax.ShapeDtypeStruct((bsz, seq, d + LANES), BF16)],
        compiler_params=_params("arbitrary", "arbitrary"),
        name="mix_out",
    )(x, y_r, y_m, mod, nw, wo_r, wo_m, w_route, b_route)


ROW_CHUNK = 16
SORT_TILE = 512
EXPERT_TILE = 512
NO_SLOT = 1e9


def _moe_plan(flags, tms, ts):
    t, ne = flags.shape
    nt = t // tms
    i32 = jnp.int32
    tri = lambda n: (jnp.arange(n)[:, None] < jnp.arange(n)[None, :]).astype(F32)
    before = lambda a, b: jnp.dot(a, b, precision=HIGHEST).astype(i32)
    cnt = flags.reshape(nt, tms, ne).astype(i32).sum(axis=1)
    pc = (cnt + ROW_CHUNK - 1) // ROW_CHUNK * ROW_CHUNK
    pcf = pc.astype(F32)
    loc_start = before(pcf, tri(ne))
    len_e = pc.sum(axis=0)
    seg_e = (len_e + ts - 1) // ts * ts
    e_start = before(seg_e.astype(F32)[None, :], tri(ne))[0]
    e_end = e_start + seg_e
    dst_start = e_start[None, :] + before(tri(nt).T, pcf)
    n_tiles_max = _moe_max_rows(t, ne, tms, ts) // ts
    tile_row = jnp.arange(n_tiles_max, dtype=i32) * ts
    tile_expert = jnp.minimum((tile_row[:, None] >= e_end[None, :]).astype(i32).sum(axis=1), ne - 1)
    max_chunks = _local_rows(tms) // ROW_CHUNK
    chunk_row = jnp.arange(max_chunks, dtype=i32)[None, :, None] * ROW_CHUNK
    run_end = (loc_start + pc)[:, None, :]
    run_of = jnp.minimum((chunk_row >= run_end).astype(i32).sum(axis=2), ne - 1)
    run_pick = run_of[:, :, None] == jnp.arange(ne, dtype=i32)[None, None, :]
    chunk_dst = jnp.where(run_pick, (dst_start - loc_start)[:, None, :], 0).sum(axis=2) + chunk_row[:, :, 0]
    return dict(
        chunk_dst=chunk_dst.reshape(-1).astype(i32), tile_chunks=(pc.sum(axis=1) // ROW_CHUNK).astype(i32),
        tail_start=(e_start + len_e).astype(i32),
        tail_chunks=((seg_e - len_e) // ROW_CHUNK).astype(i32),
        n_used=(e_end[-1:] // ts).astype(i32), tile_expert=tile_expert,
        loc_start_vec=jnp.pad(loc_start.astype(F32), ((0, 0), (0, LANES - ne))).reshape(nt, 1, LANES))


def _moe_max_rows(t, ne, tms, ts):
    rows = 2 * t + (t // tms) * ne * (ROW_CHUNK - 1) + ne * (ts - 1)
    return (rows + ts - 1) // ts * ts


def _local_rows(tms):
    return 2 * tms + N_EXPERTS * ROW_CHUNK


def _tile_slots(blk, loc_start_vec):
    tms = blk.shape[0]
    lane = lax.broadcasted_iota(jnp.int32, blk.shape, 1)
    m = jnp.where(lane < N_EXPERTS, pltpu.roll(blk, N_EXPERTS, axis=1), 0.0)
    row = lax.broadcasted_iota(jnp.int32, (tms, tms), 0)
    col = lax.broadcasted_iota(jnp.int32, (tms, tms), 1)
    before = jnp.dot((col < row).astype(BF16), m.astype(BF16), preferred_element_type=F32)
    slot = loc_start_vec + before
    slot_a = jnp.min(jnp.where(m > 0.0, slot, NO_SLOT), axis=-1, keepdims=True)
    slot_b = jnp.max(jnp.where(m > 0.0, slot, -NO_SLOT), axis=-1, keepdims=True)
    return slot_a, slot_b


def _moe_sort_kernel(cd_ref, tc_ref, tls_ref, tlc_ref, nu_ref, hx_ref, lsv_ref, xo_ref, xs_scr, z_scr,
                     sems, zsem):
    i = pl.program_id(0)
    nt = pl.num_programs(0)
    tms, width = hx_ref.shape
    loc = xs_scr.shape[1]
    slot = i % 2
    d = width - LANES
    slot_a, slot_b = _tile_slots(hx_ref[:, d:].astype(F32), lsv_ref[...])
    lane = lax.broadcasted_iota(jnp.int32, (tms, LANES), 1)
    packed = jnp.where(lane == 0, slot_a, jnp.where(lane == 1, slot_b, 0.0))
    ri = lax.broadcasted_iota(jnp.int32, (LANES, LANES), 0)
    ci = lax.broadcasted_iota(jnp.int32, (LANES, LANES), 1)
    slots_t = lax.dot_general((ri == ci).astype(F32), packed, (((1,), (1,)), ((), ())),
                              preferred_element_type=F32, precision=HIGHEST)
    srow = lax.broadcasted_iota(jnp.int32, (loc, tms), 0).astype(F32)
    perm = jnp.where((srow == slots_t[0:1, :]) | (srow == slots_t[1:2, :]), 1.0, 0.0).astype(BF16)
    xs_scr[slot] = jnp.dot(perm, hx_ref[...], preferred_element_type=F32).astype(BF16)

    def copy(src, src_row, dst_row, sem):
        return pltpu.make_async_copy(src.at[pl.ds(pl.multiple_of(src_row, ROW_CHUNK), ROW_CHUNK)],
                                     xo_ref.at[pl.ds(pl.multiple_of(dst_row, ROW_CHUNK), ROW_CHUNK)], sem)

    max_chunks = loc // ROW_CHUNK

    def run_body(j, carry):
        copy(xs_scr.at[slot], j * ROW_CHUNK, cd_ref[i * max_chunks + j], sems.at[slot]).start()
        return carry

    lax.fori_loop(0, tc_ref[i], run_body, 0)

    def wait_runs(step):
        def body(j, carry):
            copy(z_scr, 0, 0, sems.at[step % 2]).wait()
            return carry

        lax.fori_loop(0, tc_ref[step], body, 0)

    @pl.when(i == 0)
    def _():
        z_scr[...] = jnp.zeros_like(z_scr)
        ts = z_scr.shape[0]
        n_tail = jnp.int32(0)
        for e in range(N_EXPERTS):
            n = tlc_ref[e]
            t0 = tls_ref[e]

            def tail_body(j, carry, t0=t0):
                copy(z_scr, 0, t0 + j * ROW_CHUNK, sems.at[2]).start()
                return carry

            lax.fori_loop(0, n, tail_body, 0)
            n_tail = n_tail + n

        def tile_copy(tile):
            return pltpu.make_async_copy(z_scr, xo_ref.at[pl.ds(pl.multiple_of(tile * ts, ts), ts)], zsem)

        n_spare = xo_ref.shape[0] // ts - nu_ref[0]

        def spare_body(j, carry):
            tile_copy(nu_ref[0] + j).start()
            return carry

        lax.fori_loop(0, n_spare, spare_body, 0)

        def tail_wait_body(j, carry):
            copy(z_scr, 0, 0, sems.at[2]).wait()
            return carry

        lax.fori_loop(0, n_tail, tail_wait_body, 0)

        def spare_wait_body(j, carry):
            tile_copy(0).wait()
            return carry

        lax.fori_loop(0, n_spare, spare_wait_body, 0)

    @pl.when(i > 0)
    def _():
        wait_runs(i - 1)

    @pl.when(i == nt - 1)
    def _():
        wait_runs(i)


def _moe_sort(hx, plan, tms, ts):
    t, width = hx.shape
    nt = t // tms
    rows = _moe_max_rows(t, N_EXPERTS, tms, ts)
    return pl.pallas_call(
        _moe_sort_kernel,
        grid_spec=pltpu.PrefetchScalarGridSpec(
            num_scalar_prefetch=5,
            grid=(nt,),
            in_specs=[pl.BlockSpec((tms, width), lambda i, *_: (i, 0)),
                      pl.BlockSpec((None, 1, LANES), lambda i, *_: (i, 0, 0))],
            out_specs=pl.BlockSpec(memory_space=pl.ANY),
            scratch_shapes=[pltpu.VMEM((2, _local_rows(tms), width), BF16),
                            pltpu.VMEM((ts, width), BF16),
                            pltpu.SemaphoreType.DMA((3,)),
                            pltpu.SemaphoreType.DMA(())]),
        out_shape=jax.ShapeDtypeStruct((rows, width), BF16),
        compiler_params=_params("arbitrary"),
        name="moe_sort",
    )(plan["chunk_dst"], plan["tile_chunks"], plan["tail_start"], plan["tail_chunks"], plan["n_used"],
      hx, plan["loc_start_vec"])


def _moe_experts_kernel(te_ref, nu_ref, xs_ref, wg_ref, wu_ref, wd_ref, o_ref):
    j = pl.program_id(0)

    @pl.when(j < nu_ref[0])
    def _():
        d = xs_ref.shape[1] - LANES
        e = te_ref[j]
        x = xs_ref[:, :d]
        wblk = xs_ref[:, d:].astype(F32)
        lane = lax.broadcasted_iota(jnp.int32, wblk.shape, 1)
        mine = (lane == e) | (lane == e + N_EXPERTS) | (lane == e + 2 * N_EXPERTS)
        cw = jnp.sum(jnp.where(mine, wblk, 0.0), axis=-1, keepdims=True)
        hg = jnp.dot(x, wg_ref[...], preferred_element_type=F32)
        hu = jnp.dot(x, wu_ref[...], preferred_element_type=F32)
        act = hg * _sigmoid(hg) * hu * cw
        o_ref[...] = jnp.dot(act.astype(BF16), wd_ref[...], preferred_element_type=F32).astype(BF16)

    @pl.when(j >= nu_ref[0])
    def _():
        o_ref[...] = jnp.zeros_like(o_ref)


def _moe_experts(xs, plan, w_gate, w_up, w_down, ts):
    rows, width = xs.shape
    ne, d, f = w_gate.shape
    tile = lambda j, te, nu: (jnp.minimum(j, nu[0] - 1), 0)
    return pl.pallas_call(
        _moe_experts_kernel,
        grid_spec=pltpu.PrefetchScalarGridSpec(
            num_scalar_prefetch=2,
            grid=(rows // ts,),
            in_specs=[pl.BlockSpec((ts, width), tile),
                      pl.BlockSpec((None, d, f), lambda j, te, nu: (te[j], 0, 0)),
                      pl.BlockSpec((None, d, f), lambda j, te, nu: (te[j], 0, 0)),
                      pl.BlockSpec((None, f, d), lambda j, te, nu: (te[j], 0, 0))],
            out_specs=pl.BlockSpec((ts, d), lambda j, te, nu: (j, 0))),
        out_shape=jax.ShapeDtypeStruct((rows, d), BF16),
        compiler_params=_params("arbitrary"),
        name="moe_experts",
    )(plan["tile_expert"], plan["n_used"], xs, w_gate, w_up, w_down)


def _moe_combine_kernel(cd_ref, tc_ref, wblk_ref, lsv_ref, x1_ref, mod_ref, fw_ref, ys_hbm,
                        o_ref, ys_scr, sems):
    i = pl.program_id(0)
    nt = pl.num_programs(0)
    tms = x1_ref.shape[0]
    loc = ys_scr.shape[1]

    def copy(slot, src_row, dst_row):
        return pltpu.make_async_copy(
            ys_hbm.at[pl.ds(pl.multiple_of(src_row, ROW_CHUNK), ROW_CHUNK)],
            ys_scr.at[slot, pl.ds(pl.multiple_of(dst_row, ROW_CHUNK), ROW_CHUNK)], sems.at[slot])

    max_chunks = loc // ROW_CHUNK

    def fetch(tile, slot):
        def run_body(j, carry):
            copy(slot, cd_ref[tile * max_chunks + j], j * ROW_CHUNK).start()
            return carry

        lax.fori_loop(0, tc_ref[tile], run_body, 0)

    @pl.when(i == 0)
    def _():
        ys_scr[...] = jnp.zeros_like(ys_scr)
        fetch(0, 0)

    slot = i % 2

    @pl.when(i + 1 < nt)
    def _():
        fetch(i + 1, 1 - slot)

    slot_a, slot_b = _tile_slots(wblk_ref[...].astype(F32), lsv_ref[...])
    scol = lax.broadcasted_iota(jnp.int32, (tms, loc), 1).astype(F32)
    perm = jnp.where((scol == slot_a) | (scol == slot_b), 1.0, 0.0).astype(BF16)

    def wait_body(j, carry):
        copy(slot, 0, 0).wait()
        return carry

    lax.fori_loop(0, tc_ref[i], wait_body, 0)
    y = jnp.dot(perm, ys_scr[slot], preferred_element_type=F32)
    x2 = x1_ref[...] + mod_ref[5:6, :] * y
    o_ref[...] = x2 * lax.rsqrt(jnp.mean(x2 * x2, axis=-1, keepdims=True) + EPS) * fw_ref[...]


def _moe_combine(hx, ys, x1, mod, fw, plan, tms, tiles_per_seq):
    t, d = x1.shape
    nt = t // tms
    return pl.pallas_call(
        _moe_combine_kernel,
        grid_spec=pltpu.PrefetchScalarGridSpec(
            num_scalar_prefetch=2,
            grid=(nt,),
            in_specs=[pl.BlockSpec((tms, LANES), lambda i, *_: (i, d // LANES)),
                      pl.BlockSpec((None, 1, LANES), lambda i, *_: (i, 0, 0)),
                      pl.BlockSpec((tms, d), lambda i, *_: (i, 0)),
                      pl.BlockSpec((None, 6, d), lambda i, *_: (i // tiles_per_seq, 0, 0)),
                      pl.BlockSpec((1, d), lambda i, *_: (0, 0)),
                      pl.BlockSpec(memory_space=pl.ANY)],
            out_specs=pl.BlockSpec((tms, d), lambda i, *_: (i, 0)),
            scratch_shapes=[pltpu.VMEM((2, _local_rows(tms), d), BF16),
                            pltpu.SemaphoreType.DMA((2,))]),
        out_shape=jax.ShapeDtypeStruct((t, d), F32),
        compiler_params=_params("arbitrary"),
        name="moe_combine",
    )(plan["chunk_dst"], plan["tile_chunks"], hx, plan["loc_start_vec"], x1, mod, fw, ys)


def _pad_cols(w, n):
    return jnp.pad(w, ((0, 0), (0, n - w.shape[1])))


def kernel(x, c, ada_w, ada_b, mix_norm_w, w_in, rwkv_mu, rwkv_w0, rwkv_w_up, rwkv_a0, rwkv_a_up, rwkv_g_up, rwkv_k_k, rwkv_k_a, rwkv_r_k, rwkv_gn_w, rwkv_gn_b, mlstm_conv_w, mlstm_conv_b, mlstm_i_b, mlstm_f_b, mlstm_hn_w, w_out, ffn_norm_w, moe_w_group, moe_b_group, moe_w_router, moe_b_router, moe_w_gate, moe_w_up, moe_w_down, final_norm_w):
    bsz, seq, d = x.shape
    assert ada_w.shape[0] == 1, "the fused final norm assumes a single layer"
    assert seq % CHUNK == 0
    l = 0
    tm = min(512, seq)
    tms = min(SORT_TILE, seq)
    mod = _ada(c, ada_w[l], ada_b[l]).reshape(bsz, 6, d)
    w_r = w_in[l][:, :RWKV_COLS].astype(BF16)
    w_m = _pad_cols(w_in[l][:, RWKV_COLS:], MLSTM_PAD_COLS).astype(BF16)
    pr, pm = _in_proj(x, mod, mix_norm_w[l].reshape(1, d), w_r, w_m, tm)
    y_r, y_m = _mixers(pr, pm, rwkv_mu[l], rwkv_w0[l], rwkv_w_up[l], rwkv_a0[l], rwkv_a_up[l], rwkv_g_up[l],
                       rwkv_k_k[l], rwkv_k_a[l], rwkv_r_k[l], rwkv_gn_w[l], rwkv_gn_b[l], mlstm_conv_w[l],
                       mlstm_conv_b[l], mlstm_i_b[l], mlstm_f_b[l], mlstm_hn_w[l], min(MIX_STEP_ROWS, seq))
    wo = w_out[l].astype(BF16)
    w_route = _pad_cols(jnp.concatenate([moe_w_router[l], moe_w_group[l]], axis=1), LANES)
    b_route = _pad_cols(jnp.concatenate([moe_b_router[l], moe_b_group[l]]).reshape(1, -1), LANES)
    x1, hx = _mix_out(
        x, y_r, y_m, mod, ffn_norm_w[l].reshape(1, d), wo[:RWKV_WIDTH], wo[RWKV_WIDTH:], w_route, b_route, tm)
    hx = hx.reshape(bsz * seq, d + LANES)
    flags = hx[:, d + 3 * N_EXPERTS:] != 0
    plan = _moe_plan(flags, tms, EXPERT_TILE)
    xs = _moe_sort(hx, plan, tms, EXPERT_TILE)
    ys = _moe_experts(xs, plan, moe_w_gate[l].astype(BF16), moe_w_up[l].astype(BF16),
                      moe_w_down[l].astype(BF16), EXPERT_TILE)
    out = _moe_combine(hx, ys, x1.reshape(bsz * seq, d), mod, final_norm_w.reshape(1, d), plan, tms, seq // tms)
    return out.reshape(bsz, seq, d)
```

```python
import jax
import jax.numpy as jnp
from jax import lax
from jax.experimental import pallas as pl
from jax.experimental.pallas import tpu as pltpu

F32 = jnp.float32
BF16 = jnp.bfloat16
HIGHEST = lax.Precision.HIGHEST

CHUNK = 64
RWKV_CHUNK = 128
NEUMANN_BLOCK = 16
MIX_STEP_ROWS = 512
MLSTM_GROUP_ROWS = 128
MIX_SKEW = 2
EPS = 1e-6
RWKV_HEAD = 64
RWKV_HEADS = 8
RWKV_WIDTH = RWKV_HEADS * RWKV_HEAD
DECAY_LORA = 64
ICLR_LORA = 64
GATE_LORA = 128
GN_EPS = 64e-5
MLSTM_HEADS = 4
MLSTM_DK = 64
MLSTM_DV = 128
MLSTM_QK = MLSTM_HEADS * MLSTM_DK
MLSTM_WIDTH = MLSTM_HEADS * MLSTM_DV
CONV_WIDTH = 4
GATE_SOFTCAP = 15.0
RWKV_COLS = 3 * RWKV_WIDTH + DECAY_LORA + ICLR_LORA + GATE_LORA
N_GROUPS = 4
EXPERTS_PER_GROUP = 8
N_EXPERTS = N_GROUPS * EXPERTS_PER_GROUP
LANES = 128
MLSTM_MAIN = 2 * MLSTM_QK + 2 * MLSTM_WIDTH
MLSTM_PAD_COLS = MLSTM_MAIN + LANES
VMEM_LIMIT = 56 * 1024 * 1024


def _mm(a, b):
    return jnp.dot(a.astype(BF16), b.astype(BF16), preferred_element_type=F32)


def _nt(a, b):
    return lax.dot_general(a.astype(BF16), b.astype(BF16), (((1,), (1,)), ((), ())),
                           preferred_element_type=F32)


def _tn(a, b):
    return lax.dot_general(a.astype(BF16), b.astype(BF16), (((0,), (0,)), ((), ())),
                           preferred_element_type=F32)


def _mm_f32(a, b):
    return jnp.dot(a, b, preferred_element_type=F32, precision=HIGHEST)


def _cumsum_rows(tril, x):
    hi = x.astype(BF16)
    r1 = x - hi.astype(F32)
    mid = r1.astype(BF16)
    lo = (r1 - mid.astype(F32)).astype(BF16)
    n = x.shape[1]
    out = jnp.dot(tril, jnp.concatenate([hi, mid, lo], axis=1), preferred_element_type=F32)
    return out[:, :n] + out[:, n:2 * n] + out[:, 2 * n:]


def _sigmoid(z):
    return 1.0 / (1.0 + jnp.exp(-z))


def _softplus(z):
    return jnp.maximum(z, 0.0) + jnp.log1p(jnp.exp(-jnp.abs(z)))


def _log2(n):
    assert n & (n - 1) == 0
    return n.bit_length() - 1


def _params(*sem):
    return pltpu.CompilerParams(dimension_semantics=sem, vmem_limit_bytes=VMEM_LIMIT)


def _ada_kernel(c_ref, w_ref, b_ref, o_ref):
    c = c_ref[...]
    o_ref[...] = _mm_f32(c * _sigmoid(c), w_ref[...]) + b_ref[...]


def _ada(c, w, b):
    bsz, d = c.shape
    n = w.shape[1]
    return pl.pallas_call(
        _ada_kernel,
        grid=(n // d,),
        in_specs=[pl.BlockSpec((bsz, d), lambda j: (0, 0)),
                  pl.BlockSpec((d, d), lambda j: (0, j)),
                  pl.BlockSpec((1, d), lambda j: (0, j))],
        out_specs=pl.BlockSpec((bsz, d), lambda j: (0, j)),
        out_shape=jax.ShapeDtypeStruct((bsz, n), F32),
        compiler_params=_params("arbitrary"),
        name="ada",
    )(c, w, b.reshape(1, n))


def _norm_mod(x, nw, shift, scale):
    y = x * lax.rsqrt(jnp.mean(x * x, axis=-1, keepdims=True) + EPS) * nw
    return y * (1.0 + scale) + shift


def _in_proj_kernel(x_ref, mod_ref, nw_ref, wr_ref, wm_ref, pr_ref, pm_ref):
    h = _norm_mod(x_ref[...], nw_ref[...], mod_ref[0:1, :], mod_ref[1:2, :]).astype(BF16)
    pr_ref[...] = jnp.dot(h, wr_ref[...], preferred_element_type=F32)
    pm_ref[...] = jnp.dot(h, wm_ref[...], preferred_element_type=F32)


def _in_proj(x, mod, nw, w_r, w_m, tm):
    bsz, seq, d = x.shape
    nr, nm = w_r.shape[1], w_m.shape[1]
    return pl.pallas_call(
        _in_proj_kernel,
        grid=(bsz, seq // tm),
        in_specs=[pl.BlockSpec((None, tm, d), lambda b, i: (b, i, 0)),
                  pl.BlockSpec((None, 6, d), lambda b, i: (b, 0, 0)),
                  pl.BlockSpec((1, d), lambda b, i: (0, 0)),
                  pl.BlockSpec((d, nr), lambda b, i: (0, 0)),
                  pl.BlockSpec((d, nm), lambda b, i: (0, 0))],
        out_specs=[pl.BlockSpec((None, tm, nr), lambda b, i: (b, i, 0)),
                   pl.BlockSpec((None, tm, nm), lambda b, i: (b, i, 0))],
        out_shape=[jax.ShapeDtypeStruct((bsz, seq, nr), F32),
                   jax.ShapeDtypeStruct((bsz, seq, nm), F32)],
        compiler_params=_params("arbitrary", "arbitrary"),
        name="in_proj",
    )(x, mod, nw, w_r, w_m)


def _rwkv_programs(pr_ref, mu_ref, w0_ref, wup_ref, a0_ref, aup_ref, gup_ref, kk_ref, ka_ref,
                   rk_ref, gnw_ref, gnb_ref, hs_ref, o_ref, s_scr, prev_scr):
    C = RWKV_CHUNK
    N = RWKV_HEAD
    H = RWKV_HEADS
    rows = pr_ref.shape[0]
    nc = rows // C

    @pl.when(pl.program_id(1) == 0)
    def _():
        s_scr[...] = jnp.zeros_like(s_scr)
        prev_scr[...] = jnp.zeros_like(prev_scr)

    x = pr_ref[...]
    row1 = lax.broadcasted_iota(jnp.int32, x.shape, 0)
    prev = jnp.where(row1 == 0, prev_scr[...], pltpu.roll(x, 1, axis=0))
    prev_scr[...] = x[rows - 1:rows, :]
    u_all = x + (prev - x) * mu_ref[...]

    row = lax.broadcasted_iota(jnp.int32, (C, C), 0)
    col = lax.broadcasted_iota(jnp.int32, (C, C), 1)
    same = lambda blk: (row >> _log2(blk)) == (col >> _log2(blk))
    eye = (row == col).astype(F32)
    tril = (col <= row).astype(BF16)
    row2 = lax.broadcasted_iota(jnp.int32, (C, 2 * C), 0)
    col2 = lax.broadcasted_iota(jnp.int32, (C, 2 * C), 1) & (C - 1)
    strict2 = col2 < row2
    incl2 = col2 <= row2
    each = lambda f, *ls: [f(*t) for t in zip(*ls)]
    state = [s_scr[h] for h in range(H)]
    o1, o2, o3 = RWKV_WIDTH, 2 * RWKV_WIDTH, 3 * RWKV_WIDTH
    o4 = o3 + DECAY_LORA
    o5 = o4 + ICLR_LORA

    def solve(j):
        u = u_all[j * C:(j + 1) * C, :]
        r, k, v = u[:, :o1], u[:, o1:o2], u[:, o2:o3]
        wd, ad, gd = u[:, o3:o4], u[:, o4:o5], u[:, o5:]
        w_log = -_softplus(-(w0_ref[...] + _mm(jnp.tanh(wd), wup_ref[...]))) - 0.5
        lw = -jnp.exp(w_log)
        a = _sigmoid(a0_ref[...] + _mm(ad, aup_ref[...]))
        g = _mm(_sigmoid(gd), gup_ref[...])
        yield
        kk = k * kk_ref[...]
        k2 = k * (1.0 + (a - 1.0) * ka_ref[...])
        sq = kk * kk
        sq_hi = sq.astype(BF16)
        sq_lo = (sq - sq_hi.astype(F32)).astype(BF16)
        sums = (jnp.dot(sq_hi, hs_ref[...], preferred_element_type=F32)
                + jnp.dot(sq_lo, hs_ref[...], preferred_element_type=F32))
        kk = kk / jnp.maximum(jnp.sqrt(sums), 1e-12)
        rk = r * k2 * rk_ref[...]
        cum = _cumsum_rows(tril, lw)
        yield
        cl = cum[C - 1:C, :]
        p_inv = jnp.exp(-cum)
        p_rest = jnp.exp(cl - cum)
        b = kk * a
        ah_w = -kk * jnp.exp(cum - lw)
        rh_w = r * jnp.exp(cum)
        bt_w, kt_w = b * p_inv, k2 * p_inv
        b2_w, k2p_w = b * p_rest, k2 * p_rest
        p_end_w = jnp.exp(cl)
        cut = lambda t: [t[:, h * N:(h + 1) * N] for h in range(H)]
        ah, rh, bt, kt, b2, k2p, v_h, p_end, rk_h = (
            cut(t) for t in (ah_w, rh_w, bt_w, kt_w, b2_w, k2p_w, v, p_end_w, rk))
        yield
        ar = each(lambda a_, r_: jnp.concatenate([a_, r_], axis=0), ah, rh)
        bk = each(lambda b_, k_: jnp.concatenate([b_, k_], axis=0), bt, kt)
        x1 = each(_nt, ar, bk)
        l_both = each(lambda t: jnp.where(strict2, t[:C, :], 0.0), x1)
        m_both = each(lambda t: jnp.where(incl2, t[C:, :], 0.0), x1)
        l_ab = each(lambda t: t[:, :C], l_both)
        yield
        blk = NEUMANN_BLOCK
        n1 = each(lambda t: jnp.where(same(blk), t, 0.0), l_ab)
        xi = each(lambda t: eye + t, n1)
        nk = n1
        power = 2
        while power < blk:
            nk = each(lambda t: _mm(t, t), nk)
            xi = each(lambda x_, t: x_ + _mm(x_, t), xi, nk)
            power *= 2
            yield
        while blk < C:
            pair = same(2 * blk) & jnp.logical_not(same(blk))
            xe = each(lambda x_, t: _mm(x_, jnp.where(pair, t, 0.0)), xi, l_ab)
            xi = each(lambda x_, t: x_ + _mm(t, x_), xi, xe)
            blk *= 2
            yield
        zero = jnp.zeros((C, N), F32)
        z = each(lambda l_, v_: _mm(l_, jnp.concatenate([zero, v_], axis=0)), l_both, v_h)
        a2 = each(_mm, xi, ah)
        u0 = each(_mm, xi, z)
        yield
        uv = each(lambda u_, v_: jnp.concatenate([u_, v_], axis=0), u0, v_h)
        r2 = each(lambda r_, m_, a_: r_ + _mm(m_[:, :C], a_), rh, m_both, a2)
        y0 = each(_mm, m_both, uv)
        gt = each(_tn, a2, b2)
        ht = each(lambda uv_, b_, k_: _tn(uv_, jnp.concatenate([b_, k_], axis=0)), uv, b2, k2p)
        yield
        y = []
        for h in range(H):
            y.append(_nt(r2[h], state[h]) + y0[h])
            state[h] = state[h] * p_end[h] + _mm(state[h], gt[h]) + ht[h]
        yield
        yc = each(lambda t: t - jnp.mean(t, axis=-1, keepdims=True), y)
        var = each(lambda t: jnp.mean(t * t, axis=-1, keepdims=True), yc)
        rk_sum = each(lambda t: jnp.sum(t, axis=-1, keepdims=True), rk_h)
        yield
        for h in range(H):
            sl = slice(h * N, (h + 1) * N)
            yn = yc[h] * lax.rsqrt(var[h] + GN_EPS) * gnw_ref[:, sl] + gnb_ref[:, sl]
            o_ref[j * C:(j + 1) * C, sl] = (yn + rk_sum[h] * v_h[h]) * g[:, sl]

    def finish():
        for h in range(H):
            s_scr[h] = state[h]

    return [solve(j) for j in range(nc)], finish


def _mlstm_programs(pm_ref, cw_ref, cb_ref, gb_ref, hn_ref, o_ref, ext_scr, c_scr, n_scr, m_scr):
    C = CHUNK
    G = MLSTM_GROUP_ROWS
    H = MLSTM_HEADS
    DK, DV = MLSTM_DK, MLSTM_DV
    QK2 = 2 * MLSTM_QK
    rows = pm_ref.shape[0]

    @pl.when(pl.program_id(1) == 0)
    def _():
        ext_scr[0:8, :] = jnp.zeros((8, QK2), F32)
        c_scr[...] = jnp.zeros_like(c_scr)
        n_scr[...] = jnp.zeros_like(n_scr)
        m_scr[...] = jnp.zeros_like(m_scr)

    ext_scr[8:8 + rows, :] = pm_ref[:, 0:QK2]
    rg = lax.broadcasted_iota(jnp.int32, (G, G), 0)
    cg = lax.broadcasted_iota(jnp.int32, (G, G), 1)
    in_chunk = ((cg <= rg) & ((rg >> _log2(C)) == (cg >> _log2(C)))).astype(BF16)
    lane = lax.broadcasted_iota(jnp.int32, (G, LANES), 1)
    ri = lax.broadcasted_iota(jnp.int32, (LANES, LANES), 0)
    ci = lax.broadcasted_iota(jnp.int32, (LANES, LANES), 1)
    eye = (ri == ci).astype(F32)
    row = lax.broadcasted_iota(jnp.int32, (C, C), 0)
    col = lax.broadcasted_iota(jnp.int32, (C, C), 1)
    incl = col <= row
    each = lambda f, *ls: [f(*t) for t in zip(*ls)]
    m_run = [m_scr[h][0:1, 0:1] for h in range(H)]
    c_run = [c_scr[h] for h in range(H)]
    n_run = [n_scr[h][0:1, 0:DK] for h in range(H)]

    def solve(p):
        r0 = p * G
        conv = cb_ref[...] + cw_ref[0:1, :] * ext_scr[5 + r0:5 + r0 + G, :]
        for i in range(1, CONV_WIDTH):
            conv = conv + cw_ref[i:i + 1, :] * ext_scr[5 + i + r0:5 + i + r0 + G, :]
        qk = conv * _sigmoid(conv)
        q = qk[:, :MLSTM_QK] * (DK ** -0.5)
        k = qk[:, MLSTM_QK:]
        v = pm_ref[r0:r0 + G, QK2:QK2 + MLSTM_WIDTH]
        og = pm_ref[r0:r0 + G, QK2 + MLSTM_WIDTH:MLSTM_MAIN]
        z = GATE_SOFTCAP * jnp.tanh((pm_ref[r0:r0 + G, MLSTM_MAIN:] + gb_ref[...]) / GATE_SOFTCAP)
        log_f = jnp.minimum(z, 0.0) - jnp.log1p(jnp.exp(-jnp.abs(z)))
        gates = jnp.where(lane < H, z, _cumsum_rows(in_chunk, log_f))
        gates_t = lax.dot_general(eye, gates, (((1,), (1,)), ((), ())),
                                  preferred_element_type=F32, precision=HIGHEST)
        yield
        units = [(j, h) for j in range(G // C) for h in range(H)]
        li_col = [gates[j * C:(j + 1) * C, h:h + 1] for j, h in units]
        b_col = [gates[j * C:(j + 1) * C, H + h:H + h + 1] for j, h in units]
        li_row = [gates_t[h:h + 1, j * C:(j + 1) * C] for j, h in units]
        b_row = [gates_t[H + h:H + h + 1, j * C:(j + 1) * C] for j, h in units]
        q_h = [q[j * C:(j + 1) * C, h * DK:(h + 1) * DK] for j, h in units]
        k_h = [k[j * C:(j + 1) * C, h * DK:(h + 1) * DK] for j, h in units]
        v_h = [v[j * C:(j + 1) * C, h * DV:(h + 1) * DV] for j, h in units]
        b_last = each(lambda bc: bc[C - 1:C, :], b_col)
        end_log = each(lambda bl, bc, lc: bl - bc + lc, b_last, b_col, li_col)
        end_max = each(lambda el: jnp.max(el, axis=0, keepdims=True), end_log)
        m, m_new = [], []
        for i, (j, h) in enumerate(units):
            m.append(m_run[h])
            m_run[h] = jnp.maximum(b_last[i] + m_run[h], end_max[i])
            m_new.append(m_run[h])
        yield
        qk_h = each(_nt, q_h, k_h)
        d_mat = each(lambda bc, br, lr: jnp.where(incl, bc - br + lr, -jnp.inf), b_col, b_row, li_row)
        inter = each(lambda bc, m_: bc + m_, b_col, m)
        m_t = each(lambda i_, d_: jnp.maximum(i_, jnp.max(d_, axis=-1, keepdims=True)), inter, d_mat)
        yield
        scores = each(lambda s_, d_, mt: s_ * jnp.exp(d_ - mt), qk_h, d_mat, m_t)
        w_inter = each(lambda i_, mt: jnp.exp(i_ - mt), inter, m_t)
        sv_h = each(_mm, scores, v_h)
        carry_w = each(lambda bl, m_, mn: jnp.exp(bl + m_ - mn), b_last, m, m_new)
        kw = each(lambda k_, el, mn: k_ * jnp.exp(el - mn), k_h, end_log, m_new)
        kv_h = each(_tn, kw, v_h)
        yield
        c_mat, n_vec = [], []
        for i, (j, h) in enumerate(units):
            c_mat.append(c_run[h])
            n_vec.append(n_run[h])
            c_run[h] = carry_w[i] * c_run[h] + kv_h[i]
            n_run[h] = carry_w[i] * n_run[h] + jnp.sum(kw[i], axis=0, keepdims=True)
        qc_h = each(_mm, q_h, c_mat)
        yield
        num = each(lambda s_, w_, c_: s_ + w_ * c_, sv_h, w_inter, qc_h)
        s_sum = each(lambda s_: jnp.sum(s_, axis=-1, keepdims=True), scores)
        qn = each(lambda q_, n_: jnp.sum(q_ * n_, axis=-1, keepdims=True), q_h, n_vec)
        den = each(lambda s_, w_, q_: s_ + w_ * q_, s_sum, w_inter, qn)
        hh = each(lambda n_, d_, mt: n_ / jnp.maximum(jnp.abs(d_), jnp.exp(-mt)), num, den, m_t)
        ms = each(lambda t: jnp.mean(t * t, axis=-1, keepdims=True), hh)
        yield
        for i, (j, h) in enumerate(units):
            rs = slice(r0 + j * C, r0 + (j + 1) * C)
            hn = hh[i] * lax.rsqrt(ms[i] + EPS) * hn_ref[:, h * DV:(h + 1) * DV]
            o_ref[rs, h * DV:(h + 1) * DV] = hn * _sigmoid(og[j * C:(j + 1) * C, h * DV:(h + 1) * DV])

    def finish():
        ext_scr[0:8, :] = ext_scr[rows:rows + 8, :]
        for h in range(H):
            c_scr[h] = c_run[h]
            n_scr[h] = jnp.broadcast_to(jnp.concatenate([n_run[h], jnp.zeros((1, LANES - DK), F32)], axis=1),
                                        (8, LANES))
            m_scr[h] = jnp.broadcast_to(m_run[h], (8, LANES))

    return [solve(p) for p in range(rows // G)], finish


def _mixers_kernel(pr_ref, mu_ref, w0_ref, wup_ref, a0_ref, aup_ref, gup_ref, kk_ref, ka_ref, rk_ref, gnw_ref,
                   gnb_ref, hs_ref, pm_ref, cw_ref, cb_ref, gb_ref, hn_ref, yr_ref, ym_ref,
                   s_scr, prev_scr, ext_scr, c_scr, n_scr, m_scr):
    r_progs, r_finish = _rwkv_programs(pr_ref, mu_ref, w0_ref, wup_ref, a0_ref, aup_ref, gup_ref, kk_ref, ka_ref,
                                       rk_ref, gnw_ref, gnb_ref, hs_ref, yr_ref, s_scr, prev_scr)
    m_progs, m_finish = _mlstm_programs(pm_ref, cw_ref, cb_ref, gb_ref, hn_ref, ym_ref, ext_scr, c_scr, n_scr,
                                        m_scr)
    pending = list(r_progs[:1])
    for r_prog, m_prog in zip(r_progs[1:], m_progs):
        pending += [r_prog, m_prog]
    pending += m_progs[len(r_progs) - 1:]
    active = []
    tick = 0
    while pending or active:
        if pending and tick % MIX_SKEW == 0:
            active.append(pending.pop(0))
        for gen in list(active):
            try:
                next(gen)
            except StopIteration:
                active.remove(gen)
        tick += 1
    r_finish()
    m_finish()


def _mixers(pr, pm, mu, w0, w_up, a0, a_up, g_up, k_k, k_a, r_k, gn_w, gn_b, conv_w, conv_b, i_b, f_b, hn_w, rows):
    bsz, seq, cols = pr.shape
    mcols = pm.shape[2]
    row = lambda p: p.reshape(1, -1)
    full = lambda a: pl.BlockSpec(a.shape, lambda b, c: (0,) * a.ndim)
    tok = lambda n: pl.BlockSpec((None, rows, n), lambda b, c: (b, c, 0))
    lane_head = jnp.arange(RWKV_WIDTH) // RWKV_HEAD
    same_head = (lane_head[:, None] == lane_head[None, :]).astype(BF16)
    gate_b = jnp.zeros((1, LANES), F32).at[0, :MLSTM_HEADS].set(i_b).at[0, MLSTM_HEADS:2 * MLSTM_HEADS].set(f_b)
    r_params = [row(mu), row(w0), w_up, row(a0), a_up, g_up, row(k_k), row(k_a), row(r_k), row(gn_w), row(gn_b),
                same_head]
    m_params = [conv_w, row(conv_b), gate_b, row(hn_w)]
    return pl.pallas_call(
        _mixers_kernel,
        grid=(bsz, seq // rows),
        in_specs=[tok(cols)] + [full(a) for a in r_params] + [tok(mcols)] + [full(a) for a in m_params],
        out_specs=[tok(RWKV_WIDTH), tok(MLSTM_WIDTH)],
        out_shape=[jax.ShapeDtypeStruct((bsz, seq, RWKV_WIDTH), F32),
                   jax.ShapeDtypeStruct((bsz, seq, MLSTM_WIDTH), F32)],
        scratch_shapes=[pltpu.VMEM((RWKV_HEADS, RWKV_HEAD, RWKV_HEAD), F32),
                        pltpu.VMEM((1, cols), F32),
                        pltpu.VMEM((rows + 8, 2 * MLSTM_QK), F32),
                        pltpu.VMEM((MLSTM_HEADS, MLSTM_DK, MLSTM_DV), F32),
                        pltpu.VMEM((MLSTM_HEADS, 8, LANES), F32),
                        pltpu.VMEM((MLSTM_HEADS, 8, LANES), F32)],
        compiler_params=_params("arbitrary", "arbitrary"),
        name="mixers",
    )(pr, *r_params, pm, *m_params)


def _mix_out_kernel(x_ref, yr_ref, ym_ref, mod_ref, nw_ref, wor_ref, wom_ref, wr_ref, br_ref, x1_ref, hx_ref):
    d = x_ref.shape[-1]
    y = (jnp.dot(yr_ref[...].astype(BF16), wor_ref[...], preferred_element_type=F32)
         + jnp.dot(ym_ref[...].astype(BF16), wom_ref[...], preferred_element_type=F32))
    x1 = x_ref[...] + mod_ref[2:3, :] * y
    x1_ref[...] = x1
    h = _norm_mod(x1, nw_ref[...], mod_ref[3:4, :], mod_ref[4:5, :])
    h_hi = h.astype(BF16)
    hx_ref[:, :d] = h_hi

    h_lo = (h - h_hi.astype(F32)).astype(BF16)
    w = wr_ref[...]
    w_hi = w.astype(BF16)
    w_lo = (w - w_hi.astype(F32)).astype(BF16)
    hw = jnp.dot(h_hi, jnp.concatenate([w_hi, w_lo], axis=1), preferred_element_type=F32)
    logits = hw[:, :LANES] + hw[:, LANES:] + jnp.dot(h_lo, w_hi, preferred_element_type=F32) + br_ref[...]
    lane = lax.broadcasted_iota(jnp.int32, logits.shape, 1)
    neg = -jnp.inf
    el = logits
    gl = jnp.where((lane >= N_EXPERTS) & (lane < N_EXPERTS + N_GROUPS), logits, neg)
    g_max = jnp.max(gl, axis=-1, keepdims=True)
    g_idx = jnp.min(jnp.where(gl == g_max, lane, LANES), axis=-1, keepdims=True) - N_EXPERTS
    g_w = 1.0 / jnp.sum(jnp.exp(gl - g_max), axis=-1, keepdims=True)
    in_group = ((lane >> 3) == g_idx) & (lane < N_EXPERTS)
    sel = jnp.where(in_group, el, neg)
    v1 = jnp.max(sel, axis=-1, keepdims=True)
    i1 = jnp.min(jnp.where(sel == v1, lane, LANES), axis=-1, keepdims=True)
    rest = jnp.where(lane == i1, neg, sel)
    v2 = jnp.max(rest, axis=-1, keepdims=True)
    i2 = jnp.min(jnp.where(rest == v2, lane, LANES), axis=-1, keepdims=True)
    e2 = jnp.exp(v2 - v1)
    w1 = 1.0 / (1.0 + e2)
    w2 = e2 / (1.0 + e2)
    cw = jnp.where(lane == i1, w1 * g_w, jnp.where(lane == i2, w2 * g_w, 0.0))
    hi = cw.astype(BF16).astype(F32)
    mid = (cw - hi).astype(BF16).astype(F32)
    lo = (cw - hi - mid).astype(BF16).astype(F32)
    flag = jnp.where(hi != 0.0, 1.0, 0.0)
    pack = (hi + pltpu.roll(mid, N_EXPERTS, axis=1) + pltpu.roll(lo, 2 * N_EXPERTS, axis=1)
            + pltpu.roll(flag, 3 * N_EXPERTS, axis=1))
    hx_ref[:, d:] = pack.astype(BF16)


def _mix_out(x, y_r, y_m, mod, nw, wo_r, wo_m, w_route, b_route, tm):
    bsz, seq, d = x.shape
    tok = lambda n: pl.BlockSpec((None, tm, n), lambda b, i: (b, i, 0))
    full = lambda a: pl.BlockSpec(a.shape, lambda b, i: (0, 0))
    return pl.pallas_call(
        _mix_out_kernel,
        grid=(bsz, seq // tm),
        in_specs=[tok(d), tok(RWKV_WIDTH), tok(MLSTM_WIDTH),
                  pl.BlockSpec((None, 6, d), lambda b, i: (b, 0, 0)),
                  full(nw), full(wo_r), full(wo_m), full(w_route), full(b_route)],
        out_specs=[tok(d), tok(d + LANES)],
        out_shape=[jax.ShapeDtypeStruct((bsz, seq, d), F32),
                   jax.ShapeDtypeStruct((bsz, seq, d + LANES), BF16)],
        compiler_params=_params("arbitrary", "arbitrary"),
        name="mix_out",
    )(x, y_r, y_m, mod, nw, wo_r, wo_m, w_route, b_route)


ROW_CHUNK = 16
SORT_TILE = 512
EXPERT_TILE = 1024
NO_SLOT = 1e9


def _moe_plan(flags, tms, ts):
    t, ne = flags.shape
    nt = t // tms
    i32 = jnp.int32
    tri = lambda n: (jnp.arange(n)[:, None] < jnp.arange(n)[None, :]).astype(F32)
    before = lambda a, b: jnp.dot(a, b, precision=HIGHEST).astype(i32)
    cnt = flags.reshape(nt, tms, ne).astype(i32).sum(axis=1)
    pc = (cnt + ROW_CHUNK - 1) // ROW_CHUNK * ROW_CHUNK
    pcf = pc.astype(F32)
    loc_start = before(pcf, tri(ne))
    len_e = pc.sum(axis=0)
    seg_e = (len_e + ts - 1) // ts * ts
    e_start = before(seg_e.astype(F32)[None, :], tri(ne))[0]
    e_end = e_start + seg_e
    dst_start = e_start[None, :] + before(tri(nt).T, pcf)
    n_tiles_max = _moe_max_rows(t, ne, tms, ts) // ts
    tile_row = jnp.arange(n_tiles_max, dtype=i32) * ts
    tile_expert = jnp.minimum((tile_row[:, None] >= e_end[None, :]).astype(i32).sum(axis=1), ne - 1)
    max_chunks = _local_rows(tms) // ROW_CHUNK
    chunk_row = jnp.arange(max_chunks, dtype=i32)[None, :, None] * ROW_CHUNK
    run_end = (loc_start + pc)[:, None, :]
    run_of = jnp.minimum((chunk_row >= run_end).astype(i32).sum(axis=2), ne - 1)
    run_pick = run_of[:, :, None] == jnp.arange(ne, dtype=i32)[None, None, :]
    chunk_dst = jnp.where(run_pick, (dst_start - loc_start)[:, None, :], 0).sum(axis=2) + chunk_row[:, :, 0]
    return dict(
        chunk_dst=chunk_dst.reshape(-1).astype(i32), tile_chunks=(pc.sum(axis=1) // ROW_CHUNK).astype(i32),
        tail_start=(e_start + len_e).astype(i32),
        tail_chunks=((seg_e - len_e) // ROW_CHUNK).astype(i32),
        n_used=(e_end[-1:] // ts).astype(i32), tile_expert=tile_expert,
        loc_start_vec=jnp.pad(loc_start.astype(F32), ((0, 0), (0, LANES - ne))).reshape(nt, 1, LANES))


def _moe_max_rows(t, ne, tms, ts):
    rows = 2 * t + (t // tms) * ne * (ROW_CHUNK - 1) + ne * (ts - 1)
    return (rows + ts - 1) // ts * ts


def _local_rows(tms):
    return 2 * tms + N_EXPERTS * ROW_CHUNK


def _tile_slots(blk, loc_start_vec):
    tms = blk.shape[0]
    lane = lax.broadcasted_iota(jnp.int32, blk.shape, 1)
    m = jnp.where(lane < N_EXPERTS, pltpu.roll(blk, N_EXPERTS, axis=1), 0.0)
    row = lax.broadcasted_iota(jnp.int32, (tms, tms), 0)
    col = lax.broadcasted_iota(jnp.int32, (tms, tms), 1)
    before = jnp.dot((col < row).astype(BF16), m.astype(BF16), preferred_element_type=F32)
    slot = loc_start_vec + before
    slot_a = jnp.min(jnp.where(m > 0.0, slot, NO_SLOT), axis=-1, keepdims=True)
    slot_b = jnp.max(jnp.where(m > 0.0, slot, -NO_SLOT), axis=-1, keepdims=True)
    return slot_a, slot_b


def _moe_sort_kernel(cd_ref, tc_ref, tls_ref, tlc_ref, nu_ref, hx_ref, lsv_ref, xo_ref, xs_scr, z_scr,
                     sems, zsem):
    i = pl.program_id(0)
    nt = pl.num_programs(0)
    tms, width = hx_ref.shape
    loc = xs_scr.shape[1]
    slot = i % 2
    d = width - LANES
    slot_a, slot_b = _tile_slots(hx_ref[:, d:].astype(F32), lsv_ref[...])
    lane = lax.broadcasted_iota(jnp.int32, (tms, LANES), 1)
    packed = jnp.where(lane == 0, slot_a, jnp.where(lane == 1, slot_b, 0.0))
    ri = lax.broadcasted_iota(jnp.int32, (LANES, LANES), 0)
    ci = lax.broadcasted_iota(jnp.int32, (LANES, LANES), 1)
    slots_t = lax.dot_general((ri == ci).astype(F32), packed, (((1,), (1,)), ((), ())),
                              preferred_element_type=F32, precision=HIGHEST)
    srow = lax.broadcasted_iota(jnp.int32, (loc, tms), 0).astype(F32)
    perm = jnp.where((srow == slots_t[0:1, :]) | (srow == slots_t[1:2, :]), 1.0, 0.0).astype(BF16)
    xs_scr[slot] = jnp.dot(perm, hx_ref[...], preferred_element_type=F32).astype(BF16)

    def copy(src, src_row, dst_row, sem):
        return pltpu.make_async_copy(src.at[pl.ds(pl.multiple_of(src_row, ROW_CHUNK), ROW_CHUNK)],
                                     xo_ref.at[pl.ds(pl.multiple_of(dst_row, ROW_CHUNK), ROW_CHUNK)], sem)

    max_chunks = loc // ROW_CHUNK

    def run_body(j, carry):
        copy(xs_scr.at[slot], j * ROW_CHUNK, cd_ref[i * max_chunks + j], sems.at[slot]).start()
        return carry

    lax.fori_loop(0, tc_ref[i], run_body, 0)

    def wait_runs(step):
        def body(j, carry):
            copy(z_scr, 0, 0, sems.at[step % 2]).wait()
            return carry

        lax.fori_loop(0, tc_ref[step], body, 0)

    @pl.when(i == 0)
    def _():
        z_scr[...] = jnp.zeros_like(z_scr)
        ts = z_scr.shape[0]
        n_tail = jnp.int32(0)
        for e in range(N_EXPERTS):
            n = tlc_ref[e]
            t0 = tls_ref[e]

            def tail_body(j, carry, t0=t0):
                copy(z_scr, 0, t0 + j * ROW_CHUNK, sems.at[2]).start()
                return carry

            lax.fori_loop(0, n, tail_body, 0)
            n_tail = n_tail + n

        def tile_copy(tile):
            return pltpu.make_async_copy(z_scr, xo_ref.at[pl.ds(pl.multiple_of(tile * ts, ts), ts)], zsem)

        n_spare = xo_ref.shape[0] // ts - nu_ref[0]

        def spare_body(j, carry):
            tile_copy(nu_ref[0] + j).start()
            return carry

        lax.fori_loop(0, n_spare, spare_body, 0)

        def tail_wait_body(j, carry):
            copy(z_scr, 0, 0, sems.at[2]).wait()
            return carry

        lax.fori_loop(0, n_tail, tail_wait_body, 0)

        def spare_wait_body(j, carry):
            tile_copy(0).wait()
            return carry

        lax.fori_loop(0, n_spare, spare_wait_body, 0)

    @pl.when(i > 0)
    def _():
        wait_runs(i - 1)

    @pl.when(i == nt - 1)
    def _():
        wait_runs(i)


def _moe_sort(hx, plan, tms, ts):
    t, width = hx.shape
    nt = t // tms
    rows = _moe_max_rows(t, N_EXPERTS, tms, ts)
    return pl.pallas_call(
        _moe_sort_kernel,
        grid_spec=pltpu.PrefetchScalarGridSpec(
            num_scalar_prefetch=5,
            grid=(nt,),
            in_specs=[pl.BlockSpec((tms, width), lambda i, *_: (i, 0)),
                      pl.BlockSpec((None, 1, LANES), lambda i, *_: (i, 0, 0))],
            out_specs=pl.BlockSpec(memory_space=pl.ANY),
            scratch_shapes=[pltpu.VMEM((2, _local_rows(tms), width), BF16),
                            pltpu.VMEM((ts, width), BF16),
                            pltpu.SemaphoreType.DMA((3,)),
                            pltpu.SemaphoreType.DMA(())]),
        out_shape=jax.ShapeDtypeStruct((rows, width), BF16),
        compiler_params=_params("arbitrary"),
        name="moe_sort",
    )(plan["chunk_dst"], plan["tile_chunks"], plan["tail_start"], plan["tail_chunks"], plan["n_used"],
      hx, plan["loc_start_vec"])


def _moe_experts_kernel(te_ref, nu_ref, xs_ref, wg_ref, wu_ref, wd_ref, o_ref):
    j = pl.program_id(0)

    @pl.when(j < nu_ref[0])
    def _():
        d = xs_ref.shape[1] - LANES
        e = te_ref[j]
        x = xs_ref[:, :d]
        wblk = xs_ref[:, d:].astype(F32)
        lane = lax.broadcasted_iota(jnp.int32, wblk.shape, 1)
        mine = (lane == e) | (lane == e + N_EXPERTS) | (lane == e + 2 * N_EXPERTS)
        cw = jnp.sum(jnp.where(mine, wblk, 0.0), axis=-1, keepdims=True)
        hg = jnp.dot(x, wg_ref[...], preferred_element_type=F32)
        hu = jnp.dot(x, wu_ref[...], preferred_element_type=F32)
        act = hg * _sigmoid(hg) * hu * cw
        o_ref[...] = jnp.dot(act.astype(BF16), wd_ref[...], preferred_element_type=F32).astype(BF16)

    @pl.when(j >= nu_ref[0])
    def _():
        o_ref[...] = jnp.zeros_like(o_ref)


def _moe_experts(xs, plan, w_gate, w_up, w_down, ts):
    rows, width = xs.shape
    ne, d, f = w_gate.shape
    tile = lambda j, te, nu: (jnp.minimum(j, nu[0] - 1), 0)
    return pl.pallas_call(
        _moe_experts_kernel,
        grid_spec=pltpu.PrefetchScalarGridSpec(
            num_scalar_prefetch=2,
            grid=(rows // ts,),
            in_specs=[pl.BlockSpec((ts, width), tile),
                      pl.BlockSpec((None, d, f), lambda j, te, nu: (te[j], 0, 0)),
                      pl.BlockSpec((None, d, f), lambda j, te, nu: (te[j], 0, 0)),
                      pl.BlockSpec((None, f, d), lambda j, te, nu: (te[j], 0, 0))],
            out_specs=pl.BlockSpec((ts, d), lambda j, te, nu: (j, 0))),
        out_shape=jax.ShapeDtypeStruct((rows, d), BF16),
        compiler_params=_params("arbitrary"),
        name="moe_experts",
    )(plan["tile_expert"], plan["n_used"], xs, w_gate, w_up, w_down)


def _moe_combine_kernel(cd_ref, tc_ref, wblk_ref, lsv_ref, x1_ref, mod_ref, fw_ref, ys_hbm,
                        o_ref, ys_scr, sems):
    i = pl.program_id(0)
    nt = pl.num_programs(0)
    tms = x1_ref.shape[0]
    loc = ys_scr.shape[1]

    def copy(slot, src_row, dst_row):
        return pltpu.make_async_copy(
            ys_hbm.at[pl.ds(pl.multiple_of(src_row, ROW_CHUNK), ROW_CHUNK)],
            ys_scr.at[slot, pl.ds(pl.multiple_of(dst_row, ROW_CHUNK), ROW_CHUNK)], sems.at[slot])

    max_chunks = loc // ROW_CHUNK

    def fetch(tile, slot):
        def run_body(j, carry):
            copy(slot, cd_ref[tile * max_chunks + j], j * ROW_CHUNK).start()
            return carry

        lax.fori_loop(0, tc_ref[tile], run_body, 0)

    @pl.when(i == 0)
    def _():
        ys_scr[...] = jnp.zeros_like(ys_scr)
        fetch(0, 0)

    slot = i % 2

    @pl.when(i + 1 < nt)
    def _():
        fetch(i + 1, 1 - slot)

    slot_a, slot_b = _tile_slots(wblk_ref[...].astype(F32), lsv_ref[...])
    scol = lax.broadcasted_iota(jnp.int32, (tms, loc), 1).astype(F32)
    perm = jnp.where((scol == slot_a) | (scol == slot_b), 1.0, 0.0).astype(BF16)

    def wait_body(j, carry):
        copy(slot, 0, 0).wait()
        return carry

    lax.fori_loop(0, tc_ref[i], wait_body, 0)
    y = jnp.dot(perm, ys_scr[slot], preferred_element_type=F32)
    x2 = x1_ref[...] + mod_ref[5:6, :] * y
    o_ref[...] = x2 * lax.rsqrt(jnp.mean(x2 * x2, axis=-1, keepdims=True) + EPS) * fw_ref[...]


def _moe_combine(hx, ys, x1, mod, fw, plan, tms, tiles_per_seq):
    t, d = x1.shape
    nt = t // tms
    return pl.pallas_call(
        _moe_combine_kernel,
        grid_spec=pltpu.PrefetchScalarGridSpec(
            num_scalar_prefetch=2,
            grid=(nt,),
            in_specs=[pl.BlockSpec((tms, LANES), lambda i, *_: (i, d // LANES)),
                      pl.BlockSpec((None, 1, LANES), lambda i, *_: (i, 0, 0)),
                      pl.BlockSpec((tms, d), lambda i, *_: (i, 0)),
                      pl.BlockSpec((None, 6, d), lambda i, *_: (i // tiles_per_seq, 0, 0)),
                      pl.BlockSpec((1, d), lambda i, *_: (0, 0)),
                      pl.BlockSpec(memory_space=pl.ANY)],
            out_specs=pl.BlockSpec((tms, d), lambda i, *_: (i, 0)),
            scratch_shapes=[pltpu.VMEM((2, _local_rows(tms), d), BF16),
                            pltpu.SemaphoreType.DMA((2,))]),
        out_shape=jax.ShapeDtypeStruct((t, d), F32),
        compiler_params=_params("arbitrary"),
        name="moe_combine",
    )(plan["chunk_dst"], plan["tile_chunks"], hx, plan["loc_start_vec"], x1, mod, fw, ys)


def _pad_cols(w, n):
    return jnp.pad(w, ((0, 0), (0, n - w.shape[1])))


def kernel(x, c, ada_w, ada_b, mix_norm_w, w_in, rwkv_mu, rwkv_w0, rwkv_w_up, rwkv_a0, rwkv_a_up, rwkv_g_up, rwkv_k_k, rwkv_k_a, rwkv_r_k, rwkv_gn_w, rwkv_gn_b, mlstm_conv_w, mlstm_conv_b, mlstm_i_b, mlstm_f_b, mlstm_hn_w, w_out, ffn_norm_w, moe_w_group, moe_b_group, moe_w_router, moe_b_router, moe_w_gate, moe_w_up, moe_w_down, final_norm_w):
    bsz, seq, d = x.shape
    assert ada_w.shape[0] == 1, "the fused final norm assumes a single layer"
    assert seq % CHUNK == 0
    l = 0
    tm = min(512, seq)
    tms = min(SORT_TILE, seq)
    mod = _ada(c, ada_w[l], ada_b[l]).reshape(bsz, 6, d)
    w_r = w_in[l][:, :RWKV_COLS].astype(BF16)
    w_m = _pad_cols(w_in[l][:, RWKV_COLS:], MLSTM_PAD_COLS).astype(BF16)
    pr, pm = _in_proj(x, mod, mix_norm_w[l].reshape(1, d), w_r, w_m, tm)
    y_r, y_m = _mixers(pr, pm, rwkv_mu[l], rwkv_w0[l], rwkv_w_up[l], rwkv_a0[l], rwkv_a_up[l], rwkv_g_up[l],
                       rwkv_k_k[l], rwkv_k_a[l], rwkv_r_k[l], rwkv_gn_w[l], rwkv_gn_b[l], mlstm_conv_w[l],
                       mlstm_conv_b[l], mlstm_i_b[l], mlstm_f_b[l], mlstm_hn_w[l], min(MIX_STEP_ROWS, seq))
    wo = w_out[l].astype(BF16)
    w_route = _pad_cols(jnp.concatenate([moe_w_router[l], moe_w_group[l]], axis=1), LANES)
    b_route = _pad_cols(jnp.concatenate([moe_b_router[l], moe_b_group[l]]).reshape(1, -1), LANES)
    x1, hx = _mix_out(
        x, y_r, y_m, mod, ffn_norm_w[l].reshape(1, d), wo[:RWKV_WIDTH], wo[RWKV_WIDTH:], w_route, b_route, tm)
    hx = hx.reshape(bsz * seq, d + LANES)
    flags = hx[:, d + 3 * N_EXPERTS:] != 0
    plan = _moe_plan(flags, tms, EXPERT_TILE)
    xs = _moe_sort(hx, plan, tms, EXPERT_TILE)
    ys = _moe_experts(xs, plan, moe_w_gate[l].astype(BF16), moe_w_up[l].astype(BF16),
                      moe_w_down[l].astype(BF16), EXPERT_TILE)
    out = _moe_combine(hx, ys, x1.reshape(bsz * seq, d), mod, final_norm_w.reshape(1, d), plan, tms, seq // tms)
    return out.reshape(bsz, seq, d)
```
